```python
import functools
import numpy as np
import jax, jax.numpy as jnp
from jax import lax

D_MODEL = 1024
BATCH = 2
SEQ = 8192
DEPTH = 4
DEC_BATCH = 128
DEC_SEQ = 4
PAST_LEN = 2048
PAGE_SIZE = 128

CONV_WIDTH = D_MODEL // 2
POOL_WIDTH = D_MODEL // 2
HEAD_DIM = 64
N_HEADS = (D_MODEL // 2) // HEAD_DIM
N_KV_HEADS = 2
GROUP = N_HEADS // N_KV_HEADS
ROT_DIM = HEAD_DIM // 4
ROPE_THETA = 500000.0
CONV_W = 31
POOL_WINDOWS = (2, 4, 8, 16)
POOL_MAX = max(POOL_WINDOWS)
POOL_GC = POOL_WIDTH // len(POOL_WINDOWS)
CMP_BLOCK = 32
CMP_STRIDE = 16
SEL_BLOCK = 64
N_SELECT = 16
WINDOW = 512
Q_BLOCK = 128
D_FF = ((8 * D_MODEL // 3 + 127) // 128) * 128
FFN_CONV_W = 3
PLE_DIM = 256
EPS = 1e-6
FORCE_SCORE = 1e4
NEG = -1e30

SPLIT_SIZES = (2 * CONV_WIDTH, POOL_WIDTH, N_HEADS * HEAD_DIM, 6 * N_KV_HEADS * HEAD_DIM, 3 * N_HEADS, 3 * D_MODEL)
IN_COLS = sum(SPLIT_SIZES)
SPLIT_AT = tuple(int(v) for v in np.cumsum(SPLIT_SIZES)[:-1])

kernel_name = "hybrid_conv_pool_nsa_decoder_step"


def rmsnorm(x, g):
    xf = x.astype(jnp.float32)
    y = xf * lax.rsqrt(jnp.mean(xf * xf, axis=-1, keepdims=True) + EPS)
    return (y * g.astype(jnp.float32)).astype(x.dtype)


def layernorm(x, g, b):
    xf = x.astype(jnp.float32)
    mu = jnp.mean(xf, axis=-1, keepdims=True)
    var = jnp.mean(jnp.square(xf - mu), axis=-1, keepdims=True)
    y = (xf - mu) * lax.rsqrt(var + EPS) * g.astype(jnp.float32) + b.astype(jnp.float32)
    return y.astype(x.dtype)


def rotary(x, pos):
    half = ROT_DIM // 2
    inv = ROPE_THETA ** (-jnp.arange(half, dtype=jnp.float32) * 2.0 / ROT_DIM)
    ang = pos.astype(jnp.float32)[:, None] * inv[None, :]
    ang = ang.reshape((ang.shape[0],) + (1,) * (x.ndim - 3) + (half,))
    cos, sin = jnp.cos(ang), jnp.sin(ang)
    x1 = x[..., :half].astype(jnp.float32)
    x2 = x[..., half:ROT_DIM].astype(jnp.float32)
    rot = jnp.concatenate([x1 * cos - x2 * sin, x2 * cos + x1 * sin], axis=-1).astype(x.dtype)
    return jnp.concatenate([rot, x[..., ROT_DIM:]], axis=-1)


def masked_softmax(s, mask):
    s = jnp.where(mask, s.astype(jnp.float32), NEG)
    return jnp.where(mask, jax.nn.softmax(s, axis=-1), 0.0)


def causal_dwconv(u, prev, w):
    K, C = w.shape
    ext = jnp.concatenate([prev, u], axis=1)
    y = lax.conv_general_dilated(ext, w[:, None, :], window_strides=(1,), padding='VALID',
                                 dimension_numbers=('NWC', 'WIO', 'NWC'), feature_group_count=C)
    return y, ext[:, -(K - 1):]


def pool_mix(u, prev, pos0, w_grp, scale):
    B, T, C = u.shape
    P1 = POOL_MAX - 1
    ext = jnp.concatenate([prev, u], axis=1)
    cs = jnp.concatenate([jnp.zeros((B, 1, C), jnp.float32), jnp.cumsum(ext.astype(jnp.float32), axis=1)], axis=1)
    upto = cs[:, P1 + 1:P1 + 1 + T]
    pos = (pos0 + jnp.arange(T)).astype(jnp.float32)
    means = []
    for g, w in enumerate(POOL_WINDOWS):
        ch = slice(g * POOL_GC, (g + 1) * POOL_GC)
        start = cs[:, P1 + 1 - w:P1 + 1 - w + T, ch]
        cnt = jnp.minimum(pos + 1.0, float(w))[None, :, None]
        means.append((upto[..., ch] - start) / cnt)
    pooled = jnp.concatenate(means, axis=-1)
    d = (pooled - u.astype(jnp.float32)).astype(u.dtype).reshape(B, T, len(POOL_WINDOWS), POOL_GC)
    y = jnp.einsum('btgc,gcd->btgd', d, w_grp).reshape(B, T, C) * scale
    return y, ext[:, -P1:]


def compress(k, pe, w1, w2):
    B, L, G, D = k.shape
    nc = (L - CMP_BLOCK) // CMP_STRIDE + 1
    idx = jnp.arange(nc)[:, None] * CMP_STRIDE + jnp.arange(CMP_BLOCK)[None, :]
    blk = k[:, idx] + pe[None, None, :, None, :]
    flat = blk.transpose(0, 1, 3, 2, 4).reshape(B, nc, G, CMP_BLOCK * D)
    out = jax.nn.gelu(flat @ w1) @ w2
    end = jnp.arange(nc) * CMP_STRIDE + CMP_BLOCK - 1
    return out, end


def to_blocks(k):
    B, L, G, D = k.shape
    ns = -(-L // SEL_BLOCK)
    k = jnp.pad(k, ((0, 0), (0, ns * SEL_BLOCK - L), (0, 0), (0, 0)))
    return k.reshape(B, ns, SEL_BLOCK, G, D).transpose(0, 3, 1, 2, 4)


def nsa_attend(q, ng, pos_q, kc, vc, cmp_end, ks, vs, kw, vw, pos_w):
    B, Tq, G, R, D = q.shape
    scale = D ** -0.5
    s = jnp.einsum('bqgrd,bngd->bgrqn', q, kc).astype(jnp.float32) * scale
    p_c = masked_softmax(s, cmp_end[None, :] <= pos_q[:, None])
    o_c = jnp.einsum('bgrqn,bngd->bqgrd', p_c.astype(vc.dtype), vc)
    nc, ns = kc.shape[1], ks.shape[2]
    owner = (jnp.arange(nc) * CMP_STRIDE) // SEL_BLOCK
    onehot = (owner[:, None] == jnp.arange(ns)[None, :]).astype(jnp.float32)
    imp = jnp.einsum('bgrqn,nj->bgqj', p_c, onehot)
    q_blk = pos_q // SEL_BLOCK
    j = jnp.arange(ns)[None, :]
    forced = (j == 0) | (j == q_blk[:, None]) | (j == q_blk[:, None] - 1)
    allowed = j <= q_blk[:, None]
    imp = jnp.where(allowed, jnp.where(forced, FORCE_SCORE, imp), -1.0)
    n_sel = min(N_SELECT, ns)
    vals, idx = lax.top_k(imp, n_sel)
    bi = jnp.arange(B)[:, None, None, None]
    gi = jnp.arange(G)[None, :, None, None]
    kg = ks[bi, gi, idx]
    vg = vs[bi, gi, idx]
    kpos = idx[..., None] * SEL_BLOCK + jnp.arange(SEL_BLOCK)
    m_s = (vals[..., None] >= 0.0) & (kpos <= pos_q[None, None, :, None, None])
    m_s = m_s.reshape(B, G, 1, Tq, n_sel * SEL_BLOCK)
    s = jnp.einsum('bqgrd,bgqkld->bgrqkl', q, kg).reshape(B, G, R, Tq, n_sel * SEL_BLOCK) * scale
    p_s = masked_softmax(s, m_s)
    o_s = jnp.einsum('bgrqm,bgqmd->bqgrd', p_s.astype(vg.dtype), vg.reshape(B, G, Tq, n_sel * SEL_BLOCK, D))
    m_w = (pos_w[None, :] <= pos_q[:, None]) & (pos_w[None, :] > pos_q[:, None] - WINDOW) & (pos_w[None, :] >= 0)
    s = jnp.einsum('bqgrd,blgd->bgrql', q, kw).astype(jnp.float32) * scale
    p_w = masked_softmax(s, m_w)
    o_w = jnp.einsum('bgrql,blgd->bqgrd', p_w.astype(vw.dtype), vw)
    return ng[..., 0, None] * o_c + ng[..., 1, None] * o_s + ng[..., 2, None] * o_w


def nsa_prompt(q, kv, ng, pe_c, w1, w2):
    B, T = q.shape[:2]
    kc, end = compress(kv[:, :, 0], pe_c[0], w1[0], w2[0])
    vc, _ = compress(kv[:, :, 1], pe_c[1], w1[1], w2[1])
    ks, vs = to_blocks(kv[:, :, 2]), to_blocks(kv[:, :, 3])
    kw = jnp.pad(kv[:, :, 4:6], ((0, 0), (WINDOW, 0), (0, 0), (0, 0), (0, 0)))

    def one_block(i):
        s0 = i * Q_BLOCK
        pos_q = s0 + jnp.arange(Q_BLOCK)
        qb = lax.dynamic_slice_in_dim(q, s0, Q_BLOCK, axis=1)
        gb = lax.dynamic_slice_in_dim(ng, s0, Q_BLOCK, axis=1)
        wb = lax.dynamic_slice_in_dim(kw, s0, WINDOW + Q_BLOCK, axis=1)
        pos_w = s0 - WINDOW + jnp.arange(WINDOW + Q_BLOCK)
        return nsa_attend(qb, gb, pos_q, kc, vc, end, ks, vs, wb[:, :, 0], wb[:, :, 1], pos_w)

    o = lax.map(one_block, jnp.arange(T // Q_BLOCK))
    o = jnp.moveaxis(o, 0, 1).reshape(B, T, N_HEADS * HEAD_DIM)
    wp = min(WINDOW, T)
    return o, kv[:, :, 0:4], kv[:, T - wp:, 4:6]


def nsa_sample(q, kv, ng, pe_c, w1, w2, cache_kv, page_table, win_state):
    B, T = q.shape[:2]
    past_len = page_table.shape[1] * cache_kv.shape[1]
    past = cache_kv[page_table].reshape(B, past_len, 4, N_KV_HEADS, HEAD_DIM)
    full = jnp.concatenate([past, kv[:, :, 0:4]], axis=1)
    kc, end = compress(full[:, :, 0], pe_c[0], w1[0], w2[0])
    vc, _ = compress(full[:, :, 1], pe_c[1], w1[1], w2[1])
    ks, vs = to_blocks(full[:, :, 2]), to_blocks(full[:, :, 3])
    wlen = win_state.shape[1]
    win = jnp.concatenate([win_state, kv[:, :, 4:6]], axis=1)
    pos_w = past_len - wlen + jnp.arange(wlen + T)
    pos_q = past_len + jnp.arange(T)
    o = nsa_attend(q, ng, pos_q, kc, vc, end, ks, vs, win[:, :, 0], win[:, :, 1], pos_w)
    return o.reshape(B, T, N_HEADS * HEAD_DIM), kv[:, :, 0:4], win[:, T:]


def trunk_layer(x, pe_i, prev_conv, prev_pool, prev_ffn, pos0, nsa_fn, prm):
    B, T, _ = x.shape
    pos = pos0 + jnp.arange(T)
    h = rmsnorm(x, prm['norm_mix'])
    z = h @ prm['w_in']
    z_a, z_b, z_q, z_kv, z_ng, z_mg = jnp.split(z, SPLIT_AT, axis=-1)
    a = z_a[..., :CONV_WIDTH] * jax.nn.sigmoid(z_a[..., CONV_WIDTH:])
    c, new_conv = causal_dwconv(a, prev_conv, prm['conv_dw'])
    c = jax.nn.silu(layernorm(c + prm['conv_b'], prm['conv_ln_g'], prm['conv_ln_b']))
    out_a = c @ prm['w_a_out']
    pb, new_pool = pool_mix(z_b, prev_pool, pos0, prm['pool_w'], prm['pool_scale'])
    out_b = pb @ prm['w_b_out']
    q = rotary(z_q.reshape(B, T, N_HEADS, HEAD_DIM), pos).reshape(B, T, N_KV_HEADS, GROUP, HEAD_DIM)
    kv = z_kv.reshape(B, T, 6, N_KV_HEADS, HEAD_DIM)
    keys = rotary(kv[:, :, 0::2], pos)
    kv = jnp.stack([keys, kv[:, :, 1::2]], axis=3).reshape(B, T, 6, N_KV_HEADS, HEAD_DIM)
    ng = jax.nn.sigmoid(z_ng.reshape(B, T, N_KV_HEADS, GROUP, 3))
    o_c, new_kv, new_win = nsa_fn(q, kv, ng)
    out_c = o_c @ prm['w_c_out']
    mg = jax.nn.sigmoid(z_mg.reshape(B, T, 3, D_MODEL))
    m = mg[:, :, 0] * out_a + mg[:, :, 1] * out_b + mg[:, :, 2] * out_c
    x = x + m @ prm['w_o']
    u = rmsnorm(x, prm['norm_ffn']) @ prm['w_up']
    cu, new_ffn = causal_dwconv(u, prev_ffn, prm['ffn_dw'])
    f = jax.nn.gelu(cu[..., :D_FF]) * cu[..., D_FF:]
    x = x + f @ prm['w_down']
    gate = jax.nn.sigmoid(rmsnorm(x, prm['norm_ple']) @ prm['w_ple_gate'])
    x = x + gate * (pe_i @ prm['w_ple'])
    return x, (new_conv, new_pool, new_ffn, new_kv, new_win)


def setup_inputs(seed: int = 0) -> dict:
    key = jax.random.key(seed)
    keys = jax.random.split(key, 33)

    def nrm(k, shape, s):
        return jax.random.normal(keys[k], shape, jnp.float32) * s

    n_pages = PAST_LEN // PAGE_SIZE
    n_pool = (DEC_BATCH * n_pages * 5) // 4
    win = min(WINDOW, PAST_LEN)
    perm = jax.random.permutation(keys[7], n_pool)
    page_table = perm[:DEC_BATCH * n_pages].reshape(DEC_BATCH, n_pages).astype(jnp.int32)
    return {
        'x_prompt': nrm(0, (BATCH, SEQ, D_MODEL), 1.0),
        'x_sample': nrm(1, (DEC_BATCH, DEC_SEQ, D_MODEL), 1.0),
        'cache_nsa_kv': nrm(2, (DEPTH, n_pool, PAGE_SIZE, 4, N_KV_HEADS, HEAD_DIM), 1.0),
        'state_nsa_win': nrm(3, (DEPTH, DEC_BATCH, win, 2, N_KV_HEADS, HEAD_DIM), 1.0),
        'state_conv': nrm(4, (DEPTH, DEC_BATCH, CONV_W - 1, CONV_WIDTH), 0.5),
        'state_pool': nrm(5, (DEPTH, DEC_BATCH, POOL_MAX - 1, POOL_WIDTH), 1.0),
        'state_ffn': nrm(6, (DEPTH, DEC_BATCH, FFN_CONV_W - 1, 2 * D_FF), 1.0),
        'page_table': page_table,
        'p_prompt': nrm(8, (DEPTH, BATCH, SEQ, PLE_DIM), 1.0),
        'p_sample': nrm(9, (DEPTH, DEC_BATCH, DEC_SEQ, PLE_DIM), 1.0),
        'norm_mix': 1.0 + nrm(10, (DEPTH, D_MODEL), 0.05),
        'w_in': nrm(11, (DEPTH, D_MODEL, IN_COLS), D_MODEL ** -0.5),
        'conv_dw': nrm(12, (DEPTH, CONV_W, CONV_WIDTH), CONV_W ** -0.5),
        'conv_b': nrm(13, (DEPTH, CONV_WIDTH), 0.02),
        'conv_ln_g': 1.0 + nrm(14, (DEPTH, CONV_WIDTH), 0.05),
        'conv_ln_b': nrm(15, (DEPTH, CONV_WIDTH), 0.02),
        'w_a_out': nrm(16, (DEPTH, CONV_WIDTH, D_MODEL), CONV_WIDTH ** -0.5),
        'pool_w': nrm(17, (DEPTH, len(POOL_WINDOWS), POOL_GC, POOL_GC), POOL_GC ** -0.5),
        'pool_scale': 1.0 + nrm(18, (DEPTH, POOL_WIDTH), 0.1),
        'w_b_out': nrm(19, (DEPTH, POOL_WIDTH, D_MODEL), POOL_WIDTH ** -0.5),
        'pe_cmp': nrm(20, (DEPTH, 2, CMP_BLOCK, HEAD_DIM), 0.1),
        'w_cmp1': nrm(21, (DEPTH, 2, CMP_BLOCK * HEAD_DIM, HEAD_DIM), (CMP_BLOCK * HEAD_DIM) ** -0.5),
        'w_cmp2': nrm(22, (DEPTH, 2, HEAD_DIM, HEAD_DIM), HEAD_DIM ** -0.5),
        'w_c_out': nrm(23, (DEPTH, N_HEADS * HEAD_DIM, D_MODEL), (N_HEADS * HEAD_DIM) ** -0.5),
        'w_o': nrm(24, (DEPTH, D_MODEL, D_MODEL), D_MODEL ** -0.5),
        'norm_ffn': 1.0 + nrm(25, (DEPTH, D_MODEL), 0.05),
        'w_up': nrm(26, (DEPTH, D_MODEL, 2 * D_FF), D_MODEL ** -0.5),
        'ffn_dw': nrm(27, (DEPTH, FFN_CONV_W, 2 * D_FF), FFN_CONV_W ** -0.5),
        'w_down': nrm(28, (DEPTH, D_FF, D_MODEL), D_FF ** -0.5),
        'norm_ple': 1.0 + nrm(29, (DEPTH, D_MODEL), 0.05),
        'w_ple_gate': nrm(30, (DEPTH, D_MODEL, D_MODEL), D_MODEL ** -0.5),
        'w_ple': nrm(31, (DEPTH, PLE_DIM, D_MODEL), PLE_DIM ** -0.5),
        'norm_final': 1.0 + nrm(32, (D_MODEL,), 0.05),
    }


def reference(x_prompt, x_sample, cache_nsa_kv, state_nsa_win, state_conv, state_pool, state_ffn, page_table,
              p_prompt, p_sample, norm_mix, w_in, conv_dw, conv_b, conv_ln_g, conv_ln_b, w_a_out, pool_w,
              pool_scale, w_b_out, pe_cmp, w_cmp1, w_cmp2, w_c_out, w_o, norm_ffn, w_up, ffn_dw, w_down,
              norm_ple, w_ple_gate, w_ple, norm_final):
    B = x_prompt.shape[0]
    past_len = page_table.shape[1] * cache_nsa_kv.shape[2]
    dt = x_prompt.dtype
    xp, xs = x_prompt, x_sample
    st_p, st_s = [], []
    for i in range(DEPTH):
        prm = dict(norm_mix=norm_mix[i], w_in=w_in[i], conv_dw=conv_dw[i], conv_b=conv_b[i],
                   conv_ln_g=conv_ln_g[i], conv_ln_b=conv_ln_b[i], w_a_out=w_a_out[i], pool_w=pool_w[i],
                   pool_scale=pool_scale[i], w_b_out=w_b_out[i], w_c_out=w_c_out[i], w_o=w_o[i],
                   norm_ffn=norm_ffn[i], w_up=w_up[i], ffn_dw=ffn_dw[i], w_down=w_down[i],
                   norm_ple=norm_ple[i], w_ple_gate=w_ple_gate[i], w_ple=w_ple[i])
        nsa_p = functools.partial(nsa_prompt, pe_c=pe_cmp[i], w1=w_cmp1[i], w2=w_cmp2[i])
        xp, st = trunk_layer(xp, p_prompt[i],
                             jnp.zeros((B, CONV_W - 1, CONV_WIDTH), dt),
                             jnp.zeros((B, POOL_MAX - 1, POOL_WIDTH), dt),
                             jnp.zeros((B, FFN_CONV_W - 1, 2 * D_FF), dt),
                             0, nsa_p, prm)
        st_p.append(st)
        nsa_s = functools.partial(nsa_sample, pe_c=pe_cmp[i], w1=w_cmp1[i], w2=w_cmp2[i],
                                  cache_kv=cache_nsa_kv[i], page_table=page_table, win_state=state_nsa_win[i])
        xs, st = trunk_layer(xs, p_sample[i], state_conv[i], state_pool[i], state_ffn[i], past_len, nsa_s, prm)
        st_s.append(st)
    y_prompt = rmsnorm(xp, norm_final)
    y_sample = rmsnorm(xs, norm_final)
    new_conv_prompt = jnp.stack([s[0] for s in st_p])
    new_pool_prompt = jnp.stack([s[1] for s in st_p])
    new_ffn_prompt = jnp.stack([s[2] for s in st_p])
    new_kv_prompt = jnp.stack([s[3] for s in st_p])
    new_win_prompt = jnp.stack([s[4] for s in st_p])
    new_conv_sample = jnp.stack([s[0] for s in st_s])
    new_pool_sample = jnp.stack([s[1] for s in st_s])
    new_ffn_sample = jnp.stack([s[2] for s in st_s])
    new_kv_sample = jnp.stack([s[3] for s in st_s])
    new_win_sample = jnp.stack([s[4] for s in st_s])
    return (y_prompt, y_sample, new_kv_prompt, new_win_prompt, new_conv_prompt, new_pool_prompt, new_ffn_prompt,
            new_kv_sample, new_win_sample, new_conv_sample, new_pool_sample, new_ffn_sample)
```

```python
import functools
import numpy as np
import jax
import jax.numpy as jnp
from jax import lax
from jax.experimental import pallas as pl
from jax.experimental.pallas import tpu as pltpu

D_MODEL = 1024
DEPTH = 4
CONV_WIDTH = D_MODEL // 2
POOL_WIDTH = D_MODEL // 2
HEAD_DIM = 64
N_HEADS = (D_MODEL // 2) // HEAD_DIM
N_KV_HEADS = 2
GROUP = N_HEADS // N_KV_HEADS
ROT_DIM = HEAD_DIM // 4
ROPE_THETA = 500000.0
CONV_W = 31
POOL_WINDOWS = (2, 4, 8, 16)
POOL_MAX = max(POOL_WINDOWS)
POOL_GC = POOL_WIDTH // len(POOL_WINDOWS)
CMP_BLOCK = 32
CMP_STRIDE = 16
SEL_BLOCK = 64
N_SELECT = 16
WINDOW = 512
Q_BLOCK = 128
D_FF = ((8 * D_MODEL // 3 + 127) // 128) * 128
FFN_CONV_W = 3
EPS = 1e-6
FORCE_SCORE = 1e4
NEG = -1e30

SPLIT_SIZES = (2 * CONV_WIDTH, POOL_WIDTH, N_HEADS * HEAD_DIM, 6 * N_KV_HEADS * HEAD_DIM, 3 * N_HEADS, 3 * D_MODEL)
SPLIT_AT = tuple(int(v) for v in np.cumsum(SPLIT_SIZES)[:-1])

LANES = 128
VMEM_LIMIT = 48 * 1024 * 1024


def _mm_kernel(x_ref, w_ref, o_ref):
    o_ref[...] = jnp.dot(x_ref[...].astype(jnp.bfloat16), w_ref[...],
                         preferred_element_type=jnp.float32)


def mm(x, w):
    lead = x.shape[:-1]
    k = x.shape[-1]
    m = w.shape[-1]
    x2 = x.reshape(-1, k)
    n = x2.shape[0]
    col_align = LANES if m <= 1024 else 512
    mp = -(-m // col_align) * col_align
    wb = w.astype(jnp.bfloat16)
    if mp != m:
        wb = jnp.pad(wb, ((0, 0), (0, mp - m)))
    tm = 512 if n % 512 == 0 else (256 if n % 256 == 0 else n)
    npad = n
    if n % 8 != 0:
        npad = -(-n // 8) * 8
        x2 = jnp.pad(x2, ((0, npad - n), (0, 0)))
        tm = npad
    tn = mp
    for cand in (1024, 768, 512, 384, 256, 128):
        if mp % cand == 0:
            tn = cand
            break
    out = pl.pallas_call(
        _mm_kernel,
        grid=(npad // tm, mp // tn),
        in_specs=[pl.BlockSpec((tm, k), lambda i, j: (i, 0)),
                  pl.BlockSpec((k, tn), lambda i, j: (0, j))],
        out_specs=pl.BlockSpec((tm, tn), lambda i, j: (i, j)),
        out_shape=jax.ShapeDtypeStruct((npad, mp), jnp.float32),
        compiler_params=pltpu.CompilerParams(
            dimension_semantics=("parallel", "parallel"), vmem_limit_bytes=VMEM_LIMIT),
    )(x2, wb)
    return out[:n, :m].reshape(lead + (m,))


def rmsnorm(x, g):
    xf = x.astype(jnp.float32)
    y = xf * lax.rsqrt(jnp.mean(xf * xf, axis=-1, keepdims=True) + EPS)
    return (y * g.astype(jnp.float32)).astype(x.dtype)


def layernorm(x, g, b):
    xf = x.astype(jnp.float32)
    mu = jnp.mean(xf, axis=-1, keepdims=True)
    var = jnp.mean(jnp.square(xf - mu), axis=-1, keepdims=True)
    y = (xf - mu) * lax.rsqrt(var + EPS) * g.astype(jnp.float32) + b.astype(jnp.float32)
    return y.astype(x.dtype)


def rotary(x, pos):
    half = ROT_DIM // 2
    inv = ROPE_THETA ** (-jnp.arange(half, dtype=jnp.float32) * 2.0 / ROT_DIM)
    ang = pos.astype(jnp.float32)[:, None] * inv[None, :]
    ang = ang.reshape((ang.shape[0],) + (1,) * (x.ndim - 3) + (half,))
    cos, sin = jnp.cos(ang), jnp.sin(ang)
    x1 = x[..., :half].astype(jnp.float32)
    x2 = x[..., half:ROT_DIM].astype(jnp.float32)
    rot = jnp.concatenate([x1 * cos - x2 * sin, x2 * cos + x1 * sin], axis=-1).astype(x.dtype)
    return jnp.concatenate([rot, x[..., ROT_DIM:]], axis=-1)


def masked_softmax(s, mask):
    s = jnp.where(mask, s.astype(jnp.float32), NEG)
    return jnp.where(mask, jax.nn.softmax(s, axis=-1), 0.0)


def causal_dwconv(u, prev, w):
    K, C = w.shape
    ext = jnp.concatenate([prev, u], axis=1)
    y = lax.conv_general_dilated(ext, w[:, None, :], window_strides=(1,), padding='VALID',
                                 dimension_numbers=('NWC', 'WIO', 'NWC'), feature_group_count=C)
    return y, ext[:, -(K - 1):]


def pool_mix(u, prev, pos0, w_grp, scale):
    B, T, C = u.shape
    P1 = POOL_MAX - 1
    ext = jnp.concatenate([prev, u], axis=1)
    cs = jnp.concatenate([jnp.zeros((B, 1, C), jnp.float32), jnp.cumsum(ext.astype(jnp.float32), axis=1)], axis=1)
    upto = cs[:, P1 + 1:P1 + 1 + T]
    pos = (pos0 + jnp.arange(T)).astype(jnp.float32)
    means = []
    for g, w in enumerate(POOL_WINDOWS):
        ch = slice(g * POOL_GC, (g + 1) * POOL_GC)
        start = cs[:, P1 + 1 - w:P1 + 1 - w + T, ch]
        cnt = jnp.minimum(pos + 1.0, float(w))[None, :, None]
        means.append((upto[..., ch] - start) / cnt)
    pooled = jnp.concatenate(means, axis=-1)
    d = (pooled - u.astype(jnp.float32)).astype(u.dtype).reshape(B, T, len(POOL_WINDOWS), POOL_GC)
    y = jnp.concatenate([mm(d[:, :, g], w_grp[g]) for g in range(len(POOL_WINDOWS))], axis=-1) * scale
    return y, ext[:, -P1:]


def compress(k, pe, w1, w2):
    B, L, G, D = k.shape
    nc = (L - CMP_BLOCK) // CMP_STRIDE + 1
    idx = jnp.arange(nc)[:, None] * CMP_STRIDE + jnp.arange(CMP_BLOCK)[None, :]
    blk = k[:, idx] + pe[None, None, :, None, :]
    flat = blk.transpose(0, 1, 3, 2, 4).reshape(B, nc, G, CMP_BLOCK * D)
    out = mm(jax.nn.gelu(mm(flat, w1)), w2)
    end = jnp.arange(nc) * CMP_STRIDE + CMP_BLOCK - 1
    return out, end


def to_blocks(k):
    B, L, G, D = k.shape
    ns = -(-L // SEL_BLOCK)
    k = jnp.pad(k, ((0, 0), (0, ns * SEL_BLOCK - L), (0, 0), (0, 0)))
    return k.reshape(B, ns, SEL_BLOCK, G, D).transpose(0, 3, 1, 2, 4)


def nsa_attend(q, ng, pos_q, kc, vc, cmp_end, ks, vs, kw, vw, pos_w):
    B, Tq, G, R, D = q.shape
    scale = D ** -0.5
    s = jnp.einsum('bqgrd,bngd->bgrqn', q, kc).astype(jnp.float32) * scale
    p_c = masked_softmax(s, cmp_end[None, :] <= pos_q[:, None])
    o_c = jnp.einsum('bgrqn,bngd->bqgrd', p_c.astype(vc.dtype), vc)
    nc, ns = kc.shape[1], ks.shape[2]
    owner = (jnp.arange(nc) * CMP_STRIDE) // SEL_BLOCK
    onehot = (owner[:, None] == jnp.arange(ns)[None, :]).astype(jnp.float32)
    imp = jnp.einsum('bgrqn,nj->bgqj', p_c, onehot)
    q_blk = pos_q // SEL_BLOCK
    j = jnp.arange(ns)[None, :]
    forced = (j == 0) | (j == q_blk[:, None]) | (j == q_blk[:, None] - 1)
    allowed = j <= q_blk[:, None]
    imp = jnp.where(allowed, jnp.where(forced, FORCE_SCORE, imp), -1.0)
    n_sel = min(N_SELECT, ns)
    vals, idx = lax.top_k(imp, n_sel)
    bi = jnp.arange(B)[:, None, None, None]
    gi = jnp.arange(G)[None, :, None, None]
    kg = ks[bi, gi, idx]
    vg = vs[bi, gi, idx]
    kpos = idx[..., None] * SEL_BLOCK + jnp.arange(SEL_BLOCK)
    m_s = (vals[..., None] >= 0.0) & (kpos <= pos_q[None, None, :, None, None])
    m_s = m_s.reshape(B, G, 1, Tq, n_sel * SEL_BLOCK)
    s = jnp.einsum('bqgrd,bgqkld->bgrqkl', q, kg).reshape(B, G, R, Tq, n_sel * SEL_BLOCK) * scale
    p_s = masked_softmax(s, m_s)
    o_s = jnp.einsum('bgrqm,bgqmd->bqgrd', p_s.astype(vg.dtype), vg.reshape(B, G, Tq, n_sel * SEL_BLOCK, D))
    m_w = (pos_w[None, :] <= pos_q[:, None]) & (pos_w[None, :] > pos_q[:, None] - WINDOW) & (pos_w[None, :] >= 0)
    s = jnp.einsum('bqgrd,blgd->bgrql', q, kw).astype(jnp.float32) * scale
    p_w = masked_softmax(s, m_w)
    o_w = jnp.einsum('bgrql,blgd->bqgrd', p_w.astype(vw.dtype), vw)
    return ng[..., 0, None] * o_c + ng[..., 1, None] * o_s + ng[..., 2, None] * o_w


def nsa_prompt(q, kv, ng, pe_c, w1, w2):
    B, T = q.shape[:2]
    kc, end = compress(kv[:, :, 0], pe_c[0], w1[0], w2[0])
    vc, _ = compress(kv[:, :, 1], pe_c[1], w1[1], w2[1])
    ks, vs = to_blocks(kv[:, :, 2]), to_blocks(kv[:, :, 3])
    kw = jnp.pad(kv[:, :, 4:6], ((0, 0), (WINDOW, 0), (0, 0), (0, 0), (0, 0)))

    def one_block(i):
        s0 = i * Q_BLOCK
        pos_q = s0 + jnp.arange(Q_BLOCK)
        qb = lax.dynamic_slice_in_dim(q, s0, Q_BLOCK, axis=1)
        gb = lax.dynamic_slice_in_dim(ng, s0, Q_BLOCK, axis=1)
        wb = lax.dynamic_slice_in_dim(kw, s0, WINDOW + Q_BLOCK, axis=1)
        pos_w = s0 - WINDOW + jnp.arange(WINDOW + Q_BLOCK)
        return nsa_attend(qb, gb, pos_q, kc, vc, end, ks, vs, wb[:, :, 0], wb[:, :, 1], pos_w)

    o = lax.map(one_block, jnp.arange(T // Q_BLOCK))
    o = jnp.moveaxis(o, 0, 1).reshape(B, T, N_HEADS * HEAD_DIM)
    wp = min(WINDOW, T)
    return o, kv[:, :, 0:4], kv[:, T - wp:, 4:6]


def nsa_sample(q, kv, ng, pe_c, w1, w2, cache_kv, page_table, win_state):
    B, T = q.shape[:2]
    past_len = page_table.shape[1] * cache_kv.shape[1]
    past = cache_kv[page_table].reshape(B, past_len, 4, N_KV_HEADS, HEAD_DIM)
    full = jnp.concatenate([past, kv[:, :, 0:4]], axis=1)
    kc, end = compress(full[:, :, 0], pe_c[0], w1[0], w2[0])
    vc, _ = compress(full[:, :, 1], pe_c[1], w1[1], w2[1])
    ks, vs = to_blocks(full[:, :, 2]), to_blocks(full[:, :, 3])
    wlen = win_state.shape[1]
    win = jnp.concatenate([win_state, kv[:, :, 4:6]], axis=1)
    pos_w = past_len - wlen + jnp.arange(wlen + T)
    pos_q = past_len + jnp.arange(T)
    o = nsa_attend(q, ng, pos_q, kc, vc, end, ks, vs, win[:, :, 0], win[:, :, 1], pos_w)
    return o.reshape(B, T, N_HEADS * HEAD_DIM), kv[:, :, 0:4], win[:, T:]


def trunk_layer(x, pe_i, prev_conv, prev_pool, prev_ffn, pos0, nsa_fn, prm):
    B, T, _ = x.shape
    pos = pos0 + jnp.arange(T)
    h = rmsnorm(x, prm['norm_mix'])
    z = mm(h, prm['w_in'])
    z_a, z_b, z_q, z_kv, z_ng, z_mg = jnp.split(z, SPLIT_AT, axis=-1)
    a = z_a[..., :CONV_WIDTH] * jax.nn.sigmoid(z_a[..., CONV_WIDTH:])
    c, new_conv = causal_dwconv(a, prev_conv, prm['conv_dw'])
    c = jax.nn.silu(layernorm(c + prm['conv_b'], prm['conv_ln_g'], prm['conv_ln_b']))
    out_a = mm(c, prm['w_a_out'])
    pb, new_pool = pool_mix(z_b, prev_pool, pos0, prm['pool_w'], prm['pool_scale'])
    out_b = mm(pb, prm['w_b_out'])
    q = rotary(z_q.reshape(B, T, N_HEADS, HEAD_DIM), pos).reshape(B, T, N_KV_HEADS, GROUP, HEAD_DIM)
    kv = z_kv.reshape(B, T, 6, N_KV_HEADS, HEAD_DIM)
    keys = rotary(kv[:, :, 0::2], pos)
    kv = jnp.stack([keys, kv[:, :, 1::2]], axis=3).reshape(B, T, 6, N_KV_HEADS, HEAD_DIM)
    ng = jax.nn.sigmoid(z_ng.reshape(B, T, N_KV_HEADS, GROUP, 3))
    o_c, new_kv, new_win = nsa_fn(q, kv, ng)
    out_c = mm(o_c, prm['w_c_out'])
    mg = jax.nn.sigmoid(z_mg.reshape(B, T, 3, D_MODEL))
    m = mg[:, :, 0] * out_a + mg[:, :, 1] * out_b + mg[:, :, 2] * out_c
    x = x + mm(m, prm['w_o'])
    u = mm(rmsnorm(x, prm['norm_ffn']), prm['w_up'])
    cu, new_ffn = causal_dwconv(u, prev_ffn, prm['ffn_dw'])
    f = jax.nn.gelu(cu[..., :D_FF]) * cu[..., D_FF:]
    x = x + mm(f, prm['w_down'])
    gate = jax.nn.sigmoid(mm(rmsnorm(x, prm['norm_ple']), prm['w_ple_gate']))
    x = x + gate * mm(pe_i, prm['w_ple'])
    return x, (new_conv, new_pool, new_ffn, new_kv, new_win)


def kernel(x_prompt, x_sample, cache_nsa_kv, state_nsa_win, state_conv, state_pool, state_ffn, page_table,
           p_prompt, p_sample, norm_mix, w_in, conv_dw, conv_b, conv_ln_g, conv_ln_b, w_a_out, pool_w,
           pool_scale, w_b_out, pe_cmp, w_cmp1, w_cmp2, w_c_out, w_o, norm_ffn, w_up, ffn_dw, w_down,
           norm_ple, w_ple_gate, w_ple, norm_final):
    B = x_prompt.shape[0]
    past_len = page_table.shape[1] * cache_nsa_kv.shape[2]
    dt = x_prompt.dtype
    xp, xs = x_prompt, x_sample
    st_p, st_s = [], []
    for i in range(DEPTH):
        prm = dict(norm_mix=norm_mix[i], w_in=w_in[i], conv_dw=conv_dw[i], conv_b=conv_b[i],
                   conv_ln_g=conv_ln_g[i], conv_ln_b=conv_ln_b[i], w_a_out=w_a_out[i], pool_w=pool_w[i],
                   pool_scale=pool_scale[i], w_b_out=w_b_out[i], w_c_out=w_c_out[i], w_o=w_o[i],
                   norm_ffn=norm_ffn[i], w_up=w_up[i], ffn_dw=ffn_dw[i], w_down=w_down[i],
                   norm_ple=norm_ple[i], w_ple_gate=w_ple_gate[i], w_ple=w_ple[i])
        nsa_p = functools.partial(nsa_prompt, pe_c=pe_cmp[i], w1=w_cmp1[i], w2=w_cmp2[i])
        xp, st = trunk_layer(xp, p_prompt[i],
                             jnp.zeros((B, CONV_W - 1, CONV_WIDTH), dt),
                             jnp.zeros((B, POOL_MAX - 1, POOL_WIDTH), dt),
                             jnp.zeros((B, FFN_CONV_W - 1, 2 * D_FF), dt),
                             0, nsa_p, prm)
        st_p.append(st)
        nsa_s = functools.partial(nsa_sample, pe_c=pe_cmp[i], w1=w_cmp1[i], w2=w_cmp2[i],
                                  cache_kv=cache_nsa_kv[i], page_table=page_table, win_state=state_nsa_win[i])
        xs, st = trunk_layer(xs, p_sample[i], state_conv[i], state_pool[i], state_ffn[i], past_len, nsa_s, prm)
        st_s.append(st)
    y_prompt = rmsnorm(xp, norm_final)
    y_sample = rmsnorm(xs, norm_final)
    stk = lambda lst, k: jnp.stack([s[k] for s in lst])
    return (y_prompt, y_sample, stk(st_p, 3), stk(st_p, 4), stk(st_p, 0), stk(st_p, 1), stk(st_p, 2),
            stk(st_s, 3), stk(st_s, 4), stk(st_s, 0), stk(st_s, 1), stk(st_s, 2))
```

```python
import functools
import numpy as np
import jax
import jax.numpy as jnp
from jax import lax
from jax.experimental import pallas as pl
from jax.experimental.pallas import tpu as pltpu

D_MODEL = 1024
DEPTH = 4
CONV_WIDTH = D_MODEL // 2
POOL_WIDTH = D_MODEL // 2
HEAD_DIM = 64
N_HEADS = (D_MODEL // 2) // HEAD_DIM
N_KV_HEADS = 2
GROUP = N_HEADS // N_KV_HEADS
ROT_DIM = HEAD_DIM // 4
ROPE_THETA = 500000.0
CONV_W = 31
POOL_WINDOWS = (2, 4, 8, 16)
POOL_MAX = max(POOL_WINDOWS)
POOL_GC = POOL_WIDTH // len(POOL_WINDOWS)
CMP_BLOCK = 32
CMP_STRIDE = 16
SEL_BLOCK = 64
N_SELECT = 16
WINDOW = 512
Q_BLOCK = 128
D_FF = ((8 * D_MODEL // 3 + 127) // 128) * 128
FFN_CONV_W = 3
PLE_DIM = 256
EPS = 1e-6
FORCE_SCORE = 1e4
NEG = -1e30

SPLIT_SIZES = (2 * CONV_WIDTH, POOL_WIDTH, N_HEADS * HEAD_DIM, 6 * N_KV_HEADS * HEAD_DIM, 3 * N_HEADS, 3 * D_MODEL)
SPLIT_AT = tuple(int(v) for v in np.cumsum(SPLIT_SIZES)[:-1])

F32 = jnp.float32
BF16 = jnp.bfloat16
LANES = 128
VMEM_LIMIT = 56 * 1024 * 1024
NT = (((1,), (1,)), ((), ()))

NG_PAD = LANES
C_A, C_B, C_Q, C_KV = 0, 2 * CONV_WIDTH, 2 * CONV_WIDTH + POOL_WIDTH, 2 * CONV_WIDTH + POOL_WIDTH + 512
C_NG = C_KV + 768
C_MG = C_NG + NG_PAD
IN_PACKED = C_MG + 3 * D_MODEL
KV_W = N_KV_HEADS * HEAD_DIM


def _cparams(*sem):
    return pltpu.CompilerParams(dimension_semantics=sem, vmem_limit_bytes=VMEM_LIMIT)


def _const_spec(shape):
    nd = len(shape)
    return pl.BlockSpec(shape, lambda *_: (0,) * nd, pipeline_mode=pl.Buffered(1))


def _rms(x, g):
    return x * lax.rsqrt(jnp.mean(x * x, axis=-1, keepdims=True) + EPS) * g


def _gelu(x):
    return 0.5 * x * (1.0 + jnp.tanh(np.sqrt(2.0 / np.pi).astype(np.float32) * (x + 0.044715 * (x * x * x))))


def _mm_kernel(x_ref, w_ref, o_ref):
    o_ref[...] = jnp.dot(x_ref[...].astype(BF16), w_ref[...], preferred_element_type=F32)


def mm(x, w):
    lead = x.shape[:-1]
    k = x.shape[-1]
    m = w.shape[-1]
    x2 = x.reshape(-1, k)
    n = x2.shape[0]
    col_align = LANES if m <= 1024 else 512
    mp = -(-m // col_align) * col_align
    wb = w.astype(BF16)
    if mp != m:
        wb = jnp.pad(wb, ((0, 0), (0, mp - m)))
    tm = 512 if n % 512 == 0 else (256 if n % 256 == 0 else n)
    npad = n
    if n % 8 != 0:
        npad = -(-n // 8) * 8
        x2 = jnp.pad(x2, ((0, npad - n), (0, 0)))
        tm = npad
    tn = mp
    for cand in (1024, 768, 512, 384, 256, 128):
        if mp % cand == 0:
            tn = cand
            break
    out = pl.pallas_call(
        _mm_kernel,
        grid=(npad // tm, mp // tn),
        in_specs=[pl.BlockSpec((tm, k), lambda i, j: (i, 0)),
                  pl.BlockSpec((k, tn), lambda i, j: (0, j))],
        out_specs=pl.BlockSpec((tm, tn), lambda i, j: (i, j)),
        out_shape=jax.ShapeDtypeStruct((npad, mp), F32),
        compiler_params=_cparams("parallel", "parallel"),
    )(x2, wb)
    return out[:n, :m].reshape(lead + (m,))


def _rope_tables(pos):
    half = ROT_DIM // 2
    inv = ROPE_THETA ** (-jnp.arange(half, dtype=F32) * 2.0 / ROT_DIM)
    ang = pos.astype(F32)[:, None] * inv[None, :]
    cos, sin = jnp.cos(ang), jnp.sin(ang)
    rows = pos.shape[0]
    ones = jnp.ones((rows, HEAD_DIM - ROT_DIM), F32)
    zeros = jnp.zeros((rows, HEAD_DIM - ROT_DIM), F32)
    z8 = jnp.zeros((rows, half), F32)
    cos_h = jnp.concatenate([cos, cos, ones], axis=1)
    sup_h = jnp.concatenate([-sin, z8, zeros], axis=1)
    sdn_h = jnp.concatenate([z8, sin, zeros], axis=1)
    rep = LANES // HEAD_DIM
    return jnp.tile(cos_h, (1, rep)), jnp.tile(sup_h, (1, rep)), jnp.tile(sdn_h, (1, rep))


def _rope128(x, cos, sup, sdn):
    return x * cos + pltpu.roll(x, LANES - ROT_DIM // 2, 1) * sup + pltpu.roll(x, ROT_DIM // 2, 1) * sdn


def _inproj_kernel(x_ref, g_ref, w_ref, cos_ref, sup_ref, sdn_ref,
                   a_ref, zb_ref, qh_ref, ql_ref, kv4_ref, win_ref, cmp_ref, slc_ref, winb_ref, ng_ref, mg_ref):
    h = _rms(x_ref[...], g_ref[...]).astype(BF16)

    def proj(c0, n):
        return jnp.dot(h, w_ref[:, c0:c0 + n], preferred_element_type=F32)

    za = proj(C_A, 2 * CONV_WIDTH)
    a_ref[...] = za[:, :CONV_WIDTH] * jax.nn.sigmoid(za[:, CONV_WIDTH:])
    zb_ref[...] = proj(C_B, POOL_WIDTH)
    cos, sup, sdn = cos_ref[...], sup_ref[...], sdn_ref[...]
    zq = proj(C_Q, N_HEADS * HEAD_DIM)
    scale = HEAD_DIM ** -0.5
    for c in range(N_HEADS * HEAD_DIM // LANES):
        sl = slice(c * LANES, (c + 1) * LANES)
        qr = _rope128(zq[:, sl], cos, sup, sdn) * scale
        qh = qr.astype(BF16)
        qh_ref[:, sl] = qh
        ql_ref[:, sl] = (qr - qh.astype(F32)).astype(BF16)
    zkv = proj(C_KV, 6 * KV_W)
    kind = [zkv[:, j * KV_W:(j + 1) * KV_W] for j in range(6)]
    for j in (0, 2, 4):
        kind[j] = _rope128(kind[j], cos, sup, sdn)
    for j in range(4):
        kv4_ref[:, j * KV_W:(j + 1) * KV_W] = kind[j]
    cmp_ref[:, 0:KV_W] = kind[0]
    cmp_ref[:, KV_W:2 * KV_W] = kind[1]
    slc_ref[:, 0:KV_W] = kind[2].astype(BF16)
    slc_ref[:, KV_W:2 * KV_W] = kind[3].astype(BF16)
    win_ref[:, 0:KV_W] = kind[4]
    win_ref[:, KV_W:2 * KV_W] = kind[5]
    winb_ref[:, 0:KV_W] = kind[4].astype(BF16)
    winb_ref[:, KV_W:2 * KV_W] = kind[5].astype(BF16)
    ng_ref[...] = jax.nn.sigmoid(proj(C_NG, NG_PAD))
    mg_ref[...] = jax.nn.sigmoid(proj(C_MG, 3 * D_MODEL))


def _pack_w_in(w_in):
    parts = jnp.split(w_in, SPLIT_AT, axis=-1)
    ng = jnp.pad(parts[4], ((0, 0), (0, NG_PAD - parts[4].shape[1])))
    return jnp.concatenate([parts[0], parts[1], parts[2], parts[3], ng, parts[5]], axis=1).astype(BF16)


def inproj(x2, g, w_packed, tables, period, tm):
    n = x2.shape[0]
    pblocks = period // tm
    row = lambda c: pl.BlockSpec((tm, c), lambda i: (i, 0))
    tab = pl.BlockSpec((tm, LANES), lambda i: (i % pblocks, 0))
    outs = [(CONV_WIDTH, F32), (POOL_WIDTH, F32), (512, BF16), (512, BF16), (4 * KV_W, F32), (2 * KV_W, F32),
            (2 * KV_W, F32), (2 * KV_W, BF16), (2 * KV_W, BF16), (NG_PAD, F32), (3 * D_MODEL, F32)]
    return pl.pallas_call(
        _inproj_kernel,
        grid=(n // tm,),
        in_specs=[row(D_MODEL), _const_spec((1, D_MODEL)), _const_spec((D_MODEL, IN_PACKED)), tab, tab, tab],
        out_specs=[row(c) for c, _ in outs],
        out_shape=[jax.ShapeDtypeStruct((n, c), dt) for c, dt in outs],
        compiler_params=_cparams("parallel"),
        name="inproj",
    )(x2, g.reshape(1, D_MODEL), w_packed, *tables)


CONV_HALO = 32


def _conva_kernel(a_ref, w_ref, b_ref, g_ref, beta_ref, c_ref, ext_ref, *, tm, rc):
    @pl.when(pl.program_id(1) == 0)
    def _():
        ext_ref[0:CONV_HALO, :] = jnp.zeros((CONV_HALO, CONV_WIDTH), F32)

    ext_ref[CONV_HALO:CONV_HALO + tm, :] = a_ref[0]
    off = CONV_HALO - (CONV_W - 1)
    for r0 in range(0, tm, rc):
        acc = jnp.zeros((rc, CONV_WIDTH), F32)
        for k in range(CONV_W):
            acc = acc + ext_ref[r0 + off + k:r0 + off + k + rc, :] * w_ref[k:k + 1, :]
        y = acc + b_ref[...]
        mu = jnp.mean(y, axis=-1, keepdims=True)
        d = y - mu
        var = jnp.mean(d * d, axis=-1, keepdims=True)
        yn = d * lax.rsqrt(var + EPS) * g_ref[...] + beta_ref[...]
        c_ref[0, r0:r0 + rc, :] = (yn * jax.nn.sigmoid(yn)).astype(BF16)
    ext_ref[0:CONV_HALO, :] = ext_ref[tm:tm + CONV_HALO, :]


def conv_a(a3, conv_dw, conv_b, ln_g, ln_b, tm=256, rc=32):
    b, t, c = a3.shape
    vec = lambda v: v.reshape(1, c)
    return pl.pallas_call(
        functools.partial(_conva_kernel, tm=tm, rc=rc),
        grid=(b, t // tm),
        in_specs=[pl.BlockSpec((1, tm, c), lambda i, j: (i, j, 0)), _const_spec((CONV_W, c)),
                  _const_spec((1, c)), _const_spec((1, c)), _const_spec((1, c))],
        out_specs=pl.BlockSpec((1, tm, c), lambda i, j: (i, j, 0)),
        out_shape=jax.ShapeDtypeStruct((b, t, c), BF16),
        scratch_shapes=[pltpu.VMEM((tm + CONV_HALO, c), F32)],
        compiler_params=_cparams("arbitrary", "arbitrary"),
        name="conv_a",
    )(a3, conv_dw, vec(conv_b), vec(ln_g), vec(ln_b))


POOL_HALO = 16


def _pool_kernel(z_ref, pw_ref, sc_ref, pb_ref, ext_ref, *, tm, rc):
    i = pl.program_id(1)

    @pl.when(i == 0)
    def _():
        ext_ref[0:POOL_HALO, :] = jnp.zeros((POOL_HALO, POOL_WIDTH), F32)

    ext_ref[POOL_HALO:POOL_HALO + tm, :] = z_ref[0]
    for r0 in range(0, tm, rc):
        pos = (i * tm + r0 + lax.broadcasted_iota(jnp.int32, (rc, 1), 0)).astype(F32)
        for gi, w in enumerate(POOL_WINDOWS):
            lanes = slice(gi * POOL_GC, (gi + 1) * POOL_GC)
            base = POOL_HALO + r0
            cur = ext_ref[base:base + rc, lanes]
            s = cur
            for k in range(1, w):
                s = s + ext_ref[base - k:base - k + rc, lanes]
            cnt = jnp.minimum(pos + 1.0, float(w))
            d = s / cnt - cur
            y = jnp.dot(d.astype(BF16), pw_ref[gi], preferred_element_type=F32) * sc_ref[:, lanes]
            pb_ref[0, r0:r0 + rc, lanes] = y.astype(BF16)
    ext_ref[0:POOL_HALO, :] = ext_ref[tm:tm + POOL_HALO, :]


def pool_b(z3, pool_w, pool_scale, tm=512, rc=128):
    b, t, c = z3.shape
    return pl.pallas_call(
        functools.partial(_pool_kernel, tm=tm, rc=rc),
        grid=(b, t // tm),
        in_specs=[pl.BlockSpec((1, tm, c), lambda i, j: (i, j, 0)),
                  _const_spec((len(POOL_WINDOWS), POOL_GC, POOL_GC)), _const_spec((1, c))],
        out_specs=pl.BlockSpec((1, tm, c), lambda i, j: (i, j, 0)),
        out_shape=jax.ShapeDtypeStruct((b, t, c), BF16),
        scratch_shapes=[pltpu.VMEM((tm + POOL_HALO, c), F32)],
        compiler_params=_cparams("arbitrary", "arbitrary"),
        name="pool_b",
    )(z3, pool_w.astype(BF16), pool_scale.reshape(1, c))


SEG = CMP_STRIDE
CMP_LANES = 2 * KV_W


def _compress_weights(pe_c, w1, w2):
    eye = jnp.eye(2, dtype=F32)
    w1r = w1.reshape(2, CMP_BLOCK, HEAD_DIM, HEAD_DIM)
    def expand(w):
        return jnp.einsum('kjde,kl,gh->jkgdlhe', w, eye, eye).reshape(SEG * CMP_LANES, CMP_LANES).astype(BF16)
    wa, wb = expand(w1r[:, :SEG]), expand(w1r[:, SEG:])
    w2bd = jnp.einsum('kde,kl,gh->kgdlhe', w2, eye, eye).reshape(CMP_LANES, CMP_LANES).astype(BF16)
    def pe_row(p):
        return jnp.broadcast_to(p.transpose(1, 0, 2)[:, :, None, :], (SEG, 2, N_KV_HEADS, HEAD_DIM)).reshape(1, SEG * CMP_LANES)
    return wa, wb, w2bd, pe_row(pe_c[:, :SEG]), pe_row(pe_c[:, SEG:])


def _compress_kernel(x_ref, pea_ref, peb_ref, wa_ref, wb_ref, w2_ref, o_ref, bsc_ref, *, nseg):
    x = x_ref[0]
    a = jnp.dot((x + pea_ref[...]).astype(BF16), wa_ref[...], preferred_element_type=F32)
    bsc_ref[0:nseg, :] = jnp.dot((x + peb_ref[...]).astype(BF16), wb_ref[...], preferred_element_type=F32)
    bsc_ref[nseg:nseg + 8, :] = jnp.zeros((8, CMP_LANES), F32)
    u = a + bsc_ref[1:nseg + 1, :]
    o_ref[0] = jnp.dot(_gelu(u).astype(BF16), w2_ref[...], preferred_element_type=F32)


def compress_prompt(cmp3, cw):
    b, t, _ = cmp3.shape
    nseg = t // SEG
    wa, wb, w2bd, pea, peb = cw
    xs = cmp3.reshape(b, nseg, SEG * CMP_LANES)
    return pl.pallas_call(
        functools.partial(_compress_kernel, nseg=nseg),
        grid=(b,),
        in_specs=[pl.BlockSpec((1, nseg, SEG * CMP_LANES), lambda i: (i, 0, 0)),
                  _const_spec((1, SEG * CMP_LANES)), _const_spec((1, SEG * CMP_LANES)),
                  _const_spec((SEG * CMP_LANES, CMP_LANES)), _const_spec((SEG * CMP_LANES, CMP_LANES)),
                  _const_spec((CMP_LANES, CMP_LANES))],
        out_specs=pl.BlockSpec((1, nseg, CMP_LANES), lambda i: (i, 0, 0)),
        out_shape=jax.ShapeDtypeStruct((b, nseg, CMP_LANES), F32),
        scratch_shapes=[pltpu.VMEM((nseg + 8, CMP_LANES), F32)],
        compiler_params=_cparams("arbitrary"),
        name="compress",
    )(xs, pea, peb, wa, wb, w2bd)


QA = LANES + HEAD_DIM
Q3 = 3 * HEAD_DIM


def _softmax_rows(s, mask):
    s = jnp.where(mask, s, NEG)
    m = jnp.max(s, axis=-1, keepdims=True)
    e = jnp.where(mask, jnp.exp(s - m), 0.0)
    l = jnp.sum(e, axis=-1, keepdims=True)
    return e * (1.0 / jnp.where(l > 0.0, l, 1.0))


def _nsa_kernel(qh_ref, ql_ref, slc_ref, w0_ref, w1_ref, w2_ref, kcvc_ref, oh_ref, ng_ref, o_ref,
                kaug_ref, vsel_ref, kc3_ref, vc_ref, q3_ref, qaug_ref, m_ref, l_ref, acc_ref,
                *, t_len, tq, kc_len, ncmp):
    i = pl.program_id(1)
    nsel = t_len // SEL_BLOCK
    rows = GROUP * tq

    @pl.when(i == 0)
    def _():
        blk = lax.broadcasted_iota(jnp.int32, (t_len, LANES), 0) // SEL_BLOCK
        lane = lax.broadcasted_iota(jnp.int32, (t_len, LANES), 1)
        onehot = jnp.where(blk == lane, 1.0, 0.0).astype(BF16)
        kcvc = kcvc_ref[0]
        for g in range(N_KV_HEADS):
            kaug_ref[g, :, 0:LANES] = onehot
            kaug_ref[g, :, LANES:QA] = slc_ref[0, :, g * HEAD_DIM:(g + 1) * HEAD_DIM]
            vsel_ref[g] = slc_ref[0, :, KV_W + g * HEAD_DIM:KV_W + (g + 1) * HEAD_DIM]
            kc = kcvc[:, g * HEAD_DIM:(g + 1) * HEAD_DIM]
            kch = kc.astype(BF16)
            kcl = (kc - kch.astype(F32)).astype(BF16)
            kc3_ref[g, :, 0:HEAD_DIM] = kch
            kc3_ref[g, :, HEAD_DIM:2 * HEAD_DIM] = kch
            kc3_ref[g, :, 2 * HEAD_DIM:Q3] = kcl
            vc_ref[g] = kcvc[:, KV_W + g * HEAD_DIM:KV_W + (g + 1) * HEAD_DIM].astype(BF16)

    s0 = i * tq
    pos4 = s0 + lax.broadcasted_iota(jnp.int32, (rows, 1), 0) % tq
    ng = ng_ref[0]
    c_last = s0 // kc_len

    for g in range(N_KV_HEADS):
        for r in range(GROUP):
            hs = slice((g * GROUP + r) * HEAD_DIM, (g * GROUP + r + 1) * HEAD_DIM)
            rs = slice(r * tq, (r + 1) * tq)
            qh = qh_ref[0, :, hs]
            q3_ref[rs, 0:HEAD_DIM] = qh
            q3_ref[rs, HEAD_DIM:2 * HEAD_DIM] = ql_ref[0, :, hs]
            q3_ref[rs, 2 * HEAD_DIM:Q3] = qh
            qaug_ref[rs, LANES:QA] = qh

        s = lax.dot_general(q3_ref[...], kc3_ref[g], NT, preferred_element_type=F32)
        n_idx = lax.broadcasted_iota(jnp.int32, (1, s.shape[1]), 1)
        end = n_idx * CMP_STRIDE + (CMP_BLOCK - 1)
        mask = jnp.where(n_idx < ncmp, end, t_len + CMP_BLOCK) <= pos4
        p = _softmax_rows(s, mask)
        o_c = jnp.dot(p.astype(BF16), vc_ref[g], preferred_element_type=F32)
        psum = p[0:tq]
        for r in range(1, GROUP):
            psum = psum + p[r * tq:(r + 1) * tq]
        p_hi = psum.astype(BF16)
        p_lo = (psum - p_hi.astype(F32)).astype(BF16)
        imp = (jnp.dot(p_hi, oh_ref[...], preferred_element_type=F32)
               + jnp.dot(p_lo, oh_ref[...], preferred_element_type=F32))

        v = imp.T
        jidx = lax.broadcasted_iota(jnp.int32, (LANES, tq), 0)
        qblk = (s0 + lax.broadcasted_iota(jnp.int32, (1, tq), 1)) // SEL_BLOCK
        forced = (jidx == 0) | (jidx == qblk) | (jidx == qblk - 1)
        v = jnp.where(jidx <= qblk, jnp.where(forced, FORCE_SCORE, v), -1.0)
        v = jnp.where(jidx < nsel, v, -3.0)
        sel = jnp.zeros((LANES, tq), F32)
        for _ in range(min(N_SELECT, nsel)):
            mx = jnp.max(v, axis=0, keepdims=True)
            first = jnp.min(jnp.where(v == mx, jidx, LANES), axis=0, keepdims=True)
            pick = jidx == first
            sel = jnp.where(pick, 1.0, sel)
            v = jnp.where(pick, -2.0, v)
        selneg = jnp.where(sel.T > 0.5, 0.0, NEG).astype(BF16)
        for r in range(GROUP):
            qaug_ref[r * tq:(r + 1) * tq, 0:LANES] = selneg

        m_ref[...] = jnp.full((rows, 1), NEG, F32)
        l_ref[...] = jnp.zeros((rows, 1), F32)
        acc_ref[...] = jnp.zeros((rows, HEAD_DIM), F32)

        def chunk(c, causal):
            k0 = pl.multiple_of(c * kc_len, kc_len)
            kk = kaug_ref[g, pl.ds(k0, kc_len), :]
            vv = vsel_ref[g, pl.ds(k0, kc_len), :]
            sc = lax.dot_general(qaug_ref[...], kk, NT, preferred_element_type=F32)
            if causal:
                kpos = k0 + lax.broadcasted_iota(jnp.int32, (1, kc_len), 1)
                sc = jnp.where(kpos <= pos4, sc, NEG)
            m_prev = m_ref[...]
            m_new = jnp.maximum(m_prev, jnp.max(sc, axis=-1, keepdims=True))
            alpha = jnp.exp(m_prev - m_new)
            pe = jnp.exp(sc - m_new)
            l_ref[...] = alpha * l_ref[...] + jnp.sum(pe, axis=-1, keepdims=True)
            acc_ref[...] = alpha * acc_ref[...] + jnp.dot(pe.astype(BF16), vv, preferred_element_type=F32)
            m_ref[...] = m_new

        def body(c, carry):
            chunk(c, False)
            return carry

        lax.fori_loop(0, c_last, body, 0)
        chunk(c_last, True)
        o_s = acc_ref[...] * (1.0 / l_ref[...])

        kw = jnp.concatenate([w_ref[0, :, g * HEAD_DIM:(g + 1) * HEAD_DIM] for w_ref in (w0_ref, w1_ref, w2_ref)], axis=0)
        vw = jnp.concatenate([w_ref[0, :, KV_W + g * HEAD_DIM:KV_W + (g + 1) * HEAD_DIM]
                              for w_ref in (w0_ref, w1_ref, w2_ref)], axis=0)
        sw = lax.dot_general(q3_ref[:, 0:HEAD_DIM], kw, NT, preferred_element_type=F32)
        pos_w = s0 - 2 * tq + lax.broadcasted_iota(jnp.int32, (1, 3 * tq), 1)
        mw = (pos_w <= pos4) & (pos_w > pos4 - WINDOW) & (pos_w >= 0)
        pw = _softmax_rows(sw, mw)
        o_w = jnp.dot(pw.astype(BF16), vw, preferred_element_type=F32)

        for r in range(GROUP):
            rs = slice(r * tq, (r + 1) * tq)
            c0 = (g * GROUP + r) * 3
            o = (ng[:, c0:c0 + 1] * o_c[rs] + ng[:, c0 + 1:c0 + 2] * o_s[rs] + ng[:, c0 + 2:c0 + 3] * o_w[rs])
            o_ref[0, :, (g * GROUP + r) * HEAD_DIM:(g * GROUP + r + 1) * HEAD_DIM] = o.astype(BF16)


def nsa_prompt(qh3, ql3, slc3, winb3, kcvc, ng3, tq=256, kc_len=512):
    b, t, _ = qh3.shape
    nseg = t // SEG
    ncmp = (t - CMP_BLOCK) // CMP_STRIDE + 1
    owner = (jnp.arange(nseg) * CMP_STRIDE) // SEL_BLOCK
    oh = (owner[:, None] == jnp.arange(LANES)[None, :]).astype(BF16)
    rows = GROUP * tq
    tile = lambda c: pl.BlockSpec((1, tq, c), lambda i, j: (i, j, 0))
    wspec = lambda back: pl.BlockSpec((1, tq, 2 * KV_W), lambda i, j: (i, jnp.maximum(j - back, 0), 0))
    return pl.pallas_call(
        functools.partial(_nsa_kernel, t_len=t, tq=tq, kc_len=kc_len, ncmp=ncmp),
        grid=(b, t // tq),
        in_specs=[tile(512), tile(512),
                  pl.BlockSpec((1, t, 2 * KV_W), lambda i, j: (i, 0, 0)),
                  wspec(2), wspec(1), wspec(0),
                  pl.BlockSpec((1, nseg, CMP_LANES), lambda i, j: (i, 0, 0)),
                  _const_spec((nseg, LANES)), tile(NG_PAD)],
        out_specs=tile(512),
        out_shape=jax.ShapeDtypeStruct((b, t, 512), BF16),
        scratch_shapes=[pltpu.VMEM((N_KV_HEADS, t, QA), BF16), pltpu.VMEM((N_KV_HEADS, t, HEAD_DIM), BF16),
                        pltpu.VMEM((N_KV_HEADS, nseg, Q3), BF16), pltpu.VMEM((N_KV_HEADS, nseg, HEAD_DIM), BF16),
                        pltpu.VMEM((rows, Q3), BF16), pltpu.VMEM((rows, QA), BF16),
                        pltpu.VMEM((rows, 1), F32), pltpu.VMEM((rows, 1), F32), pltpu.VMEM((rows, HEAD_DIM), F32)],
        compiler_params=_cparams("arbitrary", "arbitrary"),
        name="nsa_prompt",
    )(qh3, ql3, slc3, winb3, winb3, winb3, kcvc, oh, ng3)


def _merge_kernel(c_ref, pb_ref, oc_ref, mg_ref, x_ref, wa_ref, wb_ref, wc_ref, wo_ref, o_ref):
    out_a = jnp.dot(c_ref[...], wa_ref[...], preferred_element_type=F32)
    out_b = jnp.dot(pb_ref[...], wb_ref[...], preferred_element_type=F32)
    out_c = jnp.dot(oc_ref[...], wc_ref[...], preferred_element_type=F32)
    m = (mg_ref[:, 0:D_MODEL] * out_a + mg_ref[:, D_MODEL:2 * D_MODEL] * out_b
         + mg_ref[:, 2 * D_MODEL:3 * D_MODEL] * out_c)
    o_ref[...] = x_ref[...] + jnp.dot(m.astype(BF16), wo_ref[...], preferred_element_type=F32)


def merge(c2, pb2, oc2, mg2, x2, wa, wb, wc, wo, tm=512):
    n = x2.shape[0]
    row = lambda c: pl.BlockSpec((tm, c), lambda i: (i, 0))
    return pl.pallas_call(
        _merge_kernel,
        grid=(n // tm,),
        in_specs=[row(512), row(512), row(512), row(3 * D_MODEL), row(D_MODEL),
                  _const_spec((512, D_MODEL)), _const_spec((512, D_MODEL)), _const_spec((512, D_MODEL)),
                  _const_spec((D_MODEL, D_MODEL))],
        out_specs=row(D_MODEL),
        out_shape=jax.ShapeDtypeStruct((n, D_MODEL), F32),
        compiler_params=_cparams("parallel"),
        name="merge",
    )(c2, pb2, oc2, mg2, x2, wa, wb, wc, wo)


FF_CHUNK = 256
FF_HALO = 8


def _ffn_kernel(x_ref, pe_ref, gf_ref, wup_ref, dw_ref, wdn_ref, gp_ref, wg_ref, wp_ref, o_ref, nf_ref,
                usc_ref, carry_ref, *, tm):
    @pl.when(pl.program_id(1) == 0)
    def _():
        carry_ref[...] = jnp.zeros((FF_HALO, 2 * D_FF), F32)

    x = x_ref[0]
    h = _rms(x, gf_ref[...]).astype(BF16)
    acc = jnp.zeros((tm, D_MODEL), F32)
    for c in range(D_FF // FF_CHUNK):
        cu = []
        for half in range(2):
            col = half * D_FF + c * FF_CHUNK
            cs = slice(col, col + FF_CHUNK)
            u = jnp.dot(h, wup_ref[:, cs], preferred_element_type=F32)
            usc_ref[half, 0:FF_HALO, :] = carry_ref[:, cs]
            usc_ref[half, FF_HALO:FF_HALO + tm, :] = u
            carry_ref[:, cs] = u[tm - FF_HALO:tm]
            cu.append(usc_ref[half, FF_HALO - 2:FF_HALO - 2 + tm, :] * dw_ref[0:1, cs]
                      + usc_ref[half, FF_HALO - 1:FF_HALO - 1 + tm, :] * dw_ref[1:2, cs]
                      + u * dw_ref[2:3, cs])
        f = _gelu(cu[0]) * cu[1]
        acc = acc + jnp.dot(f.astype(BF16), wdn_ref[c * FF_CHUNK:(c + 1) * FF_CHUNK, :], preferred_element_type=F32)
    nf_ref[0] = carry_ref[FF_HALO - 2:FF_HALO, :]
    x2 = x + acc
    gate = jax.nn.sigmoid(jnp.dot(_rms(x2, gp_ref[...]).astype(BF16), wg_ref[...], preferred_element_type=F32))
    o_ref[0] = x2 + gate * jnp.dot(pe_ref[0].astype(BF16), wp_ref[...], preferred_element_type=F32)


def ffn_prompt(x3, pe3, g_ffn, w_up, ffn_dw, w_down, g_ple, w_gate, w_ple, tm=512):
    b, t, _ = x3.shape
    tile = lambda c: pl.BlockSpec((1, tm, c), lambda i, j: (i, j, 0))
    return pl.pallas_call(
        functools.partial(_ffn_kernel, tm=tm),
        grid=(b, t // tm),
        in_specs=[tile(D_MODEL), tile(PLE_DIM), _const_spec((1, D_MODEL)), _const_spec((D_MODEL, 2 * D_FF)),
                  _const_spec((FFN_CONV_W, 2 * D_FF)), _const_spec((D_FF, D_MODEL)), _const_spec((1, D_MODEL)),
                  _const_spec((D_MODEL, D_MODEL)), _const_spec((PLE_DIM, D_MODEL))],
        out_specs=[tile(D_MODEL), pl.BlockSpec((1, FFN_CONV_W - 1, 2 * D_FF), lambda i, j: (i, 0, 0))],
        out_shape=[jax.ShapeDtypeStruct((b, t, D_MODEL), F32),
                   jax.ShapeDtypeStruct((b, FFN_CONV_W - 1, 2 * D_FF), F32)],
        scratch_shapes=[pltpu.VMEM((2, tm + FF_HALO, FF_CHUNK), F32), pltpu.VMEM((FF_HALO, 2 * D_FF), F32)],
        compiler_params=_cparams("arbitrary", "arbitrary"),
        name="ffn",
    )(x3, pe3, g_ffn.reshape(1, D_MODEL), w_up, ffn_dw, w_down, g_ple.reshape(1, D_MODEL), w_gate, w_ple)


def _norm_kernel(x_ref, g_ref, o_ref):
    o_ref[...] = _rms(x_ref[...], g_ref[...])


def final_norm(x2, g, tm=512):
    n = x2.shape[0]
    return pl.pallas_call(
        _norm_kernel,
        grid=(n // tm,),
        in_specs=[pl.BlockSpec((tm, D_MODEL), lambda i: (i, 0)), _const_spec((1, D_MODEL))],
        out_specs=pl.BlockSpec((tm, D_MODEL), lambda i: (i, 0)),
        out_shape=jax.ShapeDtypeStruct((n, D_MODEL), F32),
        compiler_params=_cparams("parallel"),
        name="final_norm",
    )(x2, g.reshape(1, D_MODEL))


def prompt_layer(x3, pe3, lw, tables):
    b, t, _ = x3.shape
    n = b * t
    (a, zb, qh, ql, kv4, win, cmpx, slc, winb, ng, mg) = inproj(
        x3.reshape(n, D_MODEL), lw['norm_mix'], lw['w_in'], tables, t, 512)
    r3 = lambda v: v.reshape(b, t, v.shape[-1])
    a3 = r3(a)
    zb3 = r3(zb)
    c = conv_a(a3, lw['conv_dw'], lw['conv_b'], lw['conv_ln_g'], lw['conv_ln_b'])
    pb = pool_b(zb3, lw['pool_w'], lw['pool_scale'])
    kcvc = compress_prompt(r3(cmpx), lw['cmp'])
    oc = nsa_prompt(r3(qh), r3(ql), r3(slc), r3(winb), kcvc, r3(ng))
    x1 = merge(c.reshape(n, 512), pb.reshape(n, 512), oc.reshape(n, 512), mg, x3.reshape(n, D_MODEL),
               lw['w_a_out'], lw['w_b_out'], lw['w_c_out'], lw['w_o'])
    x2, new_ffn = ffn_prompt(x1.reshape(b, t, D_MODEL), pe3, lw['norm_ffn'], lw['w_up'], lw['ffn_dw'], lw['w_down'],
                             lw['norm_ple'], lw['w_ple_gate'], lw['w_ple'])
    wp = min(WINDOW, t)
    new_kv = kv4.reshape(b, t, 4, N_KV_HEADS, HEAD_DIM)
    new_win = r3(win)[:, t - wp:].reshape(b, wp, 2, N_KV_HEADS, HEAD_DIM)
    new_conv = a3[:, t - (CONV_W - 1):]
    new_pool = zb3[:, t - (POOL_MAX - 1):]
    return x2, (new_conv, new_pool, new_ffn, new_kv, new_win)


def layer_weights(i, norm_mix, w_in, conv_dw, conv_b, conv_ln_g, conv_ln_b, w_a_out, pool_w, pool_scale, w_b_out,
                  pe_cmp, w_cmp1, w_cmp2, w_c_out, w_o, norm_ffn, w_up, ffn_dw, w_down, norm_ple, w_ple_gate, w_ple):
    return dict(norm_mix=norm_mix[i], w_in=_pack_w_in(w_in[i]), conv_dw=conv_dw[i], conv_b=conv_b[i],
                conv_ln_g=conv_ln_g[i], conv_ln_b=conv_ln_b[i], w_a_out=w_a_out[i].astype(BF16), pool_w=pool_w[i],
                pool_scale=pool_scale[i], w_b_out=w_b_out[i].astype(BF16),
                cmp=_compress_weights(pe_cmp[i], w_cmp1[i], w_cmp2[i]), w_c_out=w_c_out[i].astype(BF16),
                w_o=w_o[i].astype(BF16), norm_ffn=norm_ffn[i], w_up=w_up[i].astype(BF16), ffn_dw=ffn_dw[i],
                w_down=w_down[i].astype(BF16), norm_ple=norm_ple[i], w_ple_gate=w_ple_gate[i].astype(BF16),
                w_ple=w_ple[i].astype(BF16))


def rmsnorm(x, g):
    xf = x.astype(F32)
    y = xf * lax.rsqrt(jnp.mean(xf * xf, axis=-1, keepdims=True) + EPS)
    return (y * g.astype(F32)).astype(x.dtype)


def layernorm(x, g, b):
    xf = x.astype(F32)
    mu = jnp.mean(xf, axis=-1, keepdims=True)
    var = jnp.mean(jnp.square(xf - mu), axis=-1, keepdims=True)
    y = (xf - mu) * lax.rsqrt(var + EPS) * g.astype(F32) + b.astype(F32)
    return y.astype(x.dtype)


def rotary(x, pos):
    half = ROT_DIM // 2
    inv = ROPE_THETA ** (-jnp.arange(half, dtype=F32) * 2.0 / ROT_DIM)
    ang = pos.astype(F32)[:, None] * inv[None, :]
    ang = ang.reshape((ang.shape[0],) + (1,) * (x.ndim - 3) + (half,))
    cos, sin = jnp.cos(ang), jnp.sin(ang)
    x1 = x[..., :half].astype(F32)
    x2 = x[..., half:ROT_DIM].astype(F32)
    rot = jnp.concatenate([x1 * cos - x2 * sin, x2 * cos + x1 * sin], axis=-1).astype(x.dtype)
    return jnp.concatenate([rot, x[..., ROT_DIM:]], axis=-1)


def masked_softmax(s, mask):
    s = jnp.where(mask, s.astype(F32), NEG)
    return jnp.where(mask, jax.nn.softmax(s, axis=-1), 0.0)


def causal_dwconv(u, prev, w):
    K, C = w.shape
    ext = jnp.concatenate([prev, u], axis=1)
    y = lax.conv_general_dilated(ext, w[:, None, :], window_strides=(1,), padding='VALID',
                                 dimension_numbers=('NWC', 'WIO', 'NWC'), feature_group_count=C)
    return y, ext[:, -(K - 1):]


def pool_mix(u, prev, pos0, w_grp, scale):
    B, T, C = u.shape
    P1 = POOL_MAX - 1
    ext = jnp.concatenate([prev, u], axis=1)
    cs = jnp.concatenate([jnp.zeros((B, 1, C), F32), jnp.cumsum(ext.astype(F32), axis=1)], axis=1)
    upto = cs[:, P1 + 1:P1 + 1 + T]
    pos = (pos0 + jnp.arange(T)).astype(F32)
    means = []
    for g, w in enumerate(POOL_WINDOWS):
        ch = slice(g * POOL_GC, (g + 1) * POOL_GC)
        start = cs[:, P1 + 1 - w:P1 + 1 - w + T, ch]
        cnt = jnp.minimum(pos + 1.0, float(w))[None, :, None]
        means.append((upto[..., ch] - start) / cnt)
    pooled = jnp.concatenate(means, axis=-1)
    d = (pooled - u.astype(F32)).astype(u.dtype).reshape(B, T, len(POOL_WINDOWS), POOL_GC)
    y = jnp.concatenate([mm(d[:, :, g], w_grp[g]) for g in range(len(POOL_WINDOWS))], axis=-1) * scale
    return y, ext[:, -P1:]


def compress(k, pe, w1, w2):
    B, L, G, D = k.shape
    nc = (L - CMP_BLOCK) // CMP_STRIDE + 1
    idx = jnp.arange(nc)[:, None] * CMP_STRIDE + jnp.arange(CMP_BLOCK)[None, :]
    blk = k[:, idx] + pe[None, None, :, None, :]
    flat = blk.transpose(0, 1, 3, 2, 4).reshape(B, nc, G, CMP_BLOCK * D)
    out = mm(jax.nn.gelu(mm(flat, w1)), w2)
    end = jnp.arange(nc) * CMP_STRIDE + CMP_BLOCK - 1
    return out, end


def to_blocks(k):
    B, L, G, D = k.shape
    ns = -(-L // SEL_BLOCK)
    k = jnp.pad(k, ((0, 0), (0, ns * SEL_BLOCK - L), (0, 0), (0, 0)))
    return k.reshape(B, ns, SEL_BLOCK, G, D).transpose(0, 3, 1, 2, 4)


def nsa_attend(q, ng, pos_q, kc, vc, cmp_end, ks, vs, kw, vw, pos_w):
    B, Tq, G, R, D = q.shape
    scale = D ** -0.5
    s = jnp.einsum('bqgrd,bngd->bgrqn', q, kc).astype(F32) * scale
    p_c = masked_softmax(s, cmp_end[None, :] <= pos_q[:, None])
    o_c = jnp.einsum('bgrqn,bngd->bqgrd', p_c.astype(vc.dtype), vc)
    nc, ns = kc.shape[1], ks.shape[2]
    owner = (jnp.arange(nc) * CMP_STRIDE) // SEL_BLOCK
    onehot = (owner[:, None] == jnp.arange(ns)[None, :]).astype(F32)
    imp = jnp.einsum('bgrqn,nj->bgqj', p_c, onehot)
    q_blk = pos_q // SEL_BLOCK
    j = jnp.arange(ns)[None, :]
    forced = (j == 0) | (j == q_blk[:, None]) | (j == q_blk[:, None] - 1)
    allowed = j <= q_blk[:, None]
    imp = jnp.where(allowed, jnp.where(forced, FORCE_SCORE, imp), -1.0)
    n_sel = min(N_SELECT, ns)
    vals, idx = lax.top_k(imp, n_sel)
    bi = jnp.arange(B)[:, None, None, None]
    gi = jnp.arange(G)[None, :, None, None]
    kg = ks[bi, gi, idx]
    vg = vs[bi, gi, idx]
    kpos = idx[..., None] * SEL_BLOCK + jnp.arange(SEL_BLOCK)
    m_s = (vals[..., None] >= 0.0) & (kpos <= pos_q[None, None, :, None, None])
    m_s = m_s.reshape(B, G, 1, Tq, n_sel * SEL_BLOCK)
    s = jnp.einsum('bqgrd,bgqkld->bgrqkl', q, kg).reshape(B, G, R, Tq, n_sel * SEL_BLOCK) * scale
    p_s = masked_softmax(s, m_s)
    o_s = jnp.einsum('bgrqm,bgqmd->bqgrd', p_s.astype(vg.dtype), vg.reshape(B, G, Tq, n_sel * SEL_BLOCK, D))
    m_w = (pos_w[None, :] <= pos_q[:, None]) & (pos_w[None, :] > pos_q[:, None] - WINDOW) & (pos_w[None, :] >= 0)
    s = jnp.einsum('bqgrd,blgd->bgrql', q, kw).astype(F32) * scale
    p_w = masked_softmax(s, m_w)
    o_w = jnp.einsum('bgrql,blgd->bqgrd', p_w.astype(vw.dtype), vw)
    return ng[..., 0, None] * o_c + ng[..., 1, None] * o_s + ng[..., 2, None] * o_w


def nsa_sample(q, kv, ng, pe_c, w1, w2, cache_kv, page_table, win_state):
    B, T = q.shape[:2]
    past_len = page_table.shape[1] * cache_kv.shape[1]
    past = cache_kv[page_table].reshape(B, past_len, 4, N_KV_HEADS, HEAD_DIM)
    full = jnp.concatenate([past, kv[:, :, 0:4]], axis=1)
    kc, end = compress(full[:, :, 0], pe_c[0], w1[0], w2[0])
    vc, _ = compress(full[:, :, 1], pe_c[1], w1[1], w2[1])
    ks, vs = to_blocks(full[:, :, 2]), to_blocks(full[:, :, 3])
    wlen = win_state.shape[1]
    win = jnp.concatenate([win_state, kv[:, :, 4:6]], axis=1)
    pos_w = past_len - wlen + jnp.arange(wlen + T)
    pos_q = past_len + jnp.arange(T)
    o = nsa_attend(q, ng, pos_q, kc, vc, end, ks, vs, win[:, :, 0], win[:, :, 1], pos_w)
    return o.reshape(B, T, N_HEADS * HEAD_DIM), kv[:, :, 0:4], win[:, T:]


def trunk_layer(x, pe_i, prev_conv, prev_pool, prev_ffn, pos0, nsa_fn, prm):
    B, T, _ = x.shape
    pos = pos0 + jnp.arange(T)
    h = rmsnorm(x, prm['norm_mix'])
    z = mm(h, prm['w_in'])
    z_a, z_b, z_q, z_kv, z_ng, z_mg = jnp.split(z, SPLIT_AT, axis=-1)
    a = z_a[..., :CONV_WIDTH] * jax.nn.sigmoid(z_a[..., CONV_WIDTH:])
    c, new_conv = causal_dwconv(a, prev_conv, prm['conv_dw'])
    c = jax.nn.silu(layernorm(c + prm['conv_b'], prm['conv_ln_g'], prm['conv_ln_b']))
    out_a = mm(c, prm['w_a_out'])
    pb, new_pool = pool_mix(z_b, prev_pool, pos0, prm['pool_w'], prm['pool_scale'])
    out_b = mm(pb, prm['w_b_out'])
    q = rotary(z_q.reshape(B, T, N_HEADS, HEAD_DIM), pos).reshape(B, T, N_KV_HEADS, GROUP, HEAD_DIM)
    kv = z_kv.reshape(B, T, 6, N_KV_HEADS, HEAD_DIM)
    keys = rotary(kv[:, :, 0::2], pos)
    kv = jnp.stack([keys, kv[:, :, 1::2]], axis=3).reshape(B, T, 6, N_KV_HEADS, HEAD_DIM)
    ng = jax.nn.sigmoid(z_ng.reshape(B, T, N_KV_HEADS, GROUP, 3))
    o_c, new_kv, new_win = nsa_fn(q, kv, ng)
    out_c = mm(o_c, prm['w_c_out'])
    mg = jax.nn.sigmoid(z_mg.reshape(B, T, 3, D_MODEL))
    m = mg[:, :, 0] * out_a + mg[:, :, 1] * out_b + mg[:, :, 2] * out_c
    x = x + mm(m, prm['w_o'])
    u = mm(rmsnorm(x, prm['norm_ffn']), prm['w_up'])
    cu, new_ffn = causal_dwconv(u, prev_ffn, prm['ffn_dw'])
    f = jax.nn.gelu(cu[..., :D_FF]) * cu[..., D_FF:]
    x = x + mm(f, prm['w_down'])
    gate = jax.nn.sigmoid(mm(rmsnorm(x, prm['norm_ple']), prm['w_ple_gate']))
    x = x + gate * mm(pe_i, prm['w_ple'])
    return x, (new_conv, new_pool, new_ffn, new_kv, new_win)


def kernel(x_prompt, x_sample, cache_nsa_kv, state_nsa_win, state_conv, state_pool, state_ffn, page_table,
           p_prompt, p_sample, norm_mix, w_in, conv_dw, conv_b, conv_ln_g, conv_ln_b, w_a_out, pool_w,
           pool_scale, w_b_out, pe_cmp, w_cmp1, w_cmp2, w_c_out, w_o, norm_ffn, w_up, ffn_dw, w_down,
           norm_ple, w_ple_gate, w_ple, norm_final):
    B, T, _ = x_prompt.shape
    past_len = page_table.shape[1] * cache_nsa_kv.shape[2]
    tables = _rope_tables(jnp.arange(T))
    xp, xs = x_prompt, x_sample
    st_p, st_s = [], []
    for i in range(DEPTH):
        lw = layer_weights(i, norm_mix, w_in, conv_dw, conv_b, conv_ln_g, conv_ln_b, w_a_out, pool_w, pool_scale,
                           w_b_out, pe_cmp, w_cmp1, w_cmp2, w_c_out, w_o, norm_ffn, w_up, ffn_dw, w_down, norm_ple,
                           w_ple_gate, w_ple)
        xp, st = prompt_layer(xp, p_prompt[i], lw, tables)
        st_p.append(st)
        prm = dict(norm_mix=norm_mix[i], w_in=w_in[i], conv_dw=conv_dw[i], conv_b=conv_b[i],
                   conv_ln_g=conv_ln_g[i], conv_ln_b=conv_ln_b[i], w_a_out=w_a_out[i], pool_w=pool_w[i],
                   pool_scale=pool_scale[i], w_b_out=w_b_out[i], w_c_out=w_c_out[i], w_o=w_o[i],
                   norm_ffn=norm_ffn[i], w_up=w_up[i], ffn_dw=ffn_dw[i], w_down=w_down[i],
                   norm_ple=norm_ple[i], w_ple_gate=w_ple_gate[i], w_ple=w_ple[i])
        nsa_s = functools.partial(nsa_sample, pe_c=pe_cmp[i], w1=w_cmp1[i], w2=w_cmp2[i],
                                  cache_kv=cache_nsa_kv[i], page_table=page_table, win_state=state_nsa_win[i])
        xs, st = trunk_layer(xs, p_sample[i], state_conv[i], state_pool[i], state_ffn[i], past_len, nsa_s, prm)
        st_s.append(st)
    y_prompt = final_norm(xp.reshape(B * T, D_MODEL), norm_final).reshape(B, T, D_MODEL)
    y_sample = rmsnorm(xs, norm_final)
    stk = lambda lst, k: jnp.stack([s[k] for s in lst])
    return (y_prompt, y_sample, stk(st_p, 3), stk(st_p, 4), stk(st_p, 0), stk(st_p, 1), stk(st_p, 2),
            stk(st_s, 3), stk(st_s, 4), stk(st_s, 0), stk(st_s, 1), stk(st_s, 2))
```

```python
import functools
import numpy as np
import jax
import jax.numpy as jnp
from jax import lax
from jax.experimental import pallas as pl
from jax.experimental.pallas import tpu as pltpu

D_MODEL = 1024
DEPTH = 4
CONV_WIDTH = D_MODEL // 2
POOL_WIDTH = D_MODEL // 2
HEAD_DIM = 64
N_HEADS = (D_MODEL // 2) // HEAD_DIM
N_KV_HEADS = 2
GROUP = N_HEADS // N_KV_HEADS
ROT_DIM = HEAD_DIM // 4
ROPE_THETA = 500000.0
CONV_W = 31
POOL_WINDOWS = (2, 4, 8, 16)
POOL_MAX = max(POOL_WINDOWS)
POOL_GC = POOL_WIDTH // len(POOL_WINDOWS)
CMP_BLOCK = 32
CMP_STRIDE = 16
SEL_BLOCK = 64
N_SELECT = 16
WINDOW = 512
Q_BLOCK = 128
D_FF = ((8 * D_MODEL // 3 + 127) // 128) * 128
FFN_CONV_W = 3
PLE_DIM = 256
EPS = 1e-6
FORCE_SCORE = 1e4
NEG = -1e30

SPLIT_SIZES = (2 * CONV_WIDTH, POOL_WIDTH, N_HEADS * HEAD_DIM, 6 * N_KV_HEADS * HEAD_DIM, 3 * N_HEADS, 3 * D_MODEL)
SPLIT_AT = tuple(int(v) for v in np.cumsum(SPLIT_SIZES)[:-1])

F32 = jnp.float32
BF16 = jnp.bfloat16
LANES = 128
VMEM_LIMIT = 56 * 1024 * 1024
NT = (((1,), (1,)), ((), ()))

NG_PAD = LANES
C_A, C_B, C_Q, C_KV = 0, 2 * CONV_WIDTH, 2 * CONV_WIDTH + POOL_WIDTH, 2 * CONV_WIDTH + POOL_WIDTH + 512
C_NG = C_KV + 768
C_MG = C_NG + NG_PAD
IN_PACKED = C_MG + 3 * D_MODEL
KV_W = N_KV_HEADS * HEAD_DIM


def _cparams(*sem):
    return pltpu.CompilerParams(dimension_semantics=sem, vmem_limit_bytes=VMEM_LIMIT)


def _const_spec(shape):
    nd = len(shape)
    return pl.BlockSpec(shape, lambda *_: (0,) * nd, pipeline_mode=pl.Buffered(1))


def _rms(x, g):
    return x * lax.rsqrt(jnp.mean(x * x, axis=-1, keepdims=True) + EPS) * g


def _gelu(x):
    return 0.5 * x * (1.0 + jnp.tanh(np.sqrt(2.0 / np.pi).astype(np.float32) * (x + 0.044715 * (x * x * x))))


def _mm_kernel(x_ref, w_ref, o_ref):
    o_ref[...] = jnp.dot(x_ref[...].astype(BF16), w_ref[...], preferred_element_type=F32)


def mm(x, w):
    lead = x.shape[:-1]
    k = x.shape[-1]
    m = w.shape[-1]
    x2 = x.reshape(-1, k)
    n = x2.shape[0]
    col_align = LANES if m <= 1024 else 512
    mp = -(-m // col_align) * col_align
    wb = w.astype(BF16)
    if mp != m:
        wb = jnp.pad(wb, ((0, 0), (0, mp - m)))
    tm = 512 if n % 512 == 0 else (256 if n % 256 == 0 else n)
    npad = n
    if n % 8 != 0:
        npad = -(-n // 8) * 8
        x2 = jnp.pad(x2, ((0, npad - n), (0, 0)))
        tm = npad
    tn = mp
    for cand in (1024, 768, 512, 384, 256, 128):
        if mp % cand == 0:
            tn = cand
            break
    out = pl.pallas_call(
        _mm_kernel,
        grid=(npad // tm, mp // tn),
        in_specs=[pl.BlockSpec((tm, k), lambda i, j: (i, 0)),
                  pl.BlockSpec((k, tn), lambda i, j: (0, j))],
        out_specs=pl.BlockSpec((tm, tn), lambda i, j: (i, j)),
        out_shape=jax.ShapeDtypeStruct((npad, mp), F32),
        compiler_params=_cparams("parallel", "parallel"),
    )(x2, wb)
    return out[:n, :m].reshape(lead + (m,))


def _rope_tables(pos):
    half = ROT_DIM // 2
    inv = ROPE_THETA ** (-jnp.arange(half, dtype=F32) * 2.0 / ROT_DIM)
    ang = pos.astype(F32)[:, None] * inv[None, :]
    cos, sin = jnp.cos(ang), jnp.sin(ang)
    rows = pos.shape[0]
    ones = jnp.ones((rows, HEAD_DIM - ROT_DIM), F32)
    zeros = jnp.zeros((rows, HEAD_DIM - ROT_DIM), F32)
    z8 = jnp.zeros((rows, half), F32)
    cos_h = jnp.concatenate([cos, cos, ones], axis=1)
    sup_h = jnp.concatenate([-sin, z8, zeros], axis=1)
    sdn_h = jnp.concatenate([z8, sin, zeros], axis=1)
    rep = LANES // HEAD_DIM
    return jnp.tile(cos_h, (1, rep)), jnp.tile(sup_h, (1, rep)), jnp.tile(sdn_h, (1, rep))


def _rope128(x, cos, sup, sdn):
    return x * cos + pltpu.roll(x, LANES - ROT_DIM // 2, 1) * sup + pltpu.roll(x, ROT_DIM // 2, 1) * sdn


def _inproj_kernel(x_ref, g_ref, w_ref, cos_ref, sup_ref, sdn_ref,
                   a_ref, zb_ref, qh_ref, ql_ref, kv4_ref, win_ref, cmp_ref, slc_ref, winb_ref, ng_ref, mg_ref):
    h = _rms(x_ref[...], g_ref[...]).astype(BF16)

    def proj(c0, n):
        return jnp.dot(h, w_ref[:, c0:c0 + n], preferred_element_type=F32)

    za = proj(C_A, 2 * CONV_WIDTH)
    a_ref[...] = za[:, :CONV_WIDTH] * jax.nn.sigmoid(za[:, CONV_WIDTH:])
    zb_ref[...] = proj(C_B, POOL_WIDTH)
    cos, sup, sdn = cos_ref[...], sup_ref[...], sdn_ref[...]
    zq = proj(C_Q, N_HEADS * HEAD_DIM)
    scale = HEAD_DIM ** -0.5
    for c in range(N_HEADS * HEAD_DIM // LANES):
        sl = slice(c * LANES, (c + 1) * LANES)
        qr = _rope128(zq[:, sl], cos, sup, sdn) * scale
        qh = qr.astype(BF16)
        qh_ref[:, sl] = qh
        ql_ref[:, sl] = (qr - qh.astype(F32)).astype(BF16)
    zkv = proj(C_KV, 6 * KV_W)
    kind = [zkv[:, j * KV_W:(j + 1) * KV_W] for j in range(6)]
    for j in (0, 2, 4):
        kind[j] = _rope128(kind[j], cos, sup, sdn)
    for j in range(4):
        kv4_ref[:, j * KV_W:(j + 1) * KV_W] = kind[j]
    cmp_ref[:, 0:KV_W] = kind[0]
    cmp_ref[:, KV_W:2 * KV_W] = kind[1]
    slc_ref[:, 0:KV_W] = kind[2].astype(BF16)
    slc_ref[:, KV_W:2 * KV_W] = kind[3].astype(BF16)
    win_ref[:, 0:KV_W] = kind[4]
    win_ref[:, KV_W:2 * KV_W] = kind[5]
    winb_ref[:, 0:KV_W] = kind[4].astype(BF16)
    winb_ref[:, KV_W:2 * KV_W] = kind[5].astype(BF16)
    ng_ref[...] = jax.nn.sigmoid(proj(C_NG, NG_PAD))
    mg_ref[...] = jax.nn.sigmoid(proj(C_MG, 3 * D_MODEL))


def _pack_w_in(w_in):
    parts = jnp.split(w_in, SPLIT_AT, axis=-1)
    ng = jnp.pad(parts[4], ((0, 0), (0, NG_PAD - parts[4].shape[1])))
    return jnp.concatenate([parts[0], parts[1], parts[2], parts[3], ng, parts[5]], axis=1).astype(BF16)


def inproj(x2, g, w_packed, tables, period, tm):
    n = x2.shape[0]
    pblocks = period // tm
    row = lambda c: pl.BlockSpec((tm, c), lambda i: (i, 0))
    tab = pl.BlockSpec((tm, LANES), lambda i: (i % pblocks, 0))
    outs = [(CONV_WIDTH, F32), (POOL_WIDTH, F32), (512, BF16), (512, BF16), (4 * KV_W, F32), (2 * KV_W, F32),
            (2 * KV_W, F32), (2 * KV_W, BF16), (2 * KV_W, BF16), (NG_PAD, F32), (3 * D_MODEL, F32)]
    return pl.pallas_call(
        _inproj_kernel,
        grid=(n // tm,),
        in_specs=[row(D_MODEL), _const_spec((1, D_MODEL)), _const_spec((D_MODEL, IN_PACKED)), tab, tab, tab],
        out_specs=[row(c) for c, _ in outs],
        out_shape=[jax.ShapeDtypeStruct((n, c), dt) for c, dt in outs],
        compiler_params=_cparams("parallel"),
        name="inproj",
    )(x2, g.reshape(1, D_MODEL), w_packed, *tables)


CONV_HALO = 32


def _conva_kernel(a_ref, w_ref, b_ref, g_ref, beta_ref, c_ref, ext_ref, *, tm, rc):
    @pl.when(pl.program_id(1) == 0)
    def _():
        ext_ref[0:CONV_HALO, :] = jnp.zeros((CONV_HALO, CONV_WIDTH), F32)

    ext_ref[CONV_HALO:CONV_HALO + tm, :] = a_ref[0]
    off = CONV_HALO - (CONV_W - 1)
    for r0 in range(0, tm, rc):
        acc = jnp.zeros((rc, CONV_WIDTH), F32)
        for k in range(CONV_W):
            acc = acc + ext_ref[r0 + off + k:r0 + off + k + rc, :] * w_ref[k:k + 1, :]
        y = acc + b_ref[...]
        mu = jnp.mean(y, axis=-1, keepdims=True)
        d = y - mu
        var = jnp.mean(d * d, axis=-1, keepdims=True)
        yn = d * lax.rsqrt(var + EPS) * g_ref[...] + beta_ref[...]
        c_ref[0, r0:r0 + rc, :] = (yn * jax.nn.sigmoid(yn)).astype(BF16)
    ext_ref[0:CONV_HALO, :] = ext_ref[tm:tm + CONV_HALO, :]


def conv_a(a3, conv_dw, conv_b, ln_g, ln_b, tm=256, rc=32):
    b, t, c = a3.shape
    vec = lambda v: v.reshape(1, c)
    return pl.pallas_call(
        functools.partial(_conva_kernel, tm=tm, rc=rc),
        grid=(b, t // tm),
        in_specs=[pl.BlockSpec((1, tm, c), lambda i, j: (i, j, 0)), _const_spec((CONV_W, c)),
                  _const_spec((1, c)), _const_spec((1, c)), _const_spec((1, c))],
        out_specs=pl.BlockSpec((1, tm, c), lambda i, j: (i, j, 0)),
        out_shape=jax.ShapeDtypeStruct((b, t, c), BF16),
        scratch_shapes=[pltpu.VMEM((tm + CONV_HALO, c), F32)],
        compiler_params=_cparams("arbitrary", "arbitrary"),
        name="conv_a",
    )(a3, conv_dw, vec(conv_b), vec(ln_g), vec(ln_b))


POOL_HALO = 16


def _pool_kernel(z_ref, pw_ref, sc_ref, pb_ref, ext_ref, *, tm, rc):
    i = pl.program_id(1)

    @pl.when(i == 0)
    def _():
        ext_ref[0:POOL_HALO, :] = jnp.zeros((POOL_HALO, POOL_WIDTH), F32)

    ext_ref[POOL_HALO:POOL_HALO + tm, :] = z_ref[0]
    for r0 in range(0, tm, rc):
        pos = (i * tm + r0 + lax.broadcasted_iota(jnp.int32, (rc, 1), 0)).astype(F32)
        for gi, w in enumerate(POOL_WINDOWS):
            lanes = slice(gi * POOL_GC, (gi + 1) * POOL_GC)
            base = POOL_HALO + r0
            cur = ext_ref[base:base + rc, lanes]
            s = cur
            for k in range(1, w):
                s = s + ext_ref[base - k:base - k + rc, lanes]
            cnt = jnp.minimum(pos + 1.0, float(w))
            d = s / cnt - cur
            y = jnp.dot(d.astype(BF16), pw_ref[gi], preferred_element_type=F32) * sc_ref[:, lanes]
            pb_ref[0, r0:r0 + rc, lanes] = y.astype(BF16)
    ext_ref[0:POOL_HALO, :] = ext_ref[tm:tm + POOL_HALO, :]


def pool_b(z3, pool_w, pool_scale, tm=512, rc=128):
    b, t, c = z3.shape
    return pl.pallas_call(
        functools.partial(_pool_kernel, tm=tm, rc=rc),
        grid=(b, t // tm),
        in_specs=[pl.BlockSpec((1, tm, c), lambda i, j: (i, j, 0)),
                  _const_spec((len(POOL_WINDOWS), POOL_GC, POOL_GC)), _const_spec((1, c))],
        out_specs=pl.BlockSpec((1, tm, c), lambda i, j: (i, j, 0)),
        out_shape=jax.ShapeDtypeStruct((b, t, c), BF16),
        scratch_shapes=[pltpu.VMEM((tm + POOL_HALO, c), F32)],
        compiler_params=_cparams("arbitrary", "arbitrary"),
        name="pool_b",
    )(z3, pool_w.astype(BF16), pool_scale.reshape(1, c))


SEG = CMP_STRIDE
CMP_LANES = 2 * KV_W


def _compress_weights(pe_c, w1, w2):
    eye = jnp.eye(2, dtype=F32)
    w1r = w1.reshape(2, CMP_BLOCK, HEAD_DIM, HEAD_DIM)
    def expand(w):
        return jnp.einsum('kjde,kl,gh->jkgdlhe', w, eye, eye).reshape(SEG * CMP_LANES, CMP_LANES).astype(BF16)
    wa, wb = expand(w1r[:, :SEG]), expand(w1r[:, SEG:])
    w2bd = jnp.einsum('kde,kl,gh->kgdlhe', w2, eye, eye).reshape(CMP_LANES, CMP_LANES).astype(BF16)
    def pe_row(p):
        return jnp.broadcast_to(p.transpose(1, 0, 2)[:, :, None, :], (SEG, 2, N_KV_HEADS, HEAD_DIM)).reshape(1, SEG * CMP_LANES)
    return wa, wb, w2bd, pe_row(pe_c[:, :SEG]), pe_row(pe_c[:, SEG:])


def _compress_kernel(x_ref, pea_ref, peb_ref, wa_ref, wb_ref, w2_ref, o_ref, bsc_ref, *, nseg):
    x = x_ref[0]
    a = jnp.dot((x + pea_ref[...]).astype(BF16), wa_ref[...], preferred_element_type=F32)
    bsc_ref[0:nseg, :] = jnp.dot((x + peb_ref[...]).astype(BF16), wb_ref[...], preferred_element_type=F32)
    bsc_ref[nseg:nseg + 8, :] = jnp.zeros((8, CMP_LANES), F32)
    u = a + bsc_ref[1:nseg + 1, :]
    o_ref[0] = jnp.dot(_gelu(u).astype(BF16), w2_ref[...], preferred_element_type=F32)


def compress_prompt(cmp3, cw):
    b, t, _ = cmp3.shape
    nseg = t // SEG
    wa, wb, w2bd, pea, peb = cw
    xs = cmp3.reshape(b, nseg, SEG * CMP_LANES)
    return pl.pallas_call(
        functools.partial(_compress_kernel, nseg=nseg),
        grid=(b,),
        in_specs=[pl.BlockSpec((1, nseg, SEG * CMP_LANES), lambda i: (i, 0, 0)),
                  _const_spec((1, SEG * CMP_LANES)), _const_spec((1, SEG * CMP_LANES)),
                  _const_spec((SEG * CMP_LANES, CMP_LANES)), _const_spec((SEG * CMP_LANES, CMP_LANES)),
                  _const_spec((CMP_LANES, CMP_LANES))],
        out_specs=pl.BlockSpec((1, nseg, CMP_LANES), lambda i: (i, 0, 0)),
        out_shape=jax.ShapeDtypeStruct((b, nseg, CMP_LANES), F32),
        scratch_shapes=[pltpu.VMEM((nseg + 8, CMP_LANES), F32)],
        compiler_params=_cparams("arbitrary"),
        name="compress",
    )(xs, pea, peb, wa, wb, w2bd)


QA = LANES + HEAD_DIM
Q3 = 3 * HEAD_DIM


def _softmax_rows(s, mask):
    s = jnp.where(mask, s, NEG)
    m = jnp.max(s, axis=-1, keepdims=True)
    e = jnp.where(mask, jnp.exp(s - m), 0.0)
    l = jnp.sum(e, axis=-1, keepdims=True)
    return e * (1.0 / jnp.where(l > 0.0, l, 1.0))


def _nsa_kernel(qh_ref, ql_ref, slc_ref, w0_ref, w1_ref, w2_ref, kcvc_ref, oh_ref, ng_ref, o_ref,
                kaug_ref, vsel_ref, kc3_ref, vc_ref, q3_ref, qaug_ref, m_ref, l_ref, acc_ref,
                *, t_len, tq, kc_len, ncmp):
    i = pl.program_id(1)
    nsel = t_len // SEL_BLOCK
    rows = GROUP * tq

    @pl.when(i == 0)
    def _():
        blk = lax.broadcasted_iota(jnp.int32, (t_len, LANES), 0) // SEL_BLOCK
        lane = lax.broadcasted_iota(jnp.int32, (t_len, LANES), 1)
        onehot = jnp.where(blk == lane, 1.0, 0.0).astype(BF16)
        kcvc = kcvc_ref[0]
        for g in range(N_KV_HEADS):
            kaug_ref[g, :, 0:LANES] = onehot
            kaug_ref[g, :, LANES:QA] = slc_ref[0, :, g * HEAD_DIM:(g + 1) * HEAD_DIM]
            vsel_ref[g] = slc_ref[0, :, KV_W + g * HEAD_DIM:KV_W + (g + 1) * HEAD_DIM]
            kc = kcvc[:, g * HEAD_DIM:(g + 1) * HEAD_DIM]
            kch = kc.astype(BF16)
            kcl = (kc - kch.astype(F32)).astype(BF16)
            kc3_ref[g, :, 0:HEAD_DIM] = kch
            kc3_ref[g, :, HEAD_DIM:2 * HEAD_DIM] = kch
            kc3_ref[g, :, 2 * HEAD_DIM:Q3] = kcl
            vc_ref[g] = kcvc[:, KV_W + g * HEAD_DIM:KV_W + (g + 1) * HEAD_DIM].astype(BF16)

    s0 = i * tq
    pos4 = s0 + lax.broadcasted_iota(jnp.int32, (rows, 1), 0) % tq
    ng = ng_ref[0]
    c_last = s0 // kc_len

    for g in range(N_KV_HEADS):
        for r in range(GROUP):
            hs = slice((g * GROUP + r) * HEAD_DIM, (g * GROUP + r + 1) * HEAD_DIM)
            rs = slice(r * tq, (r + 1) * tq)
            qh = qh_ref[0, :, hs]
            q3_ref[rs, 0:HEAD_DIM] = qh
            q3_ref[rs, HEAD_DIM:2 * HEAD_DIM] = ql_ref[0, :, hs]
            q3_ref[rs, 2 * HEAD_DIM:Q3] = qh
            qaug_ref[rs, LANES:QA] = qh

        s = lax.dot_general(q3_ref[...], kc3_ref[g], NT, preferred_element_type=F32)
        n_idx = lax.broadcasted_iota(jnp.int32, (1, s.shape[1]), 1)
        end = n_idx * CMP_STRIDE + (CMP_BLOCK - 1)
        mask = jnp.where(n_idx < ncmp, end, t_len + CMP_BLOCK) <= pos4
        p = _softmax_rows(s, mask)
        o_c = jnp.dot(p.astype(BF16), vc_ref[g], preferred_element_type=F32)
        psum = p[0:tq]
        for r in range(1, GROUP):
            psum = psum + p[r * tq:(r + 1) * tq]
        p_hi = psum.astype(BF16)
        p_lo = (psum - p_hi.astype(F32)).astype(BF16)
        imp = (jnp.dot(p_hi, oh_ref[...], preferred_element_type=F32)
               + jnp.dot(p_lo, oh_ref[...], preferred_element_type=F32))

        v = imp.T
        jidx = lax.broadcasted_iota(jnp.int32, (LANES, tq), 0)
        qblk = (s0 + lax.broadcasted_iota(jnp.int32, (1, tq), 1)) // SEL_BLOCK
        forced = (jidx == 0) | (jidx == qblk) | (jidx == qblk - 1)
        v = jnp.where(jidx <= qblk, jnp.where(forced, FORCE_SCORE, v), -1.0)
        v = jnp.where(jidx < nsel, v, -3.0)
        sel = jnp.zeros((LANES, tq), F32)
        for _ in range(min(N_SELECT, nsel)):
            mx = jnp.max(v, axis=0, keepdims=True)
            first = jnp.min(jnp.where(v == mx, jidx, LANES), axis=0, keepdims=True)
            pick = jidx == first
            sel = jnp.where(pick, 1.0, sel)
            v = jnp.where(pick, -2.0, v)
        selneg = jnp.where(sel.T > 0.5, 0.0, NEG).astype(BF16)
        for r in range(GROUP):
            qaug_ref[r * tq:(r + 1) * tq, 0:LANES] = selneg

        m_ref[...] = jnp.full((rows, 1), NEG, F32)
        l_ref[...] = jnp.zeros((rows, 1), F32)
        acc_ref[...] = jnp.zeros((rows, HEAD_DIM), F32)

        def chunk(c, causal):
            k0 = pl.multiple_of(c * kc_len, kc_len)
            kk = kaug_ref[g, pl.ds(k0, kc_len), :]
            vv = vsel_ref[g, pl.ds(k0, kc_len), :]
            sc = lax.dot_general(qaug_ref[...], kk, NT, preferred_element_type=F32)
            if causal:
                kpos = k0 + lax.broadcasted_iota(jnp.int32, (1, kc_len), 1)
                sc = jnp.where(kpos <= pos4, sc, NEG)
            m_prev = m_ref[...]
            m_new = jnp.maximum(m_prev, jnp.max(sc, axis=-1, keepdims=True))
            alpha = jnp.exp(m_prev - m_new)
            pe = jnp.exp(sc - m_new)
            l_ref[...] = alpha * l_ref[...] + jnp.sum(pe, axis=-1, keepdims=True)
            acc_ref[...] = alpha * acc_ref[...] + jnp.dot(pe.astype(BF16), vv, preferred_element_type=F32)
            m_ref[...] = m_new

        def body(c, carry):
            chunk(c, False)
            return carry

        lax.fori_loop(0, c_last, body, 0)
        chunk(c_last, True)
        o_s = acc_ref[...] * (1.0 / l_ref[...])

        kw = jnp.concatenate([w_ref[0, :, g * HEAD_DIM:(g + 1) * HEAD_DIM] for w_ref in (w0_ref, w1_ref, w2_ref)], axis=0)
        vw = jnp.concatenate([w_ref[0, :, KV_W + g * HEAD_DIM:KV_W + (g + 1) * HEAD_DIM]
                              for w_ref in (w0_ref, w1_ref, w2_ref)], axis=0)
        sw = lax.dot_general(q3_ref[:, 0:HEAD_DIM], kw, NT, preferred_element_type=F32)
        pos_w = s0 - 2 * tq + lax.broadcasted_iota(jnp.int32, (1, 3 * tq), 1)
        mw = (pos_w <= pos4) & (pos_w > pos4 - WINDOW) & (pos_w >= 0)
        pw = _softmax_rows(sw, mw)
        o_w = jnp.dot(pw.astype(BF16), vw, preferred_element_type=F32)

        for r in range(GROUP):
            rs = slice(r * tq, (r + 1) * tq)
            c0 = (g * GROUP + r) * 3
            o = (ng[:, c0:c0 + 1] * o_c[rs] + ng[:, c0 + 1:c0 + 2] * o_s[rs] + ng[:, c0 + 2:c0 + 3] * o_w[rs])
            o_ref[0, :, (g * GROUP + r) * HEAD_DIM:(g * GROUP + r + 1) * HEAD_DIM] = o.astype(BF16)


def nsa_prompt(qh3, ql3, slc3, winb3, kcvc, ng3, tq=256, kc_len=512):
    b, t, _ = qh3.shape
    nseg = t // SEG
    ncmp = (t - CMP_BLOCK) // CMP_STRIDE + 1
    owner = (jnp.arange(nseg) * CMP_STRIDE) // SEL_BLOCK
    oh = (owner[:, None] == jnp.arange(LANES)[None, :]).astype(BF16)
    rows = GROUP * tq
    tile = lambda c: pl.BlockSpec((1, tq, c), lambda i, j: (i, j, 0))
    wspec = lambda back: pl.BlockSpec((1, tq, 2 * KV_W), lambda i, j: (i, jnp.maximum(j - back, 0), 0))
    return pl.pallas_call(
        functools.partial(_nsa_kernel, t_len=t, tq=tq, kc_len=kc_len, ncmp=ncmp),
        grid=(b, t // tq),
        in_specs=[tile(512), tile(512),
                  pl.BlockSpec((1, t, 2 * KV_W), lambda i, j: (i, 0, 0)),
                  wspec(2), wspec(1), wspec(0),
                  pl.BlockSpec((1, nseg, CMP_LANES), lambda i, j: (i, 0, 0)),
                  _const_spec((nseg, LANES)), tile(NG_PAD)],
        out_specs=tile(512),
        out_shape=jax.ShapeDtypeStruct((b, t, 512), BF16),
        scratch_shapes=[pltpu.VMEM((N_KV_HEADS, t, QA), BF16), pltpu.VMEM((N_KV_HEADS, t, HEAD_DIM), BF16),
                        pltpu.VMEM((N_KV_HEADS, nseg, Q3), BF16), pltpu.VMEM((N_KV_HEADS, nseg, HEAD_DIM), BF16),
                        pltpu.VMEM((rows, Q3), BF16), pltpu.VMEM((rows, QA), BF16),
                        pltpu.VMEM((rows, 1), F32), pltpu.VMEM((rows, 1), F32), pltpu.VMEM((rows, HEAD_DIM), F32)],
        compiler_params=_cparams("arbitrary", "arbitrary"),
        name="nsa_prompt",
    )(qh3, ql3, slc3, winb3, winb3, winb3, kcvc, oh, ng3)


KPAD = 2 * LANES


def _softmax_cols(s, mask):
    s = jnp.where(mask, s, NEG)
    m = jnp.max(s, axis=0, keepdims=True)
    e = jnp.where(mask, jnp.exp(s - m), 0.0)
    l = jnp.sum(e, axis=0, keepdims=True)
    return e * (1.0 / jnp.where(l > 0.0, l, 1.0))


def _nsa_t_kernel(qh_ref, ql_ref, k_ref, vt_ref, wk0_ref, wk1_ref, wk2_ref, wv0_ref, wv1_ref, wv2_ref,
                  kc_ref, vct_ref, oht_ref, ng_ref, o_ref,
                  kaug_ref, kc3_ref, q3_ref, qaug_ref, m_ref, l_ref, acc_ref, *, t_len, tq, kc_len, ncmp):
    i = pl.program_id(1)
    nsel = t_len // SEL_BLOCK
    cols = GROUP * tq
    nseg = kc_ref.shape[1]

    @pl.when(i == 0)
    def _():
        blk = lax.broadcasted_iota(jnp.int32, (t_len, LANES), 0) // SEL_BLOCK
        lane = lax.broadcasted_iota(jnp.int32, (t_len, LANES), 1)
        onehot = jnp.where(blk == lane, 1.0, 0.0).astype(BF16)
        kc_all = kc_ref[0]
        for g in range(N_KV_HEADS):
            kaug_ref[g, :, 0:LANES] = onehot
            kaug_ref[g, :, LANES:QA] = k_ref[0, :, g * HEAD_DIM:(g + 1) * HEAD_DIM]
            kaug_ref[g, :, QA:KPAD] = jnp.zeros((t_len, KPAD - QA), BF16)
            kc = kc_all[:, g * HEAD_DIM:(g + 1) * HEAD_DIM]
            kch = kc.astype(BF16)
            kc3_ref[g, :, 0:HEAD_DIM] = kch
            kc3_ref[g, :, HEAD_DIM:2 * HEAD_DIM] = kch
            kc3_ref[g, :, 2 * HEAD_DIM:Q3] = (kc - kch.astype(F32)).astype(BF16)
            kc3_ref[g, :, Q3:KPAD] = jnp.zeros((nseg, KPAD - Q3), BF16)
        q3_ref[Q3:KPAD, :] = jnp.zeros((KPAD - Q3, cols), BF16)
        qaug_ref[QA:KPAD, :] = jnp.zeros((KPAD - QA, cols), BF16)

    s0 = i * tq
    pos = s0 + lax.broadcasted_iota(jnp.int32, (1, cols), 1) % tq
    c_last = s0 // kc_len

    for g in range(N_KV_HEADS):
        for r in range(GROUP):
            hs = slice((g * GROUP + r) * HEAD_DIM, (g * GROUP + r + 1) * HEAD_DIM)
            cs = slice(r * tq, (r + 1) * tq)
            qh = qh_ref[0, hs, :]
            q3_ref[0:HEAD_DIM, cs] = qh
            q3_ref[HEAD_DIM:2 * HEAD_DIM, cs] = ql_ref[0, hs, :]
            q3_ref[2 * HEAD_DIM:Q3, cs] = qh
            qaug_ref[LANES:QA, cs] = qh

        s = jnp.dot(kc3_ref[g], q3_ref[...], preferred_element_type=F32)
        n_idx = lax.broadcasted_iota(jnp.int32, (nseg, 1), 0)
        end = jnp.where(n_idx < ncmp, n_idx * CMP_STRIDE + (CMP_BLOCK - 1), t_len + CMP_BLOCK)
        p = _softmax_cols(s, end <= pos)
        vct = vct_ref[0, g * HEAD_DIM:(g + 1) * HEAD_DIM, :].astype(BF16)
        o_c = jnp.dot(vct, p.astype(BF16), preferred_element_type=F32)
        psum = p[:, 0:tq]
        for r in range(1, GROUP):
            psum = psum + p[:, r * tq:(r + 1) * tq]
        p_hi = psum.astype(BF16)
        p_lo = (psum - p_hi.astype(F32)).astype(BF16)
        v = (jnp.dot(oht_ref[...], p_hi, preferred_element_type=F32)
             + jnp.dot(oht_ref[...], p_lo, preferred_element_type=F32))

        jidx = lax.broadcasted_iota(jnp.int32, (LANES, tq), 0)
        qblk = (s0 + lax.broadcasted_iota(jnp.int32, (1, tq), 1)) // SEL_BLOCK
        forced = (jidx == 0) | (jidx == qblk) | (jidx == qblk - 1)
        v = jnp.where(jidx <= qblk, jnp.where(forced, FORCE_SCORE, v), -1.0)
        v = jnp.where(jidx < nsel, v, -3.0)
        sel = jnp.zeros((LANES, tq), F32)
        for _ in range(min(N_SELECT, nsel)):
            mx = jnp.max(v, axis=0, keepdims=True)
            first = jnp.min(jnp.where(v == mx, jidx, LANES), axis=0, keepdims=True)
            pick = jidx == first
            sel = jnp.where(pick, 1.0, sel)
            v = jnp.where(pick, -2.0, v)
        selneg = jnp.where(sel > 0.5, 0.0, NEG).astype(BF16)
        for r in range(GROUP):
            qaug_ref[0:LANES, r * tq:(r + 1) * tq] = selneg

        m_ref[...] = jnp.full((1, cols), NEG, F32)
        l_ref[...] = jnp.zeros((1, cols), F32)
        acc_ref[...] = jnp.zeros((HEAD_DIM, cols), F32)

        def chunk(c, causal):
            k0 = pl.multiple_of(c * kc_len, kc_len)
            sc = jnp.dot(kaug_ref[g, pl.ds(k0, kc_len), :], qaug_ref[...], preferred_element_type=F32)
            if causal:
                kpos = k0 + lax.broadcasted_iota(jnp.int32, (kc_len, 1), 0)
                sc = jnp.where(kpos <= pos, sc, NEG)
            m_prev = m_ref[...]
            m_new = jnp.maximum(m_prev, jnp.max(sc, axis=0, keepdims=True))
            alpha = jnp.exp(m_prev - m_new)
            pe = jnp.exp(sc - m_new)
            l_ref[...] = alpha * l_ref[...] + jnp.sum(pe, axis=0, keepdims=True)
            vt = vt_ref[0, g * HEAD_DIM:(g + 1) * HEAD_DIM, pl.ds(k0, kc_len)]
            acc_ref[...] = alpha * acc_ref[...] + jnp.dot(vt, pe.astype(BF16), preferred_element_type=F32)
            m_ref[...] = m_new

        def body(c, carry):
            chunk(c, False)
            return carry

        lax.fori_loop(0, c_last, body, 0)
        chunk(c_last, True)
        o_s = acc_ref[...] * (1.0 / l_ref[...])

        ks = slice(g * HEAD_DIM, (g + 1) * HEAD_DIM)
        kw = jnp.concatenate([w[0, :, ks] for w in (wk0_ref, wk1_ref, wk2_ref)], axis=0)
        vwt = jnp.concatenate([w[0, ks, :] for w in (wv0_ref, wv1_ref, wv2_ref)], axis=1)
        sw = jnp.dot(kw, q3_ref[0:HEAD_DIM, :], preferred_element_type=F32)
        pos_w = s0 - 2 * tq + lax.broadcasted_iota(jnp.int32, (3 * tq, 1), 0)
        mw = (pos_w <= pos) & (pos_w > pos - WINDOW) & (pos_w >= 0)
        pw = _softmax_cols(sw, mw)
        o_w = jnp.dot(vwt, pw.astype(BF16), preferred_element_type=F32)

        for r in range(GROUP):
            cs = slice(r * tq, (r + 1) * tq)
            c0 = (g * GROUP + r) * 3
            o = (ng_ref[0, c0:c0 + 1, :] * o_c[:, cs] + ng_ref[0, c0 + 1:c0 + 2, :] * o_s[:, cs]
                 + ng_ref[0, c0 + 2:c0 + 3, :] * o_w[:, cs])
            o_ref[0, (g * GROUP + r) * HEAD_DIM:(g * GROUP + r + 1) * HEAD_DIM, :] = o.astype(BF16)


def nsa_prompt_t(qh3, ql3, slc3, winb3, kcvc, ng3, tq=256, kc_len=512):
    b, t, _ = qh3.shape
    nseg = t // SEG
    ncmp = (t - CMP_BLOCK) // CMP_STRIDE + 1
    owner = (jnp.arange(nseg) * CMP_STRIDE) // SEL_BLOCK
    oht = (jnp.arange(LANES)[:, None] == owner[None, :]).astype(BF16)
    tr = lambda v: v.transpose(0, 2, 1)
    cols = GROUP * tq
    qspec = pl.BlockSpec((1, 512, tq), lambda i, j: (i, 0, j))
    wk = lambda back: pl.BlockSpec((1, tq, KV_W), lambda i, j: (i, jnp.maximum(j - back, 0), 0))
    wv = lambda back: pl.BlockSpec((1, KV_W, tq), lambda i, j: (i, 1, jnp.maximum(j - back, 0)))
    winbt = tr(winb3)
    out_t = pl.pallas_call(
        functools.partial(_nsa_t_kernel, t_len=t, tq=tq, kc_len=kc_len, ncmp=ncmp),
        grid=(b, t // tq),
        in_specs=[qspec, qspec,
                  pl.BlockSpec((1, t, KV_W), lambda i, j: (i, 0, 0)),
                  pl.BlockSpec((1, KV_W, t), lambda i, j: (i, 1, 0)),
                  wk(2), wk(1), wk(0), wv(2), wv(1), wv(0),
                  pl.BlockSpec((1, nseg, KV_W), lambda i, j: (i, 0, 0)),
                  pl.BlockSpec((1, KV_W, nseg), lambda i, j: (i, 1, 0)),
                  _const_spec((LANES, nseg)),
                  pl.BlockSpec((1, NG_PAD, tq), lambda i, j: (i, 0, j))],
        out_specs=qspec,
        out_shape=jax.ShapeDtypeStruct((b, 512, t), BF16),
        scratch_shapes=[pltpu.VMEM((N_KV_HEADS, t, KPAD), BF16), pltpu.VMEM((N_KV_HEADS, nseg, KPAD), BF16),
                        pltpu.VMEM((KPAD, cols), BF16), pltpu.VMEM((KPAD, cols), BF16),
                        pltpu.VMEM((1, cols), F32), pltpu.VMEM((1, cols), F32), pltpu.VMEM((HEAD_DIM, cols), F32)],
        compiler_params=_cparams("arbitrary", "arbitrary"),
        name="nsa_prompt",
    )(tr(qh3), tr(ql3), slc3, tr(slc3), winb3, winb3, winb3, winbt, winbt, winbt, kcvc, tr(kcvc), oht, tr(ng3))
    return tr(out_t)


def _merge_kernel(c_ref, pb_ref, oc_ref, mg_ref, x_ref, wa_ref, wb_ref, wc_ref, wo_ref, o_ref):
    out_a = jnp.dot(c_ref[...], wa_ref[...], preferred_element_type=F32)
    out_b = jnp.dot(pb_ref[...], wb_ref[...], preferred_element_type=F32)
    out_c = jnp.dot(oc_ref[...], wc_ref[...], preferred_element_type=F32)
    m = (mg_ref[:, 0:D_MODEL] * out_a + mg_ref[:, D_MODEL:2 * D_MODEL] * out_b
         + mg_ref[:, 2 * D_MODEL:3 * D_MODEL] * out_c)
    o_ref[...] = x_ref[...] + jnp.dot(m.astype(BF16), wo_ref[...], preferred_element_type=F32)


def merge(c2, pb2, oc2, mg2, x2, wa, wb, wc, wo, tm=512):
    n = x2.shape[0]
    row = lambda c: pl.BlockSpec((tm, c), lambda i: (i, 0))
    return pl.pallas_call(
        _merge_kernel,
        grid=(n // tm,),
        in_specs=[row(512), row(512), row(512), row(3 * D_MODEL), row(D_MODEL),
                  _const_spec((512, D_MODEL)), _const_spec((512, D_MODEL)), _const_spec((512, D_MODEL)),
                  _const_spec((D_MODEL, D_MODEL))],
        out_specs=row(D_MODEL),
        out_shape=jax.ShapeDtypeStruct((n, D_MODEL), F32),
        compiler_params=_cparams("parallel"),
        name="merge",
    )(c2, pb2, oc2, mg2, x2, wa, wb, wc, wo)


FF_CHUNK = 256
FF_HALO = 8


def _ffn_kernel(x_ref, pe_ref, gf_ref, wup_ref, dw_ref, wdn_ref, gp_ref, wg_ref, wp_ref, o_ref, nf_ref,
                usc_ref, carry_ref, *, tm):
    @pl.when(pl.program_id(1) == 0)
    def _():
        carry_ref[...] = jnp.zeros((FF_HALO, 2 * D_FF), F32)

    x = x_ref[0]
    h = _rms(x, gf_ref[...]).astype(BF16)
    acc = jnp.zeros((tm, D_MODEL), F32)
    for c in range(D_FF // FF_CHUNK):
        cu = []
        for half in range(2):
            col = half * D_FF + c * FF_CHUNK
            cs = slice(col, col + FF_CHUNK)
            u = jnp.dot(h, wup_ref[:, cs], preferred_element_type=F32)
            usc_ref[half, 0:FF_HALO, :] = carry_ref[:, cs]
            usc_ref[half, FF_HALO:FF_HALO + tm, :] = u
            carry_ref[:, cs] = u[tm - FF_HALO:tm]
            cu.append(usc_ref[half, FF_HALO - 2:FF_HALO - 2 + tm, :] * dw_ref[0:1, cs]
                      + usc_ref[half, FF_HALO - 1:FF_HALO - 1 + tm, :] * dw_ref[1:2, cs]
                      + u * dw_ref[2:3, cs])
        f = _gelu(cu[0]) * cu[1]
        acc = acc + jnp.dot(f.astype(BF16), wdn_ref[c * FF_CHUNK:(c + 1) * FF_CHUNK, :], preferred_element_type=F32)
    nf_ref[0] = carry_ref[FF_HALO - 2:FF_HALO, :]
    x2 = x + acc
    gate = jax.nn.sigmoid(jnp.dot(_rms(x2, gp_ref[...]).astype(BF16), wg_ref[...], preferred_element_type=F32))
    o_ref[0] = x2 + gate * jnp.dot(pe_ref[0].astype(BF16), wp_ref[...], preferred_element_type=F32)


def ffn_prompt(x3, pe3, g_ffn, w_up, ffn_dw, w_down, g_ple, w_gate, w_ple, tm=512):
    b, t, _ = x3.shape
    tile = lambda c: pl.BlockSpec((1, tm, c), lambda i, j: (i, j, 0))
    return pl.pallas_call(
        functools.partial(_ffn_kernel, tm=tm),
        grid=(b, t // tm),
        in_specs=[tile(D_MODEL), tile(PLE_DIM), _const_spec((1, D_MODEL)), _const_spec((D_MODEL, 2 * D_FF)),
                  _const_spec((FFN_CONV_W, 2 * D_FF)), _const_spec((D_FF, D_MODEL)), _const_spec((1, D_MODEL)),
                  _const_spec((D_MODEL, D_MODEL)), _const_spec((PLE_DIM, D_MODEL))],
        out_specs=[tile(D_MODEL), pl.BlockSpec((1, FFN_CONV_W - 1, 2 * D_FF), lambda i, j: (i, 0, 0))],
        out_shape=[jax.ShapeDtypeStruct((b, t, D_MODEL), F32),
                   jax.ShapeDtypeStruct((b, FFN_CONV_W - 1, 2 * D_FF), F32)],
        scratch_shapes=[pltpu.VMEM((2, tm + FF_HALO, FF_CHUNK), F32), pltpu.VMEM((FF_HALO, 2 * D_FF), F32)],
        compiler_params=_cparams("arbitrary", "arbitrary"),
        name="ffn",
    )(x3, pe3, g_ffn.reshape(1, D_MODEL), w_up, ffn_dw, w_down, g_ple.reshape(1, D_MODEL), w_gate, w_ple)


def _norm_kernel(x_ref, g_ref, o_ref):
    o_ref[...] = _rms(x_ref[...], g_ref[...])


def final_norm(x2, g, tm=512):
    n = x2.shape[0]
    return pl.pallas_call(
        _norm_kernel,
        grid=(n // tm,),
        in_specs=[pl.BlockSpec((tm, D_MODEL), lambda i: (i, 0)), _const_spec((1, D_MODEL))],
        out_specs=pl.BlockSpec((tm, D_MODEL), lambda i: (i, 0)),
        out_shape=jax.ShapeDtypeStruct((n, D_MODEL), F32),
        compiler_params=_cparams("parallel"),
        name="final_norm",
    )(x2, g.reshape(1, D_MODEL))


def prompt_layer(x3, pe3, lw, tables):
    b, t, _ = x3.shape
    n = b * t
    (a, zb, qh, ql, kv4, win, cmpx, slc, winb, ng, mg) = inproj(
        x3.reshape(n, D_MODEL), lw['norm_mix'], lw['w_in'], tables, t, 512)
    r3 = lambda v: v.reshape(b, t, v.shape[-1])
    a3 = r3(a)
    zb3 = r3(zb)
    c = conv_a(a3, lw['conv_dw'], lw['conv_b'], lw['conv_ln_g'], lw['conv_ln_b'])
    pb = pool_b(zb3, lw['pool_w'], lw['pool_scale'])
    kcvc = compress_prompt(r3(cmpx), lw['cmp'])
    oc = nsa_prompt_t(r3(qh), r3(ql), r3(slc), r3(winb), kcvc, r3(ng))
    x1 = merge(c.reshape(n, 512), pb.reshape(n, 512), oc.reshape(n, 512), mg, x3.reshape(n, D_MODEL),
               lw['w_a_out'], lw['w_b_out'], lw['w_c_out'], lw['w_o'])
    x2, new_ffn = ffn_prompt(x1.reshape(b, t, D_MODEL), pe3, lw['norm_ffn'], lw['w_up'], lw['ffn_dw'], lw['w_down'],
                             lw['norm_ple'], lw['w_ple_gate'], lw['w_ple'])
    wp = min(WINDOW, t)
    new_kv = kv4.reshape(b, t, 4, N_KV_HEADS, HEAD_DIM)
    new_win = r3(win)[:, t - wp:].reshape(b, wp, 2, N_KV_HEADS, HEAD_DIM)
    new_conv = a3[:, t - (CONV_W - 1):]
    new_pool = zb3[:, t - (POOL_MAX - 1):]
    return x2, (new_conv, new_pool, new_ffn, new_kv, new_win)


def layer_weights(i, norm_mix, w_in, conv_dw, conv_b, conv_ln_g, conv_ln_b, w_a_out, pool_w, pool_scale, w_b_out,
                  pe_cmp, w_cmp1, w_cmp2, w_c_out, w_o, norm_ffn, w_up, ffn_dw, w_down, norm_ple, w_ple_gate, w_ple):
    return dict(norm_mix=norm_mix[i], w_in=_pack_w_in(w_in[i]), conv_dw=conv_dw[i], conv_b=conv_b[i],
                conv_ln_g=conv_ln_g[i], conv_ln_b=conv_ln_b[i], w_a_out=w_a_out[i].astype(BF16), pool_w=pool_w[i],
                pool_scale=pool_scale[i], w_b_out=w_b_out[i].astype(BF16),
                cmp=_compress_weights(pe_cmp[i], w_cmp1[i], w_cmp2[i]),
                cmp_s=_compress_weights_sample(pe_cmp[i], w_cmp1[i], w_cmp2[i]), w_c_out=w_c_out[i].astype(BF16),
                w_o=w_o[i].astype(BF16), norm_ffn=norm_ffn[i], w_up=w_up[i].astype(BF16), ffn_dw=ffn_dw[i],
                w_down=w_down[i].astype(BF16), norm_ple=norm_ple[i], w_ple_gate=w_ple_gate[i].astype(BF16),
                w_ple=w_ple[i].astype(BF16))


SB = 32


def _conva_s_kernel(st_ref, a_ref, w_ref, b_ref, g_ref, beta_ref, c_ref, *, t_new, bsz):
    hist = CONV_W - 1
    for t in range(t_new):
        for b0 in range(0, bsz, SB):
            acc = jnp.zeros((SB, CONV_WIDTH), F32)
            for k in range(CONV_W):
                j = t + k
                row = st_ref[j, b0:b0 + SB, :] if j < hist else a_ref[j - hist, b0:b0 + SB, :]
                acc = acc + row * w_ref[k:k + 1, :]
            y = acc + b_ref[...]
            mu = jnp.mean(y, axis=-1, keepdims=True)
            d = y - mu
            var = jnp.mean(d * d, axis=-1, keepdims=True)
            yn = d * lax.rsqrt(var + EPS) * g_ref[...] + beta_ref[...]
            c_ref[t, b0:b0 + SB, :] = (yn * jax.nn.sigmoid(yn)).astype(BF16)


def conv_a_sample(st_tm, a_tm, conv_dw, conv_b, ln_g, ln_b):
    t_new, bsz, c = a_tm.shape
    vec = lambda v: v.reshape(1, c)
    return pl.pallas_call(
        functools.partial(_conva_s_kernel, t_new=t_new, bsz=bsz),
        out_shape=jax.ShapeDtypeStruct((t_new, bsz, c), BF16),
        compiler_params=pltpu.CompilerParams(vmem_limit_bytes=VMEM_LIMIT),
        name="conv_a_sample",
    )(st_tm, a_tm, conv_dw, vec(conv_b), vec(ln_g), vec(ln_b))


def _pool_s_kernel(st_ref, z_ref, pw_ref, sc_ref, pb_ref, *, t_new, pos0):
    hist = POOL_MAX - 1
    for t in range(t_new):
        for gi, w in enumerate(POOL_WINDOWS):
            lanes = slice(gi * POOL_GC, (gi + 1) * POOL_GC)
            cur = z_ref[t, :, lanes]
            s = cur
            for k in range(1, w):
                j = hist + t - k
                s = s + (st_ref[j, :, lanes] if j < hist else z_ref[j - hist, :, lanes])
            cnt = float(min(pos0 + t + 1, w))
            d = s / cnt - cur
            y = jnp.dot(d.astype(BF16), pw_ref[gi], preferred_element_type=F32) * sc_ref[:, lanes]
            pb_ref[t, :, lanes] = y.astype(BF16)


def pool_b_sample(st_tm, z_tm, pool_w, pool_scale, pos0):
    t_new, bsz, c = z_tm.shape
    return pl.pallas_call(
        functools.partial(_pool_s_kernel, t_new=t_new, pos0=pos0),
        out_shape=jax.ShapeDtypeStruct((t_new, bsz, c), BF16),
        compiler_params=pltpu.CompilerParams(vmem_limit_bytes=VMEM_LIMIT),
        name="pool_b_sample",
    )(st_tm, z_tm, pool_w.astype(BF16), pool_scale.reshape(1, c))


def _ffn_s_kernel(x_ref, pe_ref, gf_ref, wv_ref, wg_ref, dwv_ref, dwg_ref, wdn_ref, stv_ref, stg_ref,
                  gp_ref, wgate_ref, wp_ref, o_ref, nfv_ref, nfg_ref, h_ref, acc_ref, *, t_new, bsz):
    c = pl.program_id(0)

    @pl.when(c == 0)
    def _():
        h_ref[...] = _rms(x_ref[...], gf_ref[...]).astype(BF16)
        acc_ref[...] = jnp.zeros(acc_ref.shape, F32)

    def conv_half(w_ref, dw_ref, st_ref, nf_ref):
        u = jnp.dot(h_ref[...], w_ref[...], preferred_element_type=F32)
        ext = [st_ref[0], st_ref[1]] + [u[t * bsz:(t + 1) * bsz] for t in range(t_new)]
        nf_ref[0] = ext[t_new]
        nf_ref[1] = ext[t_new + 1]
        return jnp.concatenate(
            [ext[t] * dw_ref[0:1, :] + ext[t + 1] * dw_ref[1:2, :] + ext[t + 2] * dw_ref[2:3, :] for t in range(t_new)],
            axis=0)

    cv = conv_half(wv_ref, dwv_ref, stv_ref, nfv_ref)
    cg = conv_half(wg_ref, dwg_ref, stg_ref, nfg_ref)
    f = _gelu(cv) * cg
    acc_ref[...] += jnp.dot(f.astype(BF16), wdn_ref[...], preferred_element_type=F32)

    @pl.when(c == pl.num_programs(0) - 1)
    def _():
        x2 = x_ref[...] + acc_ref[...]
        gate = jax.nn.sigmoid(jnp.dot(_rms(x2, gp_ref[...]).astype(BF16), wgate_ref[...], preferred_element_type=F32))
        o_ref[...] = x2 + gate * jnp.dot(pe_ref[...].astype(BF16), wp_ref[...], preferred_element_type=F32)


def ffn_sample(x2, pe2, st_tm, g_ffn, w_up, ffn_dw, w_down, g_ple, w_gate, w_ple, t_new):
    n = x2.shape[0]
    bsz = n // t_new
    nch = D_FF // FF_CHUNK
    full = lambda r, c: pl.BlockSpec((r, c), lambda i: (0, 0))
    return pl.pallas_call(
        functools.partial(_ffn_s_kernel, t_new=t_new, bsz=bsz),
        grid=(nch,),
        in_specs=[full(n, D_MODEL), full(n, PLE_DIM), full(1, D_MODEL),
                  pl.BlockSpec((D_MODEL, FF_CHUNK), lambda i: (0, i)),
                  pl.BlockSpec((D_MODEL, FF_CHUNK), lambda i: (0, nch + i)),
                  pl.BlockSpec((FFN_CONV_W, FF_CHUNK), lambda i: (0, i)),
                  pl.BlockSpec((FFN_CONV_W, FF_CHUNK), lambda i: (0, nch + i)),
                  pl.BlockSpec((FF_CHUNK, D_MODEL), lambda i: (i, 0)),
                  pl.BlockSpec((FFN_CONV_W - 1, bsz, FF_CHUNK), lambda i: (0, 0, i)),
                  pl.BlockSpec((FFN_CONV_W - 1, bsz, FF_CHUNK), lambda i: (0, 0, nch + i)),
                  full(1, D_MODEL), full(D_MODEL, D_MODEL), full(PLE_DIM, D_MODEL)],
        out_specs=[full(n, D_MODEL),
                   pl.BlockSpec((FFN_CONV_W - 1, bsz, FF_CHUNK), lambda i: (0, 0, i)),
                   pl.BlockSpec((FFN_CONV_W - 1, bsz, FF_CHUNK), lambda i: (0, 0, i))],
        out_shape=[jax.ShapeDtypeStruct((n, D_MODEL), F32),
                   jax.ShapeDtypeStruct((FFN_CONV_W - 1, bsz, D_FF), F32),
                   jax.ShapeDtypeStruct((FFN_CONV_W - 1, bsz, D_FF), F32)],
        scratch_shapes=[pltpu.VMEM((n, D_MODEL), BF16), pltpu.VMEM((n, D_MODEL), F32)],
        compiler_params=_cparams("arbitrary"),
        name="ffn_sample",
    )(x2, pe2, g_ffn.reshape(1, D_MODEL), w_up, w_up, ffn_dw, ffn_dw, w_down, st_tm, st_tm,
      g_ple.reshape(1, D_MODEL), w_gate, w_ple)


PAGE = 128
SEG_PER_PAGE = PAGE // SEG


def _compress_weights_sample(pe_c, w1, w2):
    eye = jnp.eye(2, dtype=F32)
    w1r = w1.reshape(2, CMP_BLOCK, HEAD_DIM, HEAD_DIM)
    def expand(w):
        return jnp.einsum('kjde,kl,gh->jkgdlhe', w, eye, eye).reshape(SEG, CMP_LANES, CMP_LANES).astype(BF16)
    w2bd = jnp.einsum('kde,kl,gh->kgdlhe', w2, eye, eye).reshape(CMP_LANES, CMP_LANES).astype(BF16)
    def pe_rows(p):
        return jnp.broadcast_to(p.transpose(1, 0, 2)[:, :, None, :], (SEG, 2, N_KV_HEADS, HEAD_DIM)).reshape(SEG, CMP_LANES)
    return expand(w1r[:, :SEG]), expand(w1r[:, SEG:]), w2bd, pe_rows(pe_c[:, :SEG]), pe_rows(pe_c[:, SEG:])


def _compress_s_kernel(pt_ref, *refs, n_pages, bt):
    del pt_ref
    np_all = bt * n_pages
    pages_k, pages_v = refs[:np_all], refs[np_all:2 * np_all]
    pea_ref, peb_ref, wa_ref, wb_ref, w2_ref, o_ref, bsc_ref = refs[2 * np_all:]
    m = bt * n_pages * SEG_PER_PAGE
    a = jnp.zeros((m, CMP_LANES), F32)
    bm = jnp.zeros((m, CMP_LANES), F32)
    for j in range(SEG):
        xj = jnp.concatenate(
            [jnp.concatenate([pg[pl.ds(j, SEG_PER_PAGE, stride=SEG), :] for pg in pgs], axis=0)
             for pgs in (pages_k, pages_v)], axis=1)
        a = a + jnp.dot((xj + pea_ref[j:j + 1, :]).astype(BF16), wa_ref[j], preferred_element_type=F32)
        bm = bm + jnp.dot((xj + peb_ref[j:j + 1, :]).astype(BF16), wb_ref[j], preferred_element_type=F32)
    bsc_ref[0:m, :] = bm
    bsc_ref[m:m + 8, :] = jnp.zeros((8, CMP_LANES), F32)
    u = a + bsc_ref[1:m + 1, :]
    out = jnp.dot(_gelu(u).astype(BF16), w2_ref[...], preferred_element_type=F32)
    nseg = n_pages * SEG_PER_PAGE
    for q in range(bt):
        o_ref[q] = out[q * nseg:(q + 1) * nseg]


def compress_sample(cache2, page_table, cw, bt=4):
    bsz, n_pages = page_table.shape
    nseg = n_pages * SEG_PER_PAGE
    wa, wb, w2bd, pea, peb = cw
    page_specs = [pl.BlockSpec((None, PAGE, KV_W),
                               functools.partial(lambda i, pt, q, p, kind: (pt[i * bt + q, p], 0, kind), q=q, p=p, kind=kind))
                  for kind in range(2) for q in range(bt) for p in range(n_pages)]
    const = lambda shape: pl.BlockSpec(shape, lambda i, pt: (0,) * len(shape))
    grid_spec = pltpu.PrefetchScalarGridSpec(
        num_scalar_prefetch=1, grid=(bsz // bt,),
        in_specs=page_specs + [const((SEG, CMP_LANES)), const((SEG, CMP_LANES)), const((SEG, CMP_LANES, CMP_LANES)),
                               const((SEG, CMP_LANES, CMP_LANES)), const((CMP_LANES, CMP_LANES))],
        out_specs=pl.BlockSpec((bt, nseg, CMP_LANES), lambda i, pt: (i, 0, 0)),
        scratch_shapes=[pltpu.VMEM((bt * nseg + 8, CMP_LANES), F32)])
    return pl.pallas_call(
        functools.partial(_compress_s_kernel, n_pages=n_pages, bt=bt),
        grid_spec=grid_spec,
        out_shape=jax.ShapeDtypeStruct((bsz, nseg, CMP_LANES), F32),
        compiler_params=_cparams("arbitrary"),
        name="compress_sample",
    )(page_table, *([cache2] * (2 * bt * n_pages)), pea, peb, wa, wb, w2bd)


NEW_PAD = 16


def _nsa_s_kernel(pt_ref, *refs, n_pages, t_new, win_len):
    del pt_ref
    pages = refs[:n_pages]
    (q3_ref, ngr_ref, kcvc_ref, nslc_ref, wst_ref, nwin_ref, oht_ref, o_ref,
     kaug_ref, vsel_ref, kw_ref, vw_ref, kc3_ref, qaug_ref) = refs[n_pages:]
    past = n_pages * PAGE
    kl = kaug_ref.shape[1]
    wl = kw_ref.shape[1]
    rows = GROUP * t_new
    nseg = kcvc_ref.shape[1]
    ncmp = (past + t_new - CMP_BLOCK) // CMP_STRIDE + 1
    nsel = -(-(past + t_new) // SEL_BLOCK)

    @pl.when(pl.program_id(0) == 0)
    def _():
        blk = lax.broadcasted_iota(jnp.int32, (kl, LANES), 0) // SEL_BLOCK
        lane = lax.broadcasted_iota(jnp.int32, (kl, LANES), 1)
        onehot = jnp.where(blk == lane, 1.0, 0.0).astype(BF16)
        for g in range(N_KV_HEADS):
            kaug_ref[g, :, 0:LANES] = onehot
            kaug_ref[g, past:kl, LANES:QA] = jnp.zeros((kl - past, HEAD_DIM), BF16)
            vsel_ref[g, past:kl, :] = jnp.zeros((kl - past, HEAD_DIM), BF16)
            kw_ref[g, win_len:wl, :] = jnp.zeros((wl - win_len, HEAD_DIM), BF16)
            vw_ref[g, win_len:wl, :] = jnp.zeros((wl - win_len, HEAD_DIM), BF16)

    pos4 = past + lax.broadcasted_iota(jnp.int32, (rows, 1), 0) % t_new
    kcvc = kcvc_ref[0]
    nslc = nslc_ref[0]
    nwin = nwin_ref[0]
    wst = wst_ref[0]
    for g in range(N_KV_HEADS):
        ks = slice(g * HEAD_DIM, (g + 1) * HEAD_DIM)
        vs = slice(KV_W + g * HEAD_DIM, KV_W + (g + 1) * HEAD_DIM)
        for p in range(n_pages):
            kaug_ref[g, p * PAGE:(p + 1) * PAGE, LANES:QA] = pages[p][:, ks].astype(BF16)
            vsel_ref[g, p * PAGE:(p + 1) * PAGE, :] = pages[p][:, vs].astype(BF16)
        kaug_ref[g, past:past + NEW_PAD, LANES:QA] = nslc[:, ks]
        vsel_ref[g, past:past + NEW_PAD, :] = nslc[:, vs]
        kw_ref[g, 0:win_len, :] = wst[:, ks].astype(BF16)
        vw_ref[g, 0:win_len, :] = wst[:, vs].astype(BF16)
        kw_ref[g, win_len:win_len + NEW_PAD, :] = nwin[:, ks]
        vw_ref[g, win_len:win_len + NEW_PAD, :] = nwin[:, vs]
        kc = kcvc[:, ks]
        kch = kc.astype(BF16)
        kc3_ref[:, 0:HEAD_DIM] = kch
        kc3_ref[:, HEAD_DIM:2 * HEAD_DIM] = kch
        kc3_ref[:, 2 * HEAD_DIM:Q3] = (kc - kch.astype(F32)).astype(BF16)
        vc = kcvc[:, vs].astype(BF16)
        q3 = q3_ref[0, g]

        s = lax.dot_general(q3, kc3_ref[...], NT, preferred_element_type=F32)
        n_idx = lax.broadcasted_iota(jnp.int32, (1, nseg), 1)
        end = n_idx * CMP_STRIDE + (CMP_BLOCK - 1)
        mask = jnp.where(n_idx < ncmp, end, past + t_new + CMP_BLOCK) <= pos4
        p = _softmax_rows(s, mask)
        o_c = jnp.dot(p.astype(BF16), vc, preferred_element_type=F32)
        ppad = jnp.concatenate([p, jnp.zeros((LANES - rows, nseg), F32)], axis=0)
        p_hi = ppad.astype(BF16)
        p_lo = (ppad - p_hi.astype(F32)).astype(BF16)
        x = (lax.dot_general(oht_ref[...], p_hi, NT, preferred_element_type=F32)
             + lax.dot_general(oht_ref[...], p_lo, NT, preferred_element_type=F32))
        v = x
        for r in range(1, GROUP):
            v = v + pltpu.roll(x, LANES - r * t_new, 1)

        jidx = lax.broadcasted_iota(jnp.int32, (LANES, LANES), 0)
        qblk = (past + lax.broadcasted_iota(jnp.int32, (1, LANES), 1) % t_new) // SEL_BLOCK
        forced = (jidx == 0) | (jidx == qblk) | (jidx == qblk - 1)
        v = jnp.where(jidx <= qblk, jnp.where(forced, FORCE_SCORE, v), -1.0)
        v = jnp.where(jidx < nsel, v, -3.0)
        sel = jnp.zeros((LANES, LANES), F32)
        for _ in range(min(N_SELECT, nsel)):
            mx = jnp.max(v, axis=0, keepdims=True)
            first = jnp.min(jnp.where(v == mx, jidx, LANES), axis=0, keepdims=True)
            pick = jidx == first
            sel = jnp.where(pick, 1.0, sel)
            v = jnp.where(pick, -2.0, v)
        ri = lax.broadcasted_iota(jnp.int32, (rows, LANES), 0) % t_new
        li = lax.broadcasted_iota(jnp.int32, (rows, LANES), 1)
        spread = jnp.where(ri == li, 1.0, 0.0).astype(BF16)
        sel_rows = lax.dot_general(spread, sel.astype(BF16), NT, preferred_element_type=F32)
        qaug_ref[:, 0:LANES] = jnp.where(sel_rows > 0.5, 0.0, NEG).astype(BF16)
        qaug_ref[:, LANES:QA] = q3[:, 0:HEAD_DIM]

        sc = lax.dot_general(qaug_ref[...], kaug_ref[g], NT, preferred_element_type=F32)
        kpos = lax.broadcasted_iota(jnp.int32, (1, kl), 1)
        ps = _softmax_rows(sc, kpos <= pos4)
        o_s = jnp.dot(ps.astype(BF16), vsel_ref[g], preferred_element_type=F32)

        sw = lax.dot_general(q3[:, 0:HEAD_DIM], kw_ref[g], NT, preferred_element_type=F32)
        widx = lax.broadcasted_iota(jnp.int32, (1, wl), 1)
        pos_w = past - win_len + widx
        mw = (widx < win_len + t_new) & (pos_w <= pos4) & (pos_w > pos4 - WINDOW) & (pos_w >= 0)
        pw = _softmax_rows(sw, mw)
        o_w = jnp.dot(pw.astype(BF16), vw_ref[g], preferred_element_type=F32)

        ngr = ngr_ref[0, g]
        o_ref[0, g] = ngr[:, 0:1] * o_c + ngr[:, 1:2] * o_s + ngr[:, 2:3] * o_w


def nsa_sample_attn(cache2, page_table, q3, ngr, kcvc, nslc, wst, nwin):
    bsz, n_pages = page_table.shape
    t_new = q3.shape[2] // GROUP
    rows = GROUP * t_new
    past = n_pages * PAGE
    win_len = wst.shape[1]
    nseg = kcvc.shape[1]
    kl = past + LANES
    wl = win_len + LANES
    owner = (jnp.arange(nseg) * CMP_STRIDE) // SEL_BLOCK
    oht = (jnp.arange(LANES)[:, None] == owner[None, :]).astype(BF16)
    page_specs = [pl.BlockSpec((None, PAGE, 2 * KV_W), functools.partial(lambda i, pt, p: (pt[i, p], 0, 1), p=p))
                  for p in range(n_pages)]
    per_b = lambda *shape: pl.BlockSpec((1,) + shape, lambda i, pt: (i,) + (0,) * len(shape))
    grid_spec = pltpu.PrefetchScalarGridSpec(
        num_scalar_prefetch=1, grid=(bsz,),
        in_specs=page_specs + [per_b(N_KV_HEADS, rows, Q3), per_b(N_KV_HEADS, rows, LANES), per_b(nseg, CMP_LANES),
                               per_b(NEW_PAD, 2 * KV_W), per_b(win_len, 2 * KV_W), per_b(NEW_PAD, 2 * KV_W),
                               pl.BlockSpec((LANES, nseg), lambda i, pt: (0, 0))],
        out_specs=per_b(N_KV_HEADS, rows, HEAD_DIM),
        scratch_shapes=[pltpu.VMEM((N_KV_HEADS, kl, QA), BF16), pltpu.VMEM((N_KV_HEADS, kl, HEAD_DIM), BF16),
                        pltpu.VMEM((N_KV_HEADS, wl, HEAD_DIM), BF16), pltpu.VMEM((N_KV_HEADS, wl, HEAD_DIM), BF16),
                        pltpu.VMEM((nseg, Q3), BF16), pltpu.VMEM((rows, QA), BF16)])
    return pl.pallas_call(
        functools.partial(_nsa_s_kernel, n_pages=n_pages, t_new=t_new, win_len=win_len),
        grid_spec=grid_spec,
        out_shape=jax.ShapeDtypeStruct((bsz, N_KV_HEADS, rows, HEAD_DIM), F32),
        compiler_params=_cparams("arbitrary"),
        name="nsa_sample",
    )(page_table, *([cache2] * n_pages), q3, ngr, kcvc, nslc, wst, nwin, oht)


def sample_layer(x2, pe2, lw, tables, cache2, page_table, st_win, st_conv, st_pool, st_ffn, t_new):
    n = x2.shape[0]
    bsz = n // t_new
    past = page_table.shape[1] * PAGE
    (a, zb, qh, ql, kv4, win, _, slc, winb, ng, mg) = inproj(x2, lw['norm_mix'], lw['w_in'], tables, n, 256)
    tm3 = lambda v: v.reshape(t_new, bsz, v.shape[-1])
    bm3 = lambda v: tm3(v).transpose(1, 0, 2)
    c = conv_a_sample(st_conv.transpose(1, 0, 2), tm3(a), lw['conv_dw'], lw['conv_b'], lw['conv_ln_g'], lw['conv_ln_b'])
    pb = pool_b_sample(st_pool.transpose(1, 0, 2), tm3(zb), lw['pool_w'], lw['pool_scale'], past)
    kcvc = compress_sample(cache2, page_table, lw['cmp_s'])

    def heads(v):
        return v.reshape(t_new, bsz, N_KV_HEADS, GROUP, HEAD_DIM).transpose(1, 2, 3, 0, 4).reshape(
            bsz, N_KV_HEADS, GROUP * t_new, HEAD_DIM)
    qh4, ql4 = heads(qh), heads(ql)
    q3 = jnp.concatenate([qh4, ql4, qh4], axis=-1)
    ngr = ng[:, :3 * N_HEADS].reshape(t_new, bsz, N_KV_HEADS, GROUP, 3).transpose(1, 2, 3, 0, 4).reshape(
        bsz, N_KV_HEADS, GROUP * t_new, 3)
    ngr = jnp.pad(ngr, ((0, 0), (0, 0), (0, 0), (0, LANES - 3)))
    pad_new = lambda v: jnp.pad(bm3(v), ((0, 0), (0, NEW_PAD - t_new), (0, 0)))
    wst = st_win.reshape(bsz, st_win.shape[1], 2 * KV_W)
    oc = nsa_sample_attn(cache2, page_table, q3, ngr, kcvc, pad_new(slc), wst, pad_new(winb))
    oc2 = oc.reshape(bsz, N_KV_HEADS, GROUP, t_new, HEAD_DIM).transpose(3, 0, 1, 2, 4).reshape(n, 512).astype(BF16)
    x1 = merge(c.reshape(n, 512), pb.reshape(n, 512), oc2, mg, x2,
               lw['w_a_out'], lw['w_b_out'], lw['w_c_out'], lw['w_o'], tm=256)
    x3, nfv, nfg = ffn_sample(x1, pe2, st_ffn.transpose(1, 0, 2), lw['norm_ffn'], lw['w_up'], lw['ffn_dw'], lw['w_down'],
                              lw['norm_ple'], lw['w_ple_gate'], lw['w_ple'], t_new)
    new_kv = bm3(kv4).reshape(bsz, t_new, 4, N_KV_HEADS, HEAD_DIM)
    new_win = jnp.concatenate([wst[:, t_new:], bm3(win)], axis=1).reshape(bsz, -1, 2, N_KV_HEADS, HEAD_DIM)
    new_conv = jnp.concatenate([st_conv[:, t_new:], bm3(a)], axis=1)
    new_pool = jnp.concatenate([st_pool[:, t_new:], bm3(zb)], axis=1)
    new_ffn = jnp.concatenate([nfv, nfg], axis=-1).transpose(1, 0, 2)
    return x3, (new_conv, new_pool, new_ffn, new_kv, new_win)


def rmsnorm(x, g):
    xf = x.astype(F32)
    y = xf * lax.rsqrt(jnp.mean(xf * xf, axis=-1, keepdims=True) + EPS)
    return (y * g.astype(F32)).astype(x.dtype)


def layernorm(x, g, b):
    xf = x.astype(F32)
    mu = jnp.mean(xf, axis=-1, keepdims=True)
    var = jnp.mean(jnp.square(xf - mu), axis=-1, keepdims=True)
    y = (xf - mu) * lax.rsqrt(var + EPS) * g.astype(F32) + b.astype(F32)
    return y.astype(x.dtype)


def rotary(x, pos):
    half = ROT_DIM // 2
    inv = ROPE_THETA ** (-jnp.arange(half, dtype=F32) * 2.0 / ROT_DIM)
    ang = pos.astype(F32)[:, None] * inv[None, :]
    ang = ang.reshape((ang.shape[0],) + (1,) * (x.ndim - 3) + (half,))
    cos, sin = jnp.cos(ang), jnp.sin(ang)
    x1 = x[..., :half].astype(F32)
    x2 = x[..., half:ROT_DIM].astype(F32)
    rot = jnp.concatenate([x1 * cos - x2 * sin, x2 * cos + x1 * sin], axis=-1).astype(x.dtype)
    return jnp.concatenate([rot, x[..., ROT_DIM:]], axis=-1)


def masked_softmax(s, mask):
    s = jnp.where(mask, s.astype(F32), NEG)
    return jnp.where(mask, jax.nn.softmax(s, axis=-1), 0.0)


def causal_dwconv(u, prev, w):
    K, C = w.shape
    ext = jnp.concatenate([prev, u], axis=1)
    y = lax.conv_general_dilated(ext, w[:, None, :], window_strides=(1,), padding='VALID',
                                 dimension_numbers=('NWC', 'WIO', 'NWC'), feature_group_count=C)
    return y, ext[:, -(K - 1):]


def pool_mix(u, prev, pos0, w_grp, scale):
    B, T, C = u.shape
    P1 = POOL_MAX - 1
    ext = jnp.concatenate([prev, u], axis=1)
    cs = jnp.concatenate([jnp.zeros((B, 1, C), F32), jnp.cumsum(ext.astype(F32), axis=1)], axis=1)
    upto = cs[:, P1 + 1:P1 + 1 + T]
    pos = (pos0 + jnp.arange(T)).astype(F32)
    means = []
    for g, w in enumerate(POOL_WINDOWS):
        ch = slice(g * POOL_GC, (g + 1) * POOL_GC)
        start = cs[:, P1 + 1 - w:P1 + 1 - w + T, ch]
        cnt = jnp.minimum(pos + 1.0, float(w))[None, :, None]
        means.append((upto[..., ch] - start) / cnt)
    pooled = jnp.concatenate(means, axis=-1)
    d = (pooled - u.astype(F32)).astype(u.dtype).reshape(B, T, len(POOL_WINDOWS), POOL_GC)
    y = jnp.concatenate([mm(d[:, :, g], w_grp[g]) for g in range(len(POOL_WINDOWS))], axis=-1) * scale
    return y, ext[:, -P1:]


def compress(k, pe, w1, w2):
    B, L, G, D = k.shape
    nc = (L - CMP_BLOCK) // CMP_STRIDE + 1
    idx = jnp.arange(nc)[:, None] * CMP_STRIDE + jnp.arange(CMP_BLOCK)[None, :]
    blk = k[:, idx] + pe[None, None, :, None, :]
    flat = blk.transpose(0, 1, 3, 2, 4).reshape(B, nc, G, CMP_BLOCK * D)
    out = mm(jax.nn.gelu(mm(flat, w1)), w2)
    end = jnp.arange(nc) * CMP_STRIDE + CMP_BLOCK - 1
    return out, end


def to_blocks(k):
    B, L, G, D = k.shape
    ns = -(-L // SEL_BLOCK)
    k = jnp.pad(k, ((0, 0), (0, ns * SEL_BLOCK - L), (0, 0), (0, 0)))
    return k.reshape(B, ns, SEL_BLOCK, G, D).transpose(0, 3, 1, 2, 4)


def nsa_attend(q, ng, pos_q, kc, vc, cmp_end, ks, vs, kw, vw, pos_w):
    B, Tq, G, R, D = q.shape
    scale = D ** -0.5
    s = jnp.einsum('bqgrd,bngd->bgrqn', q, kc).astype(F32) * scale
    p_c = masked_softmax(s, cmp_end[None, :] <= pos_q[:, None])
    o_c = jnp.einsum('bgrqn,bngd->bqgrd', p_c.astype(vc.dtype), vc)
    nc, ns = kc.shape[1], ks.shape[2]
    owner = (jnp.arange(nc) * CMP_STRIDE) // SEL_BLOCK
    onehot = (owner[:, None] == jnp.arange(ns)[None, :]).astype(F32)
    imp = jnp.einsum('bgrqn,nj->bgqj', p_c, onehot)
    q_blk = pos_q // SEL_BLOCK
    j = jnp.arange(ns)[None, :]
    forced = (j == 0) | (j == q_blk[:, None]) | (j == q_blk[:, None] - 1)
    allowed = j <= q_blk[:, None]
    imp = jnp.where(allowed, jnp.where(forced, FORCE_SCORE, imp), -1.0)
    n_sel = min(N_SELECT, ns)
    vals, idx = lax.top_k(imp, n_sel)
    bi = jnp.arange(B)[:, None, None, None]
    gi = jnp.arange(G)[None, :, None, None]
    kg = ks[bi, gi, idx]
    vg = vs[bi, gi, idx]
    kpos = idx[..., None] * SEL_BLOCK + jnp.arange(SEL_BLOCK)
    m_s = (vals[..., None] >= 0.0) & (kpos <= pos_q[None, None, :, None, None])
    m_s = m_s.reshape(B, G, 1, Tq, n_sel * SEL_BLOCK)
    s = jnp.einsum('bqgrd,bgqkld->bgrqkl', q, kg).reshape(B, G, R, Tq, n_sel * SEL_BLOCK) * scale
    p_s = masked_softmax(s, m_s)
    o_s = jnp.einsum('bgrqm,bgqmd->bqgrd', p_s.astype(vg.dtype), vg.reshape(B, G, Tq, n_sel * SEL_BLOCK, D))
    m_w = (pos_w[None, :] <= pos_q[:, None]) & (pos_w[None, :] > pos_q[:, None] - WINDOW) & (pos_w[None, :] >= 0)
    s = jnp.einsum('bqgrd,blgd->bgrql', q, kw).astype(F32) * scale
    p_w = masked_softmax(s, m_w)
    o_w = jnp.einsum('bgrql,blgd->bqgrd', p_w.astype(vw.dtype), vw)
    return ng[..., 0, None] * o_c + ng[..., 1, None] * o_s + ng[..., 2, None] * o_w


def nsa_sample(q, kv, ng, pe_c, w1, w2, cache_kv, page_table, win_state):
    B, T = q.shape[:2]
    past_len = page_table.shape[1] * cache_kv.shape[1]
    past = cache_kv[page_table].reshape(B, past_len, 4, N_KV_HEADS, HEAD_DIM)
    full = jnp.concatenate([past, kv[:, :, 0:4]], axis=1)
    kc, end = compress(full[:, :, 0], pe_c[0], w1[0], w2[0])
    vc, _ = compress(full[:, :, 1], pe_c[1], w1[1], w2[1])
    ks, vs = to_blocks(full[:, :, 2]), to_blocks(full[:, :, 3])
    wlen = win_state.shape[1]
    win = jnp.concatenate([win_state, kv[:, :, 4:6]], axis=1)
    pos_w = past_len - wlen + jnp.arange(wlen + T)
    pos_q = past_len + jnp.arange(T)
    o = nsa_attend(q, ng, pos_q, kc, vc, end, ks, vs, win[:, :, 0], win[:, :, 1], pos_w)
    return o.reshape(B, T, N_HEADS * HEAD_DIM), kv[:, :, 0:4], win[:, T:]


def trunk_layer(x, pe_i, prev_conv, prev_pool, prev_ffn, pos0, nsa_fn, prm):
    B, T, _ = x.shape
    pos = pos0 + jnp.arange(T)
    h = rmsnorm(x, prm['norm_mix'])
    z = mm(h, prm['w_in'])
    z_a, z_b, z_q, z_kv, z_ng, z_mg = jnp.split(z, SPLIT_AT, axis=-1)
    a = z_a[..., :CONV_WIDTH] * jax.nn.sigmoid(z_a[..., CONV_WIDTH:])
    c, new_conv = causal_dwconv(a, prev_conv, prm['conv_dw'])
    c = jax.nn.silu(layernorm(c + prm['conv_b'], prm['conv_ln_g'], prm['conv_ln_b']))
    out_a = mm(c, prm['w_a_out'])
    pb, new_pool = pool_mix(z_b, prev_pool, pos0, prm['pool_w'], prm['pool_scale'])
    out_b = mm(pb, prm['w_b_out'])
    q = rotary(z_q.reshape(B, T, N_HEADS, HEAD_DIM), pos).reshape(B, T, N_KV_HEADS, GROUP, HEAD_DIM)
    kv = z_kv.reshape(B, T, 6, N_KV_HEADS, HEAD_DIM)
    keys = rotary(kv[:, :, 0::2], pos)
    kv = jnp.stack([keys, kv[:, :, 1::2]], axis=3).reshape(B, T, 6, N_KV_HEADS, HEAD_DIM)
    ng = jax.nn.sigmoid(z_ng.reshape(B, T, N_KV_HEADS, GROUP, 3))
    o_c, new_kv, new_win = nsa_fn(q, kv, ng)
    out_c = mm(o_c, prm['w_c_out'])
    mg = jax.nn.sigmoid(z_mg.reshape(B, T, 3, D_MODEL))
    m = mg[:, :, 0] * out_a + mg[:, :, 1] * out_b + mg[:, :, 2] * out_c
    x = x + mm(m, prm['w_o'])
    u = mm(rmsnorm(x, prm['norm_ffn']), prm['w_up'])
    cu, new_ffn = causal_dwconv(u, prev_ffn, prm['ffn_dw'])
    f = jax.nn.gelu(cu[..., :D_FF]) * cu[..., D_FF:]
    x = x + mm(f, prm['w_down'])
    gate = jax.nn.sigmoid(mm(rmsnorm(x, prm['norm_ple']), prm['w_ple_gate']))
    x = x + gate * mm(pe_i, prm['w_ple'])
    return x, (new_conv, new_pool, new_ffn, new_kv, new_win)


def kernel(x_prompt, x_sample, cache_nsa_kv, state_nsa_win, state_conv, state_pool, state_ffn, page_table,
           p_prompt, p_sample, norm_mix, w_in, conv_dw, conv_b, conv_ln_g, conv_ln_b, w_a_out, pool_w,
           pool_scale, w_b_out, pe_cmp, w_cmp1, w_cmp2, w_c_out, w_o, norm_ffn, w_up, ffn_dw, w_down,
           norm_ple, w_ple_gate, w_ple, norm_final):
    B, T, _ = x_prompt.shape
    bs, ts, _ = x_sample.shape
    n_pool, page = cache_nsa_kv.shape[1:3]
    assert page == PAGE and T % 512 == 0 and T // SEL_BLOCK <= LANES and (bs * ts) % 256 == 0 and bs % SB == 0
    past_len = page_table.shape[1] * page
    tables = _rope_tables(jnp.arange(T))
    tables_s = _rope_tables(past_len + jnp.repeat(jnp.arange(ts), bs))
    xp = x_prompt
    xs = x_sample.transpose(1, 0, 2).reshape(ts * bs, D_MODEL)
    st_p, st_s = [], []
    for i in range(DEPTH):
        lw = layer_weights(i, norm_mix, w_in, conv_dw, conv_b, conv_ln_g, conv_ln_b, w_a_out, pool_w, pool_scale,
                           w_b_out, pe_cmp, w_cmp1, w_cmp2, w_c_out, w_o, norm_ffn, w_up, ffn_dw, w_down, norm_ple,
                           w_ple_gate, w_ple)
        xp, st = prompt_layer(xp, p_prompt[i], lw, tables)
        st_p.append(st)
        xs, st = sample_layer(xs, p_sample[i].transpose(1, 0, 2).reshape(ts * bs, PLE_DIM), lw, tables_s,
                              cache_nsa_kv[i].reshape(n_pool, page, 4 * KV_W), page_table, state_nsa_win[i],
                              state_conv[i], state_pool[i], state_ffn[i], ts)
        st_s.append(st)
    y_prompt = final_norm(xp.reshape(B * T, D_MODEL), norm_final).reshape(B, T, D_MODEL)
    y_sample = final_norm(xs, norm_final, tm=256).reshape(ts, bs, D_MODEL).transpose(1, 0, 2)
    stk = lambda lst, k: jnp.stack([s[k] for s in lst])
    return (y_prompt, y_sample, stk(st_p, 3), stk(st_p, 4), stk(st_p, 0), stk(st_p, 1), stk(st_p, 2),
            stk(st_s, 3), stk(st_s, 4), stk(st_s, 0), stk(st_s, 1), stk(st_s, 2))
```

```python
import functools
import numpy as np
import jax
import jax.numpy as jnp
from jax import lax
from jax.experimental import pallas as pl
from jax.experimental.pallas import tpu as pltpu

D_MODEL = 1024
DEPTH = 4
CONV_WIDTH = D_MODEL // 2
POOL_WIDTH = D_MODEL // 2
HEAD_DIM = 64
N_HEADS = (D_MODEL // 2) // HEAD_DIM
N_KV_HEADS = 2
GROUP = N_HEADS // N_KV_HEADS
ROT_DIM = HEAD_DIM // 4
ROPE_THETA = 500000.0
CONV_W = 31
POOL_WINDOWS = (2, 4, 8, 16)
POOL_MAX = max(POOL_WINDOWS)
POOL_GC = POOL_WIDTH // len(POOL_WINDOWS)
CMP_BLOCK = 32
CMP_STRIDE = 16
SEL_BLOCK = 64
N_SELECT = 16
WINDOW = 512
Q_BLOCK = 128
D_FF = ((8 * D_MODEL // 3 + 127) // 128) * 128
FFN_CONV_W = 3
PLE_DIM = 256
EPS = 1e-6
FORCE_SCORE = 1e4
NEG = -1e30

SPLIT_SIZES = (2 * CONV_WIDTH, POOL_WIDTH, N_HEADS * HEAD_DIM, 6 * N_KV_HEADS * HEAD_DIM, 3 * N_HEADS, 3 * D_MODEL)
SPLIT_AT = tuple(int(v) for v in np.cumsum(SPLIT_SIZES)[:-1])

F32 = jnp.float32
BF16 = jnp.bfloat16
LANES = 128
VMEM_LIMIT = 56 * 1024 * 1024
NT = (((1,), (1,)), ((), ()))

NG_PAD = LANES
C_A, C_B, C_Q, C_KV = 0, 2 * CONV_WIDTH, 2 * CONV_WIDTH + POOL_WIDTH, 2 * CONV_WIDTH + POOL_WIDTH + 512
C_NG = C_KV + 768
C_MG = C_NG + NG_PAD
IN_PACKED = C_MG + 3 * D_MODEL
KV_W = N_KV_HEADS * HEAD_DIM


def _cparams(*sem):
    return pltpu.CompilerParams(dimension_semantics=sem, vmem_limit_bytes=VMEM_LIMIT)


def _const_spec(shape):
    nd = len(shape)
    return pl.BlockSpec(shape, lambda *_: (0,) * nd, pipeline_mode=pl.Buffered(1))


def _rms(x, g):
    return x * lax.rsqrt(jnp.mean(x * x, axis=-1, keepdims=True) + EPS) * g


def _gelu(x):
    return 0.5 * x * (1.0 + jnp.tanh(np.sqrt(2.0 / np.pi).astype(np.float32) * (x + 0.044715 * (x * x * x))))


def _mm_kernel(x_ref, w_ref, o_ref):
    o_ref[...] = jnp.dot(x_ref[...].astype(BF16), w_ref[...], preferred_element_type=F32)


def mm(x, w):
    lead = x.shape[:-1]
    k = x.shape[-1]
    m = w.shape[-1]
    x2 = x.reshape(-1, k)
    n = x2.shape[0]
    col_align = LANES if m <= 1024 else 512
    mp = -(-m // col_align) * col_align
    wb = w.astype(BF16)
    if mp != m:
        wb = jnp.pad(wb, ((0, 0), (0, mp - m)))
    tm = 512 if n % 512 == 0 else (256 if n % 256 == 0 else n)
    npad = n
    if n % 8 != 0:
        npad = -(-n // 8) * 8
        x2 = jnp.pad(x2, ((0, npad - n), (0, 0)))
        tm = npad
    tn = mp
    for cand in (1024, 768, 512, 384, 256, 128):
        if mp % cand == 0:
            tn = cand
            break
    out = pl.pallas_call(
        _mm_kernel,
        grid=(npad // tm, mp // tn),
        in_specs=[pl.BlockSpec((tm, k), lambda i, j: (i, 0)),
                  pl.BlockSpec((k, tn), lambda i, j: (0, j))],
        out_specs=pl.BlockSpec((tm, tn), lambda i, j: (i, j)),
        out_shape=jax.ShapeDtypeStruct((npad, mp), F32),
        compiler_params=_cparams("parallel", "parallel"),
    )(x2, wb)
    return out[:n, :m].reshape(lead + (m,))


def _rope_tables(pos):
    half = ROT_DIM // 2
    inv = ROPE_THETA ** (-jnp.arange(half, dtype=F32) * 2.0 / ROT_DIM)
    ang = pos.astype(F32)[:, None] * inv[None, :]
    cos, sin = jnp.cos(ang), jnp.sin(ang)
    rows = pos.shape[0]
    ones = jnp.ones((rows, HEAD_DIM - ROT_DIM), F32)
    zeros = jnp.zeros((rows, HEAD_DIM - ROT_DIM), F32)
    z8 = jnp.zeros((rows, half), F32)
    cos_h = jnp.concatenate([cos, cos, ones], axis=1)
    sup_h = jnp.concatenate([-sin, z8, zeros], axis=1)
    sdn_h = jnp.concatenate([z8, sin, zeros], axis=1)
    rep = LANES // HEAD_DIM
    return jnp.tile(cos_h, (1, rep)), jnp.tile(sup_h, (1, rep)), jnp.tile(sdn_h, (1, rep))


def _rope128(x, cos, sup, sdn):
    return x * cos + pltpu.roll(x, LANES - ROT_DIM // 2, 1) * sup + pltpu.roll(x, ROT_DIM // 2, 1) * sdn


def _inproj_kernel(x_ref, g_ref, w_ref, cos_ref, sup_ref, sdn_ref,
                   a_ref, zb_ref, qh_ref, ql_ref, kv4_ref, win_ref, cmp_ref, slc_ref, winb_ref, ng_ref, mg_ref):
    h = _rms(x_ref[...], g_ref[...]).astype(BF16)

    def proj(c0, n):
        return jnp.dot(h, w_ref[:, c0:c0 + n], preferred_element_type=F32)

    za = proj(C_A, 2 * CONV_WIDTH)
    a_ref[...] = za[:, :CONV_WIDTH] * jax.nn.sigmoid(za[:, CONV_WIDTH:])
    zb_ref[...] = proj(C_B, POOL_WIDTH)
    cos, sup, sdn = cos_ref[...], sup_ref[...], sdn_ref[...]
    zq = proj(C_Q, N_HEADS * HEAD_DIM)
    scale = HEAD_DIM ** -0.5
    for c in range(N_HEADS * HEAD_DIM // LANES):
        sl = slice(c * LANES, (c + 1) * LANES)
        qr = _rope128(zq[:, sl], cos, sup, sdn) * scale
        qh = qr.astype(BF16)
        qh_ref[:, sl] = qh
        ql_ref[:, sl] = (qr - qh.astype(F32)).astype(BF16)
    zkv = proj(C_KV, 6 * KV_W)
    kind = [zkv[:, j * KV_W:(j + 1) * KV_W] for j in range(6)]
    for j in (0, 2, 4):
        kind[j] = _rope128(kind[j], cos, sup, sdn)
    for j in range(4):
        kv4_ref[:, j * KV_W:(j + 1) * KV_W] = kind[j]
    cmp_ref[:, 0:KV_W] = kind[0]
    cmp_ref[:, KV_W:2 * KV_W] = kind[1]
    slc_ref[:, 0:KV_W] = kind[2].astype(BF16)
    slc_ref[:, KV_W:2 * KV_W] = kind[3].astype(BF16)
    win_ref[:, 0:KV_W] = kind[4]
    win_ref[:, KV_W:2 * KV_W] = kind[5]
    winb_ref[:, 0:KV_W] = kind[4].astype(BF16)
    winb_ref[:, KV_W:2 * KV_W] = kind[5].astype(BF16)
    ng_ref[...] = jax.nn.sigmoid(proj(C_NG, NG_PAD))
    mg_ref[...] = jax.nn.sigmoid(proj(C_MG, 3 * D_MODEL))


def _pack_w_in(w_in):
    parts = jnp.split(w_in, SPLIT_AT, axis=-1)
    ng = jnp.pad(parts[4], ((0, 0), (0, NG_PAD - parts[4].shape[1])))
    return jnp.concatenate([parts[0], parts[1], parts[2], parts[3], ng, parts[5]], axis=1).astype(BF16)


def inproj(x2, g, w_packed, tables, period, tm):
    n = x2.shape[0]
    pblocks = period // tm
    row = lambda c: pl.BlockSpec((tm, c), lambda i: (i, 0))
    tab = pl.BlockSpec((tm, LANES), lambda i: (i % pblocks, 0))
    outs = [(CONV_WIDTH, F32), (POOL_WIDTH, F32), (512, BF16), (512, BF16), (4 * KV_W, F32), (2 * KV_W, F32),
            (2 * KV_W, F32), (2 * KV_W, BF16), (2 * KV_W, BF16), (NG_PAD, F32), (3 * D_MODEL, F32)]
    return pl.pallas_call(
        _inproj_kernel,
        grid=(n // tm,),
        in_specs=[row(D_MODEL), _const_spec((1, D_MODEL)), _const_spec((D_MODEL, IN_PACKED)), tab, tab, tab],
        out_specs=[row(c) for c, _ in outs],
        out_shape=[jax.ShapeDtypeStruct((n, c), dt) for c, dt in outs],
        compiler_params=_cparams("parallel"),
        name="inproj",
    )(x2, g.reshape(1, D_MODEL), w_packed, *tables)


CONV_HALO = 32


def _conva_kernel(a_ref, w_ref, b_ref, g_ref, beta_ref, c_ref, ext_ref, *, tm, rc):
    @pl.when(pl.program_id(1) == 0)
    def _():
        ext_ref[0:CONV_HALO, :] = jnp.zeros((CONV_HALO, CONV_WIDTH), F32)

    ext_ref[CONV_HALO:CONV_HALO + tm, :] = a_ref[0]
    off = CONV_HALO - (CONV_W - 1)
    for r0 in range(0, tm, rc):
        acc = jnp.zeros((rc, CONV_WIDTH), F32)
        for k in range(CONV_W):
            acc = acc + ext_ref[r0 + off + k:r0 + off + k + rc, :] * w_ref[k:k + 1, :]
        y = acc + b_ref[...]
        mu = jnp.mean(y, axis=-1, keepdims=True)
        d = y - mu
        var = jnp.mean(d * d, axis=-1, keepdims=True)
        yn = d * lax.rsqrt(var + EPS) * g_ref[...] + beta_ref[...]
        c_ref[0, r0:r0 + rc, :] = (yn * jax.nn.sigmoid(yn)).astype(BF16)
    ext_ref[0:CONV_HALO, :] = ext_ref[tm:tm + CONV_HALO, :]


def conv_a(a3, conv_dw, conv_b, ln_g, ln_b, tm=256, rc=32):
    b, t, c = a3.shape
    vec = lambda v: v.reshape(1, c)
    return pl.pallas_call(
        functools.partial(_conva_kernel, tm=tm, rc=rc),
        grid=(b, t // tm),
        in_specs=[pl.BlockSpec((1, tm, c), lambda i, j: (i, j, 0)), _const_spec((CONV_W, c)),
                  _const_spec((1, c)), _const_spec((1, c)), _const_spec((1, c))],
        out_specs=pl.BlockSpec((1, tm, c), lambda i, j: (i, j, 0)),
        out_shape=jax.ShapeDtypeStruct((b, t, c), BF16),
        scratch_shapes=[pltpu.VMEM((tm + CONV_HALO, c), F32)],
        compiler_params=_cparams("arbitrary", "arbitrary"),
        name="conv_a",
    )(a3, conv_dw, vec(conv_b), vec(ln_g), vec(ln_b))


POOL_HALO = 16


def _pool_kernel(z_ref, pw_ref, sc_ref, pb_ref, ext_ref, *, tm, rc):
    i = pl.program_id(1)

    @pl.when(i == 0)
    def _():
        ext_ref[0:POOL_HALO, :] = jnp.zeros((POOL_HALO, POOL_WIDTH), F32)

    ext_ref[POOL_HALO:POOL_HALO + tm, :] = z_ref[0]
    for r0 in range(0, tm, rc):
        pos = (i * tm + r0 + lax.broadcasted_iota(jnp.int32, (rc, 1), 0)).astype(F32)
        for gi, w in enumerate(POOL_WINDOWS):
            lanes = slice(gi * POOL_GC, (gi + 1) * POOL_GC)
            base = POOL_HALO + r0
            cur = ext_ref[base:base + rc, lanes]
            s = cur
            for k in range(1, w):
                s = s + ext_ref[base - k:base - k + rc, lanes]
            cnt = jnp.minimum(pos + 1.0, float(w))
            d = s / cnt - cur
            y = jnp.dot(d.astype(BF16), pw_ref[gi], preferred_element_type=F32) * sc_ref[:, lanes]
            pb_ref[0, r0:r0 + rc, lanes] = y.astype(BF16)
    ext_ref[0:POOL_HALO, :] = ext_ref[tm:tm + POOL_HALO, :]


def pool_b(z3, pool_w, pool_scale, tm=512, rc=128):
    b, t, c = z3.shape
    return pl.pallas_call(
        functools.partial(_pool_kernel, tm=tm, rc=rc),
        grid=(b, t // tm),
        in_specs=[pl.BlockSpec((1, tm, c), lambda i, j: (i, j, 0)),
                  _const_spec((len(POOL_WINDOWS), POOL_GC, POOL_GC)), _const_spec((1, c))],
        out_specs=pl.BlockSpec((1, tm, c), lambda i, j: (i, j, 0)),
        out_shape=jax.ShapeDtypeStruct((b, t, c), BF16),
        scratch_shapes=[pltpu.VMEM((tm + POOL_HALO, c), F32)],
        compiler_params=_cparams("arbitrary", "arbitrary"),
        name="pool_b",
    )(z3, pool_w.astype(BF16), pool_scale.reshape(1, c))


SEG = CMP_STRIDE
CMP_LANES = 2 * KV_W


def _compress_weights(pe_c, w1, w2):
    eye = jnp.eye(2, dtype=F32)
    w1r = w1.reshape(2, CMP_BLOCK, HEAD_DIM, HEAD_DIM)
    def expand(w):
        return jnp.einsum('kjde,kl,gh->jkgdlhe', w, eye, eye).reshape(SEG * CMP_LANES, CMP_LANES).astype(BF16)
    wa, wb = expand(w1r[:, :SEG]), expand(w1r[:, SEG:])
    w2bd = jnp.einsum('kde,kl,gh->kgdlhe', w2, eye, eye).reshape(CMP_LANES, CMP_LANES).astype(BF16)
    def pe_row(p):
        return jnp.broadcast_to(p.transpose(1, 0, 2)[:, :, None, :], (SEG, 2, N_KV_HEADS, HEAD_DIM)).reshape(1, SEG * CMP_LANES)
    return wa, wb, w2bd, pe_row(pe_c[:, :SEG]), pe_row(pe_c[:, SEG:])


def _compress_kernel(x_ref, pea_ref, peb_ref, wa_ref, wb_ref, w2_ref, o_ref, bsc_ref, *, nseg):
    x = x_ref[0]
    a = jnp.dot((x + pea_ref[...]).astype(BF16), wa_ref[...], preferred_element_type=F32)
    bsc_ref[0:nseg, :] = jnp.dot((x + peb_ref[...]).astype(BF16), wb_ref[...], preferred_element_type=F32)
    bsc_ref[nseg:nseg + 8, :] = jnp.zeros((8, CMP_LANES), F32)
    u = a + bsc_ref[1:nseg + 1, :]
    o_ref[0] = jnp.dot(_gelu(u).astype(BF16), w2_ref[...], preferred_element_type=F32)


def compress_prompt(cmp3, cw):
    b, t, _ = cmp3.shape
    nseg = t // SEG
    wa, wb, w2bd, pea, peb = cw
    xs = cmp3.reshape(b, nseg, SEG * CMP_LANES)
    return pl.pallas_call(
        functools.partial(_compress_kernel, nseg=nseg),
        grid=(b,),
        in_specs=[pl.BlockSpec((1, nseg, SEG * CMP_LANES), lambda i: (i, 0, 0)),
                  _const_spec((1, SEG * CMP_LANES)), _const_spec((1, SEG * CMP_LANES)),
                  _const_spec((SEG * CMP_LANES, CMP_LANES)), _const_spec((SEG * CMP_LANES, CMP_LANES)),
                  _const_spec((CMP_LANES, CMP_LANES))],
        out_specs=pl.BlockSpec((1, nseg, CMP_LANES), lambda i: (i, 0, 0)),
        out_shape=jax.ShapeDtypeStruct((b, nseg, CMP_LANES), F32),
        scratch_shapes=[pltpu.VMEM((nseg + 8, CMP_LANES), F32)],
        compiler_params=_cparams("arbitrary"),
        name="compress",
    )(xs, pea, peb, wa, wb, w2bd)


QA = LANES + HEAD_DIM
Q3 = 3 * HEAD_DIM


def _softmax_rows(s, mask):
    s = jnp.where(mask, s, NEG)
    m = jnp.max(s, axis=-1, keepdims=True)
    e = jnp.where(mask, jnp.exp(s - m), 0.0)
    l = jnp.sum(e, axis=-1, keepdims=True)
    return e * (1.0 / jnp.where(l > 0.0, l, 1.0))


def _nsa_kernel(qh_ref, ql_ref, slc_ref, w0_ref, w1_ref, w2_ref, kcvc_ref, oh_ref, ng_ref, o_ref,
                kaug_ref, vsel_ref, kc3_ref, vc_ref, q3_ref, qaug_ref, m_ref, l_ref, acc_ref,
                *, t_len, tq, kc_len, ncmp):
    i = pl.program_id(1)
    nsel = t_len // SEL_BLOCK
    rows = GROUP * tq

    @pl.when(i == 0)
    def _():
        blk = lax.broadcasted_iota(jnp.int32, (t_len, LANES), 0) // SEL_BLOCK
        lane = lax.broadcasted_iota(jnp.int32, (t_len, LANES), 1)
        onehot = jnp.where(blk == lane, 1.0, 0.0).astype(BF16)
        kcvc = kcvc_ref[0]
        for g in range(N_KV_HEADS):
            kaug_ref[g, :, 0:LANES] = onehot
            kaug_ref[g, :, LANES:QA] = slc_ref[0, :, g * HEAD_DIM:(g + 1) * HEAD_DIM]
            vsel_ref[g] = slc_ref[0, :, KV_W + g * HEAD_DIM:KV_W + (g + 1) * HEAD_DIM]
            kc = kcvc[:, g * HEAD_DIM:(g + 1) * HEAD_DIM]
            kch = kc.astype(BF16)
            kcl = (kc - kch.astype(F32)).astype(BF16)
            kc3_ref[g, :, 0:HEAD_DIM] = kch
            kc3_ref[g, :, HEAD_DIM:2 * HEAD_DIM] = kch
            kc3_ref[g, :, 2 * HEAD_DIM:Q3] = kcl
            vc_ref[g] = kcvc[:, KV_W + g * HEAD_DIM:KV_W + (g + 1) * HEAD_DIM].astype(BF16)

    s0 = i * tq
    pos4 = s0 + lax.broadcasted_iota(jnp.int32, (rows, 1), 0) % tq
    ng = ng_ref[0]
    c_last = s0 // kc_len

    for g in range(N_KV_HEADS):
        for r in range(GROUP):
            hs = slice((g * GROUP + r) * HEAD_DIM, (g * GROUP + r + 1) * HEAD_DIM)
            rs = slice(r * tq, (r + 1) * tq)
            qh = qh_ref[0, :, hs]
            q3_ref[rs, 0:HEAD_DIM] = qh
            q3_ref[rs, HEAD_DIM:2 * HEAD_DIM] = ql_ref[0, :, hs]
            q3_ref[rs, 2 * HEAD_DIM:Q3] = qh
            qaug_ref[rs, LANES:QA] = qh

        s = lax.dot_general(q3_ref[...], kc3_ref[g], NT, preferred_element_type=F32)
        n_idx = lax.broadcasted_iota(jnp.int32, (1, s.shape[1]), 1)
        end = n_idx * CMP_STRIDE + (CMP_BLOCK - 1)
        mask = jnp.where(n_idx < ncmp, end, t_len + CMP_BLOCK) <= pos4
        p = _softmax_rows(s, mask)
        o_c = jnp.dot(p.astype(BF16), vc_ref[g], preferred_element_type=F32)
        psum = p[0:tq]
        for r in range(1, GROUP):
            psum = psum + p[r * tq:(r + 1) * tq]
        p_hi = psum.astype(BF16)
        p_lo = (psum - p_hi.astype(F32)).astype(BF16)
        imp = (jnp.dot(p_hi, oh_ref[...], preferred_element_type=F32)
               + jnp.dot(p_lo, oh_ref[...], preferred_element_type=F32))

        v = imp.T
        jidx = lax.broadcasted_iota(jnp.int32, (LANES, tq), 0)
        qblk = (s0 + lax.broadcasted_iota(jnp.int32, (1, tq), 1)) // SEL_BLOCK
        forced = (jidx == 0) | (jidx == qblk) | (jidx == qblk - 1)
        v = jnp.where(jidx <= qblk, jnp.where(forced, FORCE_SCORE, v), -1.0)
        v = jnp.where(jidx < nsel, v, -3.0)
        sel = jnp.zeros((LANES, tq), F32)
        for _ in range(min(N_SELECT, nsel)):
            mx = jnp.max(v, axis=0, keepdims=True)
            first = jnp.min(jnp.where(v == mx, jidx, LANES), axis=0, keepdims=True)
            pick = jidx == first
            sel = jnp.where(pick, 1.0, sel)
            v = jnp.where(pick, -2.0, v)
        selneg = jnp.where(sel.T > 0.5, 0.0, NEG).astype(BF16)
        for r in range(GROUP):
            qaug_ref[r * tq:(r + 1) * tq, 0:LANES] = selneg

        m_ref[...] = jnp.full((rows, 1), NEG, F32)
        l_ref[...] = jnp.zeros((rows, 1), F32)
        acc_ref[...] = jnp.zeros((rows, HEAD_DIM), F32)

        def chunk(c, causal):
            k0 = pl.multiple_of(c * kc_len, kc_len)
            kk = kaug_ref[g, pl.ds(k0, kc_len), :]
            vv = vsel_ref[g, pl.ds(k0, kc_len), :]
            sc = lax.dot_general(qaug_ref[...], kk, NT, preferred_element_type=F32)
            if causal:
                kpos = k0 + lax.broadcasted_iota(jnp.int32, (1, kc_len), 1)
                sc = jnp.where(kpos <= pos4, sc, NEG)
            m_prev = m_ref[...]
            m_new = jnp.maximum(m_prev, jnp.max(sc, axis=-1, keepdims=True))
            alpha = jnp.exp(m_prev - m_new)
            pe = jnp.exp(sc - m_new)
            l_ref[...] = alpha * l_ref[...] + jnp.sum(pe, axis=-1, keepdims=True)
            acc_ref[...] = alpha * acc_ref[...] + jnp.dot(pe.astype(BF16), vv, preferred_element_type=F32)
            m_ref[...] = m_new

        def body(c, carry):
            chunk(c, False)
            return carry

        lax.fori_loop(0, c_last, body, 0)
        chunk(c_last, True)
        o_s = acc_ref[...] * (1.0 / l_ref[...])

        kw = jnp.concatenate([w_ref[0, :, g * HEAD_DIM:(g + 1) * HEAD_DIM] for w_ref in (w0_ref, w1_ref, w2_ref)], axis=0)
        vw = jnp.concatenate([w_ref[0, :, KV_W + g * HEAD_DIM:KV_W + (g + 1) * HEAD_DIM]
                              for w_ref in (w0_ref, w1_ref, w2_ref)], axis=0)
        sw = lax.dot_general(q3_ref[:, 0:HEAD_DIM], kw, NT, preferred_element_type=F32)
        pos_w = s0 - 2 * tq + lax.broadcasted_iota(jnp.int32, (1, 3 * tq), 1)
        mw = (pos_w <= pos4) & (pos_w > pos4 - WINDOW) & (pos_w >= 0)
        pw = _softmax_rows(sw, mw)
        o_w = jnp.dot(pw.astype(BF16), vw, preferred_element_type=F32)

        for r in range(GROUP):
            rs = slice(r * tq, (r + 1) * tq)
            c0 = (g * GROUP + r) * 3
            o = (ng[:, c0:c0 + 1] * o_c[rs] + ng[:, c0 + 1:c0 + 2] * o_s[rs] + ng[:, c0 + 2:c0 + 3] * o_w[rs])
            o_ref[0, :, (g * GROUP + r) * HEAD_DIM:(g * GROUP + r + 1) * HEAD_DIM] = o.astype(BF16)


def nsa_prompt(qh3, ql3, slc3, winb3, kcvc, ng3, tq=256, kc_len=512):
    b, t, _ = qh3.shape
    nseg = t // SEG
    ncmp = (t - CMP_BLOCK) // CMP_STRIDE + 1
    owner = (jnp.arange(nseg) * CMP_STRIDE) // SEL_BLOCK
    oh = (owner[:, None] == jnp.arange(LANES)[None, :]).astype(BF16)
    rows = GROUP * tq
    tile = lambda c: pl.BlockSpec((1, tq, c), lambda i, j: (i, j, 0))
    wspec = lambda back: pl.BlockSpec((1, tq, 2 * KV_W), lambda i, j: (i, jnp.maximum(j - back, 0), 0))
    return pl.pallas_call(
        functools.partial(_nsa_kernel, t_len=t, tq=tq, kc_len=kc_len, ncmp=ncmp),
        grid=(b, t // tq),
        in_specs=[tile(512), tile(512),
                  pl.BlockSpec((1, t, 2 * KV_W), lambda i, j: (i, 0, 0)),
                  wspec(2), wspec(1), wspec(0),
                  pl.BlockSpec((1, nseg, CMP_LANES), lambda i, j: (i, 0, 0)),
                  _const_spec((nseg, LANES)), tile(NG_PAD)],
        out_specs=tile(512),
        out_shape=jax.ShapeDtypeStruct((b, t, 512), BF16),
        scratch_shapes=[pltpu.VMEM((N_KV_HEADS, t, QA), BF16), pltpu.VMEM((N_KV_HEADS, t, HEAD_DIM), BF16),
                        pltpu.VMEM((N_KV_HEADS, nseg, Q3), BF16), pltpu.VMEM((N_KV_HEADS, nseg, HEAD_DIM), BF16),
                        pltpu.VMEM((rows, Q3), BF16), pltpu.VMEM((rows, QA), BF16),
                        pltpu.VMEM((rows, 1), F32), pltpu.VMEM((rows, 1), F32), pltpu.VMEM((rows, HEAD_DIM), F32)],
        compiler_params=_cparams("arbitrary", "arbitrary"),
        name="nsa_prompt",
    )(qh3, ql3, slc3, winb3, winb3, winb3, kcvc, oh, ng3)


KPAD = 2 * LANES


def _softmax_cols(s, mask):
    s = jnp.where(mask, s, NEG)
    m = jnp.max(s, axis=0, keepdims=True)
    e = jnp.where(mask, jnp.exp(s - m), 0.0)
    l = jnp.sum(e, axis=0, keepdims=True)
    return e * (1.0 / jnp.where(l > 0.0, l, 1.0))


def _nsa_t_kernel(qh_ref, ql_ref, k_ref, vt_ref, wk0_ref, wk1_ref, wk2_ref, wv0_ref, wv1_ref, wv2_ref,
                  kc_ref, vct_ref, oht_ref, ng_ref, o_ref,
                  kaug_ref, kc3_ref, q3_ref, qaug_ref, m_ref, l_ref, acc_ref, sc_ref, oc_ref, ow_ref, *, t_len, tq, kc_len, ncmp):
    i = pl.program_id(1)
    nsel = t_len // SEL_BLOCK
    cols = GROUP * tq
    nseg = kc_ref.shape[1]

    @pl.when(i == 0)
    def _():
        blk = lax.broadcasted_iota(jnp.int32, (t_len, LANES), 0) // SEL_BLOCK
        lane = lax.broadcasted_iota(jnp.int32, (t_len, LANES), 1)
        onehot = jnp.where(blk == lane, 1.0, 0.0).astype(BF16)
        kc_all = kc_ref[0]
        for g in range(N_KV_HEADS):
            kaug_ref[g, :, 0:LANES] = onehot
            kaug_ref[g, :, LANES:QA] = k_ref[0, :, g * HEAD_DIM:(g + 1) * HEAD_DIM]
            kaug_ref[g, :, QA:KPAD] = jnp.zeros((t_len, KPAD - QA), BF16)
            kc = kc_all[:, g * HEAD_DIM:(g + 1) * HEAD_DIM]
            kch = kc.astype(BF16)
            kc3_ref[g, :, 0:HEAD_DIM] = kch
            kc3_ref[g, :, HEAD_DIM:2 * HEAD_DIM] = kch
            kc3_ref[g, :, 2 * HEAD_DIM:Q3] = (kc - kch.astype(F32)).astype(BF16)
            kc3_ref[g, :, Q3:KPAD] = jnp.zeros((nseg, KPAD - Q3), BF16)
        q3_ref[Q3:KPAD, :] = jnp.zeros((KPAD - Q3, cols), BF16)
        qaug_ref[QA:KPAD, :] = jnp.zeros((KPAD - QA, cols), BF16)

    s0 = i * tq
    pos = s0 + lax.broadcasted_iota(jnp.int32, (1, cols), 1) % tq
    c_last = s0 // kc_len

    for g in range(N_KV_HEADS):
        for r in range(GROUP):
            hs = slice((g * GROUP + r) * HEAD_DIM, (g * GROUP + r + 1) * HEAD_DIM)
            cs = slice(r * tq, (r + 1) * tq)
            qh = qh_ref[0, hs, :]
            q3_ref[0:HEAD_DIM, cs] = qh
            q3_ref[HEAD_DIM:2 * HEAD_DIM, cs] = ql_ref[0, hs, :]
            q3_ref[2 * HEAD_DIM:Q3, cs] = qh
            qaug_ref[LANES:QA, cs] = qh

        s = jnp.dot(kc3_ref[g], q3_ref[...], preferred_element_type=F32)
        n_idx = lax.broadcasted_iota(jnp.int32, (nseg, 1), 0)
        end = jnp.where(n_idx < ncmp, n_idx * CMP_STRIDE + (CMP_BLOCK - 1), t_len + CMP_BLOCK)
        p = _softmax_cols(s, end <= pos)
        vct = vct_ref[0, g * HEAD_DIM:(g + 1) * HEAD_DIM, :].astype(BF16)
        o_c = jnp.dot(vct, p.astype(BF16), preferred_element_type=F32)
        psum = p[:, 0:tq]
        for r in range(1, GROUP):
            psum = psum + p[:, r * tq:(r + 1) * tq]
        p_hi = psum.astype(BF16)
        p_lo = (psum - p_hi.astype(F32)).astype(BF16)
        v = (jnp.dot(oht_ref[...], p_hi, preferred_element_type=F32)
             + jnp.dot(oht_ref[...], p_lo, preferred_element_type=F32))

        jidx = lax.broadcasted_iota(jnp.int32, (LANES, tq), 0)
        qblk = (s0 + lax.broadcasted_iota(jnp.int32, (1, tq), 1)) // SEL_BLOCK
        forced = (jidx == 0) | (jidx == qblk) | (jidx == qblk - 1)
        v = jnp.where(jidx <= qblk, jnp.where(forced, FORCE_SCORE, v), -1.0)
        v = jnp.where(jidx < nsel, v, -3.0)
        sel = jnp.zeros((LANES, tq), F32)
        for _ in range(min(N_SELECT, nsel)):
            mx = jnp.max(v, axis=0, keepdims=True)
            first = jnp.min(jnp.where(v == mx, jidx, LANES), axis=0, keepdims=True)
            pick = jidx == first
            sel = jnp.where(pick, 1.0, sel)
            v = jnp.where(pick, -2.0, v)
        selneg = jnp.where(sel > 0.5, 0.0, NEG).astype(BF16)
        for r in range(GROUP):
            qaug_ref[0:LANES, r * tq:(r + 1) * tq] = selneg

        ks = slice(g * HEAD_DIM, (g + 1) * HEAD_DIM)
        kw = jnp.concatenate([w[0, :, ks] for w in (wk0_ref, wk1_ref, wk2_ref)], axis=0)
        vwt = jnp.concatenate([w[0, ks, :] for w in (wv0_ref, wv1_ref, wv2_ref)], axis=1)
        sw = jnp.dot(kw, q3_ref[0:HEAD_DIM, :], preferred_element_type=F32)
        pos_w = s0 - 2 * tq + lax.broadcasted_iota(jnp.int32, (3 * tq, 1), 0)
        mw = (pos_w <= pos) & (pos_w > pos - WINDOW) & (pos_w >= 0)
        pw = _softmax_cols(sw, mw)
        o_w = jnp.dot(vwt, pw.astype(BF16), preferred_element_type=F32)
        oc_ref[...] = o_c
        ow_ref[...] = o_w

        m_ref[...] = jnp.full((1, cols), NEG, F32)
        l_ref[...] = jnp.zeros((1, cols), F32)
        acc_ref[...] = jnp.zeros((HEAD_DIM, cols), F32)

        def scores(c):
            k0 = pl.multiple_of(c * kc_len, kc_len)
            return jnp.dot(kaug_ref[g, pl.ds(k0, kc_len), :], qaug_ref[...], preferred_element_type=F32)

        def update(c, sc, cmax, causal):
            k0 = pl.multiple_of(c * kc_len, kc_len)
            if causal:
                kpos = k0 + lax.broadcasted_iota(jnp.int32, (kc_len, 1), 0)
                sc = jnp.where(kpos <= pos, sc, NEG)
                cmax = jnp.max(sc, axis=0, keepdims=True)
            m_prev = m_ref[...]
            m_new = jnp.maximum(m_prev, cmax)
            alpha = jnp.exp(m_prev - m_new)
            pe = jnp.exp(sc - m_new)
            l_ref[...] = alpha * l_ref[...] + jnp.sum(pe, axis=0, keepdims=True)
            vt = vt_ref[0, g * HEAD_DIM:(g + 1) * HEAD_DIM, pl.ds(k0, kc_len)]
            acc_ref[...] = alpha * acc_ref[...] + jnp.dot(vt, pe.astype(BF16), preferred_element_type=F32)
            m_ref[...] = m_new

        sc_ref[0] = scores(0)

        def body(c, carry):
            slot = c % 2
            sc = sc_ref[slot]
            sc_ref[1 - slot] = scores(c + 1)
            update(c, sc, jnp.max(sc, axis=0, keepdims=True), False)
            return carry

        lax.fori_loop(0, c_last, body, 0)
        update(c_last, sc_ref[c_last % 2], None, True)
        o_s = acc_ref[...] * (1.0 / l_ref[...])

        for r in range(GROUP):
            cs = slice(r * tq, (r + 1) * tq)
            c0 = (g * GROUP + r) * 3
            o = (ng_ref[0, c0:c0 + 1, :] * oc_ref[:, cs] + ng_ref[0, c0 + 1:c0 + 2, :] * o_s[:, cs]
                 + ng_ref[0, c0 + 2:c0 + 3, :] * ow_ref[:, cs])
            o_ref[0, (g * GROUP + r) * HEAD_DIM:(g * GROUP + r + 1) * HEAD_DIM, :] = o.astype(BF16)


def nsa_prompt_t(qh3, ql3, slc3, winb3, kcvc, ng3, tq=256, kc_len=512):
    b, t, _ = qh3.shape
    nseg = t // SEG
    ncmp = (t - CMP_BLOCK) // CMP_STRIDE + 1
    owner = (jnp.arange(nseg) * CMP_STRIDE) // SEL_BLOCK
    oht = (jnp.arange(LANES)[:, None] == owner[None, :]).astype(BF16)
    tr = lambda v: v.transpose(0, 2, 1)
    cols = GROUP * tq
    qspec = pl.BlockSpec((1, 512, tq), lambda i, j: (i, 0, j))
    wk = lambda back: pl.BlockSpec((1, tq, KV_W), lambda i, j: (i, jnp.maximum(j - back, 0), 0))
    wv = lambda back: pl.BlockSpec((1, KV_W, tq), lambda i, j: (i, 1, jnp.maximum(j - back, 0)))
    winbt = tr(winb3)
    out_t = pl.pallas_call(
        functools.partial(_nsa_t_kernel, t_len=t, tq=tq, kc_len=kc_len, ncmp=ncmp),
        grid=(b, t // tq),
        in_specs=[qspec, qspec,
                  pl.BlockSpec((1, t, KV_W), lambda i, j: (i, 0, 0)),
                  pl.BlockSpec((1, KV_W, t), lambda i, j: (i, 1, 0)),
                  wk(2), wk(1), wk(0), wv(2), wv(1), wv(0),
                  pl.BlockSpec((1, nseg, KV_W), lambda i, j: (i, 0, 0)),
                  pl.BlockSpec((1, KV_W, nseg), lambda i, j: (i, 1, 0)),
                  _const_spec((LANES, nseg)),
                  pl.BlockSpec((1, NG_PAD, tq), lambda i, j: (i, 0, j))],
        out_specs=qspec,
        out_shape=jax.ShapeDtypeStruct((b, 512, t), BF16),
        scratch_shapes=[pltpu.VMEM((N_KV_HEADS, t, KPAD), BF16), pltpu.VMEM((N_KV_HEADS, nseg, KPAD), BF16),
                        pltpu.VMEM((KPAD, cols), BF16), pltpu.VMEM((KPAD, cols), BF16),
                        pltpu.VMEM((1, cols), F32), pltpu.VMEM((1, cols), F32), pltpu.VMEM((HEAD_DIM, cols), F32),
                        pltpu.VMEM((2, kc_len, cols), F32), pltpu.VMEM((HEAD_DIM, cols), F32),
                        pltpu.VMEM((HEAD_DIM, cols), F32)],
        compiler_params=_cparams("arbitrary", "arbitrary"),
        name="nsa_prompt",
    )(tr(qh3), tr(ql3), slc3, tr(slc3), winb3, winb3, winb3, winbt, winbt, winbt, kcvc, tr(kcvc), oht, tr(ng3))
    return tr(out_t)


def _merge_kernel(c_ref, pb_ref, oc_ref, mg_ref, x_ref, wa_ref, wb_ref, wc_ref, wo_ref, o_ref):
    out_a = jnp.dot(c_ref[...], wa_ref[...], preferred_element_type=F32)
    out_b = jnp.dot(pb_ref[...], wb_ref[...], preferred_element_type=F32)
    out_c = jnp.dot(oc_ref[...], wc_ref[...], preferred_element_type=F32)
    m = (mg_ref[:, 0:D_MODEL] * out_a + mg_ref[:, D_MODEL:2 * D_MODEL] * out_b
         + mg_ref[:, 2 * D_MODEL:3 * D_MODEL] * out_c)
    o_ref[...] = x_ref[...] + jnp.dot(m.astype(BF16), wo_ref[...], preferred_element_type=F32)


def merge(c2, pb2, oc2, mg2, x2, wa, wb, wc, wo, tm=512):
    n = x2.shape[0]
    row = lambda c: pl.BlockSpec((tm, c), lambda i: (i, 0))
    return pl.pallas_call(
        _merge_kernel,
        grid=(n // tm,),
        in_specs=[row(512), row(512), row(512), row(3 * D_MODEL), row(D_MODEL),
                  _const_spec((512, D_MODEL)), _const_spec((512, D_MODEL)), _const_spec((512, D_MODEL)),
                  _const_spec((D_MODEL, D_MODEL))],
        out_specs=row(D_MODEL),
        out_shape=jax.ShapeDtypeStruct((n, D_MODEL), F32),
        compiler_params=_cparams("parallel"),
        name="merge",
    )(c2, pb2, oc2, mg2, x2, wa, wb, wc, wo)


FF_CHUNK = 256
FF_HALO = 8


def _ffn_kernel(x_ref, pe_ref, gf_ref, wup_ref, dw_ref, wdn_ref, gp_ref, wg_ref, wp_ref, o_ref, nf_ref,
                usc_ref, carry_ref, *, tm):
    @pl.when(pl.program_id(1) == 0)
    def _():
        carry_ref[...] = jnp.zeros((FF_HALO, 2 * D_FF), F32)

    x = x_ref[0]
    h = _rms(x, gf_ref[...]).astype(BF16)
    acc = jnp.zeros((tm, D_MODEL), F32)
    for c in range(D_FF // FF_CHUNK):
        cu = []
        for half in range(2):
            col = half * D_FF + c * FF_CHUNK
            cs = slice(col, col + FF_CHUNK)
            u = jnp.dot(h, wup_ref[:, cs], preferred_element_type=F32)
            usc_ref[half, 0:FF_HALO, :] = carry_ref[:, cs]
            usc_ref[half, FF_HALO:FF_HALO + tm, :] = u
            carry_ref[:, cs] = u[tm - FF_HALO:tm]
            cu.append(usc_ref[half, FF_HALO - 2:FF_HALO - 2 + tm, :] * dw_ref[0:1, cs]
                      + usc_ref[half, FF_HALO - 1:FF_HALO - 1 + tm, :] * dw_ref[1:2, cs]
                      + u * dw_ref[2:3, cs])
        f = _gelu(cu[0]) * cu[1]
        acc = acc + jnp.dot(f.astype(BF16), wdn_ref[c * FF_CHUNK:(c + 1) * FF_CHUNK, :], preferred_element_type=F32)
    nf_ref[0] = carry_ref[FF_HALO - 2:FF_HALO, :]
    x2 = x + acc
    gate = jax.nn.sigmoid(jnp.dot(_rms(x2, gp_ref[...]).astype(BF16), wg_ref[...], preferred_element_type=F32))
    o_ref[0] = x2 + gate * jnp.dot(pe_ref[0].astype(BF16), wp_ref[...], preferred_element_type=F32)


def ffn_prompt(x3, pe3, g_ffn, w_up, ffn_dw, w_down, g_ple, w_gate, w_ple, tm=512):
    b, t, _ = x3.shape
    tile = lambda c: pl.BlockSpec((1, tm, c), lambda i, j: (i, j, 0))
    return pl.pallas_call(
        functools.partial(_ffn_kernel, tm=tm),
        grid=(b, t // tm),
        in_specs=[tile(D_MODEL), tile(PLE_DIM), _const_spec((1, D_MODEL)), _const_spec((D_MODEL, 2 * D_FF)),
                  _const_spec((FFN_CONV_W, 2 * D_FF)), _const_spec((D_FF, D_MODEL)), _const_spec((1, D_MODEL)),
                  _const_spec((D_MODEL, D_MODEL)), _const_spec((PLE_DIM, D_MODEL))],
        out_specs=[tile(D_MODEL), pl.BlockSpec((1, FFN_CONV_W - 1, 2 * D_FF), lambda i, j: (i, 0, 0))],
        out_shape=[jax.ShapeDtypeStruct((b, t, D_MODEL), F32),
                   jax.ShapeDtypeStruct((b, FFN_CONV_W - 1, 2 * D_FF), F32)],
        scratch_shapes=[pltpu.VMEM((2, tm + FF_HALO, FF_CHUNK), F32), pltpu.VMEM((FF_HALO, 2 * D_FF), F32)],
        compiler_params=_cparams("arbitrary", "arbitrary"),
        name="ffn",
    )(x3, pe3, g_ffn.reshape(1, D_MODEL), w_up, ffn_dw, w_down, g_ple.reshape(1, D_MODEL), w_gate, w_ple)


def _norm_kernel(x_ref, g_ref, o_ref):
    o_ref[...] = _rms(x_ref[...], g_ref[...])


def final_norm(x2, g, tm=512):
    n = x2.shape[0]
    return pl.pallas_call(
        _norm_kernel,
        grid=(n // tm,),
        in_specs=[pl.BlockSpec((tm, D_MODEL), lambda i: (i, 0)), _const_spec((1, D_MODEL))],
        out_specs=pl.BlockSpec((tm, D_MODEL), lambda i: (i, 0)),
        out_shape=jax.ShapeDtypeStruct((n, D_MODEL), F32),
        compiler_params=_cparams("parallel"),
        name="final_norm",
    )(x2, g.reshape(1, D_MODEL))


def prompt_layer(x3, pe3, lw, tables):
    b, t, _ = x3.shape
    n = b * t
    (a, zb, qh, ql, kv4, win, cmpx, slc, winb, ng, mg) = inproj(
        x3.reshape(n, D_MODEL), lw['norm_mix'], lw['w_in'], tables, t, 512)
    r3 = lambda v: v.reshape(b, t, v.shape[-1])
    a3 = r3(a)
    zb3 = r3(zb)
    c = conv_a(a3, lw['conv_dw'], lw['conv_b'], lw['conv_ln_g'], lw['conv_ln_b'])
    pb = pool_b(zb3, lw['pool_w'], lw['pool_scale'])
    kcvc = compress_prompt(r3(cmpx), lw['cmp'])
    oc = nsa_prompt_t(r3(qh), r3(ql), r3(slc), r3(winb), kcvc, r3(ng))
    x1 = merge(c.reshape(n, 512), pb.reshape(n, 512), oc.reshape(n, 512), mg, x3.reshape(n, D_MODEL),
               lw['w_a_out'], lw['w_b_out'], lw['w_c_out'], lw['w_o'])
    x2, new_ffn = ffn_prompt(x1.reshape(b, t, D_MODEL), pe3, lw['norm_ffn'], lw['w_up'], lw['ffn_dw'], lw['w_down'],
                             lw['norm_ple'], lw['w_ple_gate'], lw['w_ple'])
    wp = min(WINDOW, t)
    new_kv = kv4.reshape(b, t, 4, N_KV_HEADS, HEAD_DIM)
    new_win = r3(win)[:, t - wp:].reshape(b, wp, 2, N_KV_HEADS, HEAD_DIM)
    new_conv = a3[:, t - (CONV_W - 1):]
    new_pool = zb3[:, t - (POOL_MAX - 1):]
    return x2, (new_conv, new_pool, new_ffn, new_kv, new_win)


def layer_weights(i, norm_mix, w_in, conv_dw, conv_b, conv_ln_g, conv_ln_b, w_a_out, pool_w, pool_scale, w_b_out,
                  pe_cmp, w_cmp1, w_cmp2, w_c_out, w_o, norm_ffn, w_up, ffn_dw, w_down, norm_ple, w_ple_gate, w_ple):
    return dict(norm_mix=norm_mix[i], w_in=_pack_w_in(w_in[i]), conv_dw=conv_dw[i], conv_b=conv_b[i],
                conv_ln_g=conv_ln_g[i], conv_ln_b=conv_ln_b[i], w_a_out=w_a_out[i].astype(BF16), pool_w=pool_w[i],
                pool_scale=pool_scale[i], w_b_out=w_b_out[i].astype(BF16),
                cmp=_compress_weights(pe_cmp[i], w_cmp1[i], w_cmp2[i]),
                cmp_s=_compress_weights_sample(pe_cmp[i], w_cmp1[i], w_cmp2[i]), w_c_out=w_c_out[i].astype(BF16),
                w_o=w_o[i].astype(BF16), norm_ffn=norm_ffn[i], w_up=w_up[i].astype(BF16), ffn_dw=ffn_dw[i],
                w_down=w_down[i].astype(BF16), norm_ple=norm_ple[i], w_ple_gate=w_ple_gate[i].astype(BF16),
                w_ple=w_ple[i].astype(BF16))


SB = 32


def _conva_s_kernel(st_ref, a_ref, w_ref, b_ref, g_ref, beta_ref, c_ref, *, t_new, bsz):
    hist = CONV_W - 1
    for t in range(t_new):
        for b0 in range(0, bsz, SB):
            acc = jnp.zeros((SB, CONV_WIDTH), F32)
            for k in range(CONV_W):
                j = t + k
                row = st_ref[j, b0:b0 + SB, :] if j < hist else a_ref[j - hist, b0:b0 + SB, :]
                acc = acc + row * w_ref[k:k + 1, :]
            y = acc + b_ref[...]
            mu = jnp.mean(y, axis=-1, keepdims=True)
            d = y - mu
            var = jnp.mean(d * d, axis=-1, keepdims=True)
            yn = d * lax.rsqrt(var + EPS) * g_ref[...] + beta_ref[...]
            c_ref[t, b0:b0 + SB, :] = (yn * jax.nn.sigmoid(yn)).astype(BF16)


def conv_a_sample(st_tm, a_tm, conv_dw, conv_b, ln_g, ln_b):
    t_new, bsz, c = a_tm.shape
    vec = lambda v: v.reshape(1, c)
    return pl.pallas_call(
        functools.partial(_conva_s_kernel, t_new=t_new, bsz=bsz),
        out_shape=jax.ShapeDtypeStruct((t_new, bsz, c), BF16),
        compiler_params=pltpu.CompilerParams(vmem_limit_bytes=VMEM_LIMIT),
        name="conv_a_sample",
    )(st_tm, a_tm, conv_dw, vec(conv_b), vec(ln_g), vec(ln_b))


def _pool_s_kernel(st_ref, z_ref, pw_ref, sc_ref, pb_ref, *, t_new, pos0):
    hist = POOL_MAX - 1
    for t in range(t_new):
        for gi, w in enumerate(POOL_WINDOWS):
            lanes = slice(gi * POOL_GC, (gi + 1) * POOL_GC)
            cur = z_ref[t, :, lanes]
            s = cur
            for k in range(1, w):
                j = hist + t - k
                s = s + (st_ref[j, :, lanes] if j < hist else z_ref[j - hist, :, lanes])
            cnt = float(min(pos0 + t + 1, w))
            d = s / cnt - cur
            y = jnp.dot(d.astype(BF16), pw_ref[gi], preferred_element_type=F32) * sc_ref[:, lanes]
            pb_ref[t, :, lanes] = y.astype(BF16)


def pool_b_sample(st_tm, z_tm, pool_w, pool_scale, pos0):
    t_new, bsz, c = z_tm.shape
    return pl.pallas_call(
        functools.partial(_pool_s_kernel, t_new=t_new, pos0=pos0),
        out_shape=jax.ShapeDtypeStruct((t_new, bsz, c), BF16),
        compiler_params=pltpu.CompilerParams(vmem_limit_bytes=VMEM_LIMIT),
        name="pool_b_sample",
    )(st_tm, z_tm, pool_w.astype(BF16), pool_scale.reshape(1, c))


def _ffn_s_kernel(x_ref, pe_ref, gf_ref, wv_ref, wg_ref, dwv_ref, dwg_ref, wdn_ref, stv_ref, stg_ref,
                  gp_ref, wgate_ref, wp_ref, o_ref, nfv_ref, nfg_ref, h_ref, acc_ref, *, t_new, bsz):
    c = pl.program_id(0)

    @pl.when(c == 0)
    def _():
        h_ref[...] = _rms(x_ref[...], gf_ref[...]).astype(BF16)
        acc_ref[...] = jnp.zeros(acc_ref.shape, F32)

    def conv_half(w_ref, dw_ref, st_ref, nf_ref):
        u = jnp.dot(h_ref[...], w_ref[...], preferred_element_type=F32)
        ext = [st_ref[0], st_ref[1]] + [u[t * bsz:(t + 1) * bsz] for t in range(t_new)]
        nf_ref[0] = ext[t_new]
        nf_ref[1] = ext[t_new + 1]
        return jnp.concatenate(
            [ext[t] * dw_ref[0:1, :] + ext[t + 1] * dw_ref[1:2, :] + ext[t + 2] * dw_ref[2:3, :] for t in range(t_new)],
            axis=0)

    cv = conv_half(wv_ref, dwv_ref, stv_ref, nfv_ref)
    cg = conv_half(wg_ref, dwg_ref, stg_ref, nfg_ref)
    f = _gelu(cv) * cg
    acc_ref[...] += jnp.dot(f.astype(BF16), wdn_ref[...], preferred_element_type=F32)

    @pl.when(c == pl.num_programs(0) - 1)
    def _():
        x2 = x_ref[...] + acc_ref[...]
        gate = jax.nn.sigmoid(jnp.dot(_rms(x2, gp_ref[...]).astype(BF16), wgate_ref[...], preferred_element_type=F32))
        o_ref[...] = x2 + gate * jnp.dot(pe_ref[...].astype(BF16), wp_ref[...], preferred_element_type=F32)


def ffn_sample(x2, pe2, st_tm, g_ffn, w_up, ffn_dw, w_down, g_ple, w_gate, w_ple, t_new):
    n = x2.shape[0]
    bsz = n // t_new
    nch = D_FF // FF_CHUNK
    full = lambda r, c: pl.BlockSpec((r, c), lambda i: (0, 0))
    return pl.pallas_call(
        functools.partial(_ffn_s_kernel, t_new=t_new, bsz=bsz),
        grid=(nch,),
        in_specs=[full(n, D_MODEL), full(n, PLE_DIM), full(1, D_MODEL),
                  pl.BlockSpec((D_MODEL, FF_CHUNK), lambda i: (0, i)),
                  pl.BlockSpec((D_MODEL, FF_CHUNK), lambda i: (0, nch + i)),
                  pl.BlockSpec((FFN_CONV_W, FF_CHUNK), lambda i: (0, i)),
                  pl.BlockSpec((FFN_CONV_W, FF_CHUNK), lambda i: (0, nch + i)),
                  pl.BlockSpec((FF_CHUNK, D_MODEL), lambda i: (i, 0)),
                  pl.BlockSpec((FFN_CONV_W - 1, bsz, FF_CHUNK), lambda i: (0, 0, i)),
                  pl.BlockSpec((FFN_CONV_W - 1, bsz, FF_CHUNK), lambda i: (0, 0, nch + i)),
                  full(1, D_MODEL), full(D_MODEL, D_MODEL), full(PLE_DIM, D_MODEL)],
        out_specs=[full(n, D_MODEL),
                   pl.BlockSpec((FFN_CONV_W - 1, bsz, FF_CHUNK), lambda i: (0, 0, i)),
                   pl.BlockSpec((FFN_CONV_W - 1, bsz, FF_CHUNK), lambda i: (0, 0, i))],
        out_shape=[jax.ShapeDtypeStruct((n, D_MODEL), F32),
                   jax.ShapeDtypeStruct((FFN_CONV_W - 1, bsz, D_FF), F32),
                   jax.ShapeDtypeStruct((FFN_CONV_W - 1, bsz, D_FF), F32)],
        scratch_shapes=[pltpu.VMEM((n, D_MODEL), BF16), pltpu.VMEM((n, D_MODEL), F32)],
        compiler_params=_cparams("arbitrary"),
        name="ffn_sample",
    )(x2, pe2, g_ffn.reshape(1, D_MODEL), w_up, w_up, ffn_dw, ffn_dw, w_down, st_tm, st_tm,
      g_ple.reshape(1, D_MODEL), w_gate, w_ple)


PAGE = 128
SEG_PER_PAGE = PAGE // SEG


def _compress_weights_sample(pe_c, w1, w2):
    eye = jnp.eye(2, dtype=F32)
    w1r = w1.reshape(2, CMP_BLOCK, HEAD_DIM, HEAD_DIM)
    def expand(w):
        return jnp.einsum('kjde,kl,gh->jkgdlhe', w, eye, eye).reshape(SEG, CMP_LANES, CMP_LANES).astype(BF16)
    w2bd = jnp.einsum('kde,kl,gh->kgdlhe', w2, eye, eye).reshape(CMP_LANES, CMP_LANES).astype(BF16)
    def pe_rows(p):
        return jnp.broadcast_to(p.transpose(1, 0, 2)[:, :, None, :], (SEG, 2, N_KV_HEADS, HEAD_DIM)).reshape(SEG, CMP_LANES)
    return expand(w1r[:, :SEG]), expand(w1r[:, SEG:]), w2bd, pe_rows(pe_c[:, :SEG]), pe_rows(pe_c[:, SEG:])


def _compress_s_kernel(pt_ref, *refs, n_pages, bt):
    del pt_ref
    np_all = bt * n_pages
    pages_k, pages_v = refs[:np_all], refs[np_all:2 * np_all]
    pea_ref, peb_ref, wa_ref, wb_ref, w2_ref, o_ref, bsc_ref = refs[2 * np_all:]
    m = bt * n_pages * SEG_PER_PAGE
    a = jnp.zeros((m, CMP_LANES), F32)
    bm = jnp.zeros((m, CMP_LANES), F32)
    for j in range(SEG):
        xj = jnp.concatenate(
            [jnp.concatenate([pg[pl.ds(j, SEG_PER_PAGE, stride=SEG), :] for pg in pgs], axis=0)
             for pgs in (pages_k, pages_v)], axis=1)
        a = a + jnp.dot((xj + pea_ref[j:j + 1, :]).astype(BF16), wa_ref[j], preferred_element_type=F32)
        bm = bm + jnp.dot((xj + peb_ref[j:j + 1, :]).astype(BF16), wb_ref[j], preferred_element_type=F32)
    bsc_ref[0:m, :] = bm
    bsc_ref[m:m + 8, :] = jnp.zeros((8, CMP_LANES), F32)
    u = a + bsc_ref[1:m + 1, :]
    out = jnp.dot(_gelu(u).astype(BF16), w2_ref[...], preferred_element_type=F32)
    nseg = n_pages * SEG_PER_PAGE
    for q in range(bt):
        o_ref[q] = out[q * nseg:(q + 1) * nseg]


def compress_sample(cache2, page_table, cw, bt=4):
    bsz, n_pages = page_table.shape
    nseg = n_pages * SEG_PER_PAGE
    wa, wb, w2bd, pea, peb = cw
    page_specs = [pl.BlockSpec((None, PAGE, KV_W),
                               functools.partial(lambda i, pt, q, p, kind: (pt[i * bt + q, p], 0, kind), q=q, p=p, kind=kind))
                  for kind in range(2) for q in range(bt) for p in range(n_pages)]
    const = lambda shape: pl.BlockSpec(shape, lambda i, pt: (0,) * len(shape))
    grid_spec = pltpu.PrefetchScalarGridSpec(
        num_scalar_prefetch=1, grid=(bsz // bt,),
        in_specs=page_specs + [const((SEG, CMP_LANES)), const((SEG, CMP_LANES)), const((SEG, CMP_LANES, CMP_LANES)),
                               const((SEG, CMP_LANES, CMP_LANES)), const((CMP_LANES, CMP_LANES))],
        out_specs=pl.BlockSpec((bt, nseg, CMP_LANES), lambda i, pt: (i, 0, 0)),
        scratch_shapes=[pltpu.VMEM((bt * nseg + 8, CMP_LANES), F32)])
    return pl.pallas_call(
        functools.partial(_compress_s_kernel, n_pages=n_pages, bt=bt),
        grid_spec=grid_spec,
        out_shape=jax.ShapeDtypeStruct((bsz, nseg, CMP_LANES), F32),
        compiler_params=_cparams("arbitrary"),
        name="compress_sample",
    )(page_table, *([cache2] * (2 * bt * n_pages)), pea, peb, wa, wb, w2bd)


NEW_PAD = 16


def _nsa_s_kernel(pt_ref, *refs, n_pages, t_new, win_len):
    del pt_ref
    pages = refs[:n_pages]
    (q3_ref, ngr_ref, kcvc_ref, nslc_ref, wst_ref, nwin_ref, oht_ref, o_ref,
     kaug_ref, vsel_ref, kw_ref, vw_ref, kc3_ref, qaug_ref) = refs[n_pages:]
    past = n_pages * PAGE
    kl = kaug_ref.shape[1]
    wl = kw_ref.shape[1]
    rows = GROUP * t_new
    nseg = kcvc_ref.shape[1]
    ncmp = (past + t_new - CMP_BLOCK) // CMP_STRIDE + 1
    nsel = -(-(past + t_new) // SEL_BLOCK)

    @pl.when(pl.program_id(0) == 0)
    def _():
        blk = lax.broadcasted_iota(jnp.int32, (kl, LANES), 0) // SEL_BLOCK
        lane = lax.broadcasted_iota(jnp.int32, (kl, LANES), 1)
        onehot = jnp.where(blk == lane, 1.0, 0.0).astype(BF16)
        for g in range(N_KV_HEADS):
            kaug_ref[g, :, 0:LANES] = onehot
            kaug_ref[g, past:kl, LANES:QA] = jnp.zeros((kl - past, HEAD_DIM), BF16)
            vsel_ref[g, past:kl, :] = jnp.zeros((kl - past, HEAD_DIM), BF16)
            kw_ref[g, win_len:wl, :] = jnp.zeros((wl - win_len, HEAD_DIM), BF16)
            vw_ref[g, win_len:wl, :] = jnp.zeros((wl - win_len, HEAD_DIM), BF16)

    pos4 = past + lax.broadcasted_iota(jnp.int32, (rows, 1), 0) % t_new
    kcvc = kcvc_ref[0]
    nslc = nslc_ref[0]
    nwin = nwin_ref[0]
    wst = wst_ref[0]
    outs = []
    for g in range(N_KV_HEADS):
        ks = slice(g * HEAD_DIM, (g + 1) * HEAD_DIM)
        vs = slice(KV_W + g * HEAD_DIM, KV_W + (g + 1) * HEAD_DIM)
        for p in range(n_pages):
            kaug_ref[g, p * PAGE:(p + 1) * PAGE, LANES:QA] = pages[p][:, ks].astype(BF16)
            vsel_ref[g, p * PAGE:(p + 1) * PAGE, :] = pages[p][:, vs].astype(BF16)
        kaug_ref[g, past:past + NEW_PAD, LANES:QA] = nslc[:, ks]
        vsel_ref[g, past:past + NEW_PAD, :] = nslc[:, vs]
        kw_ref[g, 0:win_len, :] = wst[:, ks].astype(BF16)
        vw_ref[g, 0:win_len, :] = wst[:, vs].astype(BF16)
        kw_ref[g, win_len:win_len + NEW_PAD, :] = nwin[:, ks]
        vw_ref[g, win_len:win_len + NEW_PAD, :] = nwin[:, vs]
        kc = kcvc[:, ks]
        kch = kc.astype(BF16)
        kc3_ref[g, :, 0:HEAD_DIM] = kch
        kc3_ref[g, :, HEAD_DIM:2 * HEAD_DIM] = kch
        kc3_ref[g, :, 2 * HEAD_DIM:Q3] = (kc - kch.astype(F32)).astype(BF16)
        vc = kcvc[:, vs].astype(BF16)
        q3 = q3_ref[0, g]

        s = lax.dot_general(q3, kc3_ref[g], NT, preferred_element_type=F32)
        n_idx = lax.broadcasted_iota(jnp.int32, (1, nseg), 1)
        end = n_idx * CMP_STRIDE + (CMP_BLOCK - 1)
        mask = jnp.where(n_idx < ncmp, end, past + t_new + CMP_BLOCK) <= pos4
        p = _softmax_rows(s, mask)
        o_c = jnp.dot(p.astype(BF16), vc, preferred_element_type=F32)
        ppad = jnp.concatenate([p, jnp.zeros((LANES - rows, nseg), F32)], axis=0)
        p_hi = ppad.astype(BF16)
        p_lo = (ppad - p_hi.astype(F32)).astype(BF16)
        x = (lax.dot_general(oht_ref[...], p_hi, NT, preferred_element_type=F32)
             + lax.dot_general(oht_ref[...], p_lo, NT, preferred_element_type=F32))
        v = x
        for r in range(1, GROUP):
            v = v + pltpu.roll(x, LANES - r * t_new, 1)

        jidx = lax.broadcasted_iota(jnp.int32, (LANES, LANES), 0)
        qblk = (past + lax.broadcasted_iota(jnp.int32, (1, LANES), 1) % t_new) // SEL_BLOCK
        forced = (jidx == 0) | (jidx == qblk) | (jidx == qblk - 1)
        v = jnp.where(jidx <= qblk, jnp.where(forced, FORCE_SCORE, v), -1.0)
        v = jnp.where(jidx < nsel, v, -3.0)
        sel = jnp.zeros((LANES, LANES), F32)
        for _ in range(min(N_SELECT, nsel)):
            mx = jnp.max(v, axis=0, keepdims=True)
            first = jnp.min(jnp.where(v == mx, jidx, LANES), axis=0, keepdims=True)
            pick = jidx == first
            sel = jnp.where(pick, 1.0, sel)
            v = jnp.where(pick, -2.0, v)
        ri = lax.broadcasted_iota(jnp.int32, (rows, LANES), 0) % t_new
        li = lax.broadcasted_iota(jnp.int32, (rows, LANES), 1)
        spread = jnp.where(ri == li, 1.0, 0.0).astype(BF16)
        sel_rows = lax.dot_general(spread, sel.astype(BF16), NT, preferred_element_type=F32)
        qaug_ref[g, :, 0:LANES] = jnp.where(sel_rows > 0.5, 0.0, NEG).astype(BF16)
        qaug_ref[g, :, LANES:QA] = q3[:, 0:HEAD_DIM]

        sc = lax.dot_general(qaug_ref[g], kaug_ref[g], NT, preferred_element_type=F32)
        kpos = lax.broadcasted_iota(jnp.int32, (1, kl), 1)
        ps = _softmax_rows(sc, kpos <= pos4)
        o_s = jnp.dot(ps.astype(BF16), vsel_ref[g], preferred_element_type=F32)

        sw = lax.dot_general(q3[:, 0:HEAD_DIM], kw_ref[g], NT, preferred_element_type=F32)
        widx = lax.broadcasted_iota(jnp.int32, (1, wl), 1)
        pos_w = past - win_len + widx
        mw = (widx < win_len + t_new) & (pos_w <= pos4) & (pos_w > pos4 - WINDOW) & (pos_w >= 0)
        pw = _softmax_rows(sw, mw)
        o_w = jnp.dot(pw.astype(BF16), vw_ref[g], preferred_element_type=F32)

        ngr = ngr_ref[0, g]
        outs.append(ngr[:, 0:1] * o_c + ngr[:, 1:2] * o_s + ngr[:, 2:3] * o_w)
    o_ref[0] = jnp.stack(outs, axis=0)


def nsa_sample_attn(cache2, page_table, q3, ngr, kcvc, nslc, wst, nwin):
    bsz, n_pages = page_table.shape
    t_new = q3.shape[2] // GROUP
    rows = GROUP * t_new
    past = n_pages * PAGE
    win_len = wst.shape[1]
    nseg = kcvc.shape[1]
    kl = past + LANES
    wl = win_len + LANES
    owner = (jnp.arange(nseg) * CMP_STRIDE) // SEL_BLOCK
    oht = (jnp.arange(LANES)[:, None] == owner[None, :]).astype(BF16)
    page_specs = [pl.BlockSpec((None, PAGE, 2 * KV_W), functools.partial(lambda i, pt, p: (pt[i, p], 0, 1), p=p))
                  for p in range(n_pages)]
    per_b = lambda *shape: pl.BlockSpec((1,) + shape, lambda i, pt: (i,) + (0,) * len(shape))
    grid_spec = pltpu.PrefetchScalarGridSpec(
        num_scalar_prefetch=1, grid=(bsz,),
        in_specs=page_specs + [per_b(N_KV_HEADS, rows, Q3), per_b(N_KV_HEADS, rows, LANES), per_b(nseg, CMP_LANES),
                               per_b(NEW_PAD, 2 * KV_W), per_b(win_len, 2 * KV_W), per_b(NEW_PAD, 2 * KV_W),
                               pl.BlockSpec((LANES, nseg), lambda i, pt: (0, 0))],
        out_specs=per_b(N_KV_HEADS, rows, HEAD_DIM),
        scratch_shapes=[pltpu.VMEM((N_KV_HEADS, kl, QA), BF16), pltpu.VMEM((N_KV_HEADS, kl, HEAD_DIM), BF16),
                        pltpu.VMEM((N_KV_HEADS, wl, HEAD_DIM), BF16), pltpu.VMEM((N_KV_HEADS, wl, HEAD_DIM), BF16),
                        pltpu.VMEM((N_KV_HEADS, nseg, Q3), BF16), pltpu.VMEM((N_KV_HEADS, rows, QA), BF16)])
    return pl.pallas_call(
        functools.partial(_nsa_s_kernel, n_pages=n_pages, t_new=t_new, win_len=win_len),
        grid_spec=grid_spec,
        out_shape=jax.ShapeDtypeStruct((bsz, N_KV_HEADS, rows, HEAD_DIM), F32),
        compiler_params=_cparams("arbitrary"),
        name="nsa_sample",
    )(page_table, *([cache2] * n_pages), q3, ngr, kcvc, nslc, wst, nwin, oht)


def sample_layer(x2, pe2, lw, tables, cache2, page_table, st_win, st_conv, st_pool, st_ffn, t_new):
    n = x2.shape[0]
    bsz = n // t_new
    past = page_table.shape[1] * PAGE
    (a, zb, qh, ql, kv4, win, _, slc, winb, ng, mg) = inproj(x2, lw['norm_mix'], lw['w_in'], tables, n, 256)
    tm3 = lambda v: v.reshape(t_new, bsz, v.shape[-1])
    bm3 = lambda v: tm3(v).transpose(1, 0, 2)
    c = conv_a_sample(st_conv.transpose(1, 0, 2), tm3(a), lw['conv_dw'], lw['conv_b'], lw['conv_ln_g'], lw['conv_ln_b'])
    pb = pool_b_sample(st_pool.transpose(1, 0, 2), tm3(zb), lw['pool_w'], lw['pool_scale'], past)
    kcvc = compress_sample(cache2, page_table, lw['cmp_s'])

    def heads(v):
        return v.reshape(t_new, bsz, N_KV_HEADS, GROUP, HEAD_DIM).transpose(1, 2, 3, 0, 4).reshape(
            bsz, N_KV_HEADS, GROUP * t_new, HEAD_DIM)
    qh4, ql4 = heads(qh), heads(ql)
    q3 = jnp.concatenate([qh4, ql4, qh4], axis=-1)
    ngr = ng[:, :3 * N_HEADS].reshape(t_new, bsz, N_KV_HEADS, GROUP, 3).transpose(1, 2, 3, 0, 4).reshape(
        bsz, N_KV_HEADS, GROUP * t_new, 3)
    ngr = jnp.pad(ngr, ((0, 0), (0, 0), (0, 0), (0, LANES - 3)))
    pad_new = lambda v: jnp.pad(bm3(v), ((0, 0), (0, NEW_PAD - t_new), (0, 0)))
    wst = st_win.reshape(bsz, st_win.shape[1], 2 * KV_W)
    oc = nsa_sample_attn(cache2, page_table, q3, ngr, kcvc, pad_new(slc), wst, pad_new(winb))
    oc2 = oc.reshape(bsz, N_KV_HEADS, GROUP, t_new, HEAD_DIM).transpose(3, 0, 1, 2, 4).reshape(n, 512).astype(BF16)
    x1 = merge(c.reshape(n, 512), pb.reshape(n, 512), oc2, mg, x2,
               lw['w_a_out'], lw['w_b_out'], lw['w_c_out'], lw['w_o'], tm=256)
    x3, nfv, nfg = ffn_sample(x1, pe2, st_ffn.transpose(1, 0, 2), lw['norm_ffn'], lw['w_up'], lw['ffn_dw'], lw['w_down'],
                              lw['norm_ple'], lw['w_ple_gate'], lw['w_ple'], t_new)
    new_kv = bm3(kv4).reshape(bsz, t_new, 4, N_KV_HEADS, HEAD_DIM)
    new_win = jnp.concatenate([wst[:, t_new:], bm3(win)], axis=1).reshape(bsz, -1, 2, N_KV_HEADS, HEAD_DIM)
    new_conv = jnp.concatenate([st_conv[:, t_new:], bm3(a)], axis=1)
    new_pool = jnp.concatenate([st_pool[:, t_new:], bm3(zb)], axis=1)
    new_ffn = jnp.concatenate([nfv, nfg], axis=-1).transpose(1, 0, 2)
    return x3, (new_conv, new_pool, new_ffn, new_kv, new_win)


def rmsnorm(x, g):
    xf = x.astype(F32)
    y = xf * lax.rsqrt(jnp.mean(xf * xf, axis=-1, keepdims=True) + EPS)
    return (y * g.astype(F32)).astype(x.dtype)


def layernorm(x, g, b):
    xf = x.astype(F32)
    mu = jnp.mean(xf, axis=-1, keepdims=True)
    var = jnp.mean(jnp.square(xf - mu), axis=-1, keepdims=True)
    y = (xf - mu) * lax.rsqrt(var + EPS) * g.astype(F32) + b.astype(F32)
    return y.astype(x.dtype)


def rotary(x, pos):
    half = ROT_DIM // 2
    inv = ROPE_THETA ** (-jnp.arange(half, dtype=F32) * 2.0 / ROT_DIM)
    ang = pos.astype(F32)[:, None] * inv[None, :]
    ang = ang.reshape((ang.shape[0],) + (1,) * (x.ndim - 3) + (half,))
    cos, sin = jnp.cos(ang), jnp.sin(ang)
    x1 = x[..., :half].astype(F32)
    x2 = x[..., half:ROT_DIM].astype(F32)
    rot = jnp.concatenate([x1 * cos - x2 * sin, x2 * cos + x1 * sin], axis=-1).astype(x.dtype)
    return jnp.concatenate([rot, x[..., ROT_DIM:]], axis=-1)


def masked_softmax(s, mask):
    s = jnp.where(mask, s.astype(F32), NEG)
    return jnp.where(mask, jax.nn.softmax(s, axis=-1), 0.0)


def causal_dwconv(u, prev, w):
    K, C = w.shape
    ext = jnp.concatenate([prev, u], axis=1)
    y = lax.conv_general_dilated(ext, w[:, None, :], window_strides=(1,), padding='VALID',
                                 dimension_numbers=('NWC', 'WIO', 'NWC'), feature_group_count=C)
    return y, ext[:, -(K - 1):]


def pool_mix(u, prev, pos0, w_grp, scale):
    B, T, C = u.shape
    P1 = POOL_MAX - 1
    ext = jnp.concatenate([prev, u], axis=1)
    cs = jnp.concatenate([jnp.zeros((B, 1, C), F32), jnp.cumsum(ext.astype(F32), axis=1)], axis=1)
    upto = cs[:, P1 + 1:P1 + 1 + T]
    pos = (pos0 + jnp.arange(T)).astype(F32)
    means = []
    for g, w in enumerate(POOL_WINDOWS):
        ch = slice(g * POOL_GC, (g + 1) * POOL_GC)
        start = cs[:, P1 + 1 - w:P1 + 1 - w + T, ch]
        cnt = jnp.minimum(pos + 1.0, float(w))[None, :, None]
        means.append((upto[..., ch] - start) / cnt)
    pooled = jnp.concatenate(means, axis=-1)
    d = (pooled - u.astype(F32)).astype(u.dtype).reshape(B, T, len(POOL_WINDOWS), POOL_GC)
    y = jnp.concatenate([mm(d[:, :, g], w_grp[g]) for g in range(len(POOL_WINDOWS))], axis=-1) * scale
    return y, ext[:, -P1:]


def compress(k, pe, w1, w2):
    B, L, G, D = k.shape
    nc = (L - CMP_BLOCK) // CMP_STRIDE + 1
    idx = jnp.arange(nc)[:, None] * CMP_STRIDE + jnp.arange(CMP_BLOCK)[None, :]
    blk = k[:, idx] + pe[None, None, :, None, :]
    flat = blk.transpose(0, 1, 3, 2, 4).reshape(B, nc, G, CMP_BLOCK * D)
    out = mm(jax.nn.gelu(mm(flat, w1)), w2)
    end = jnp.arange(nc) * CMP_STRIDE + CMP_BLOCK - 1
    return out, end


def to_blocks(k):
    B, L, G, D = k.shape
    ns = -(-L // SEL_BLOCK)
    k = jnp.pad(k, ((0, 0), (0, ns * SEL_BLOCK - L), (0, 0), (0, 0)))
    return k.reshape(B, ns, SEL_BLOCK, G, D).transpose(0, 3, 1, 2, 4)


def nsa_attend(q, ng, pos_q, kc, vc, cmp_end, ks, vs, kw, vw, pos_w):
    B, Tq, G, R, D = q.shape
    scale = D ** -0.5
    s = jnp.einsum('bqgrd,bngd->bgrqn', q, kc).astype(F32) * scale
    p_c = masked_softmax(s, cmp_end[None, :] <= pos_q[:, None])
    o_c = jnp.einsum('bgrqn,bngd->bqgrd', p_c.astype(vc.dtype), vc)
    nc, ns = kc.shape[1], ks.shape[2]
    owner = (jnp.arange(nc) * CMP_STRIDE) // SEL_BLOCK
    onehot = (owner[:, None] == jnp.arange(ns)[None, :]).astype(F32)
    imp = jnp.einsum('bgrqn,nj->bgqj', p_c, onehot)
    q_blk = pos_q // SEL_BLOCK
    j = jnp.arange(ns)[None, :]
    forced = (j == 0) | (j == q_blk[:, None]) | (j == q_blk[:, None] - 1)
    allowed = j <= q_blk[:, None]
    imp = jnp.where(allowed, jnp.where(forced, FORCE_SCORE, imp), -1.0)
    n_sel = min(N_SELECT, ns)
    vals, idx = lax.top_k(imp, n_sel)
    bi = jnp.arange(B)[:, None, None, None]
    gi = jnp.arange(G)[None, :, None, None]
    kg = ks[bi, gi, idx]
    vg = vs[bi, gi, idx]
    kpos = idx[..., None] * SEL_BLOCK + jnp.arange(SEL_BLOCK)
    m_s = (vals[..., None] >= 0.0) & (kpos <= pos_q[None, None, :, None, None])
    m_s = m_s.reshape(B, G, 1, Tq, n_sel * SEL_BLOCK)
    s = jnp.einsum('bqgrd,bgqkld->bgrqkl', q, kg).reshape(B, G, R, Tq, n_sel * SEL_BLOCK) * scale
    p_s = masked_softmax(s, m_s)
    o_s = jnp.einsum('bgrqm,bgqmd->bqgrd', p_s.astype(vg.dtype), vg.reshape(B, G, Tq, n_sel * SEL_BLOCK, D))
    m_w = (pos_w[None, :] <= pos_q[:, None]) & (pos_w[None, :] > pos_q[:, None] - WINDOW) & (pos_w[None, :] >= 0)
    s = jnp.einsum('bqgrd,blgd->bgrql', q, kw).astype(F32) * scale
    p_w = masked_softmax(s, m_w)
    o_w = jnp.einsum('bgrql,blgd->bqgrd', p_w.astype(vw.dtype), vw)
    return ng[..., 0, None] * o_c + ng[..., 1, None] * o_s + ng[..., 2, None] * o_w


def nsa_sample(q, kv, ng, pe_c, w1, w2, cache_kv, page_table, win_state):
    B, T = q.shape[:2]
    past_len = page_table.shape[1] * cache_kv.shape[1]
    past = cache_kv[page_table].reshape(B, past_len, 4, N_KV_HEADS, HEAD_DIM)
    full = jnp.concatenate([past, kv[:, :, 0:4]], axis=1)
    kc, end = compress(full[:, :, 0], pe_c[0], w1[0], w2[0])
    vc, _ = compress(full[:, :, 1], pe_c[1], w1[1], w2[1])
    ks, vs = to_blocks(full[:, :, 2]), to_blocks(full[:, :, 3])
    wlen = win_state.shape[1]
    win = jnp.concatenate([win_state, kv[:, :, 4:6]], axis=1)
    pos_w = past_len - wlen + jnp.arange(wlen + T)
    pos_q = past_len + jnp.arange(T)
    o = nsa_attend(q, ng, pos_q, kc, vc, end, ks, vs, win[:, :, 0], win[:, :, 1], pos_w)
    return o.reshape(B, T, N_HEADS * HEAD_DIM), kv[:, :, 0:4], win[:, T:]


def trunk_layer(x, pe_i, prev_conv, prev_pool, prev_ffn, pos0, nsa_fn, prm):
    B, T, _ = x.shape
    pos = pos0 + jnp.arange(T)
    h = rmsnorm(x, prm['norm_mix'])
    z = mm(h, prm['w_in'])
    z_a, z_b, z_q, z_kv, z_ng, z_mg = jnp.split(z, SPLIT_AT, axis=-1)
    a = z_a[..., :CONV_WIDTH] * jax.nn.sigmoid(z_a[..., CONV_WIDTH:])
    c, new_conv = causal_dwconv(a, prev_conv, prm['conv_dw'])
    c = jax.nn.silu(layernorm(c + prm['conv_b'], prm['conv_ln_g'], prm['conv_ln_b']))
    out_a = mm(c, prm['w_a_out'])
    pb, new_pool = pool_mix(z_b, prev_pool, pos0, prm['pool_w'], prm['pool_scale'])
    out_b = mm(pb, prm['w_b_out'])
    q = rotary(z_q.reshape(B, T, N_HEADS, HEAD_DIM), pos).reshape(B, T, N_KV_HEADS, GROUP, HEAD_DIM)
    kv = z_kv.reshape(B, T, 6, N_KV_HEADS, HEAD_DIM)
    keys = rotary(kv[:, :, 0::2], pos)
    kv = jnp.stack([keys, kv[:, :, 1::2]], axis=3).reshape(B, T, 6, N_KV_HEADS, HEAD_DIM)
    ng = jax.nn.sigmoid(z_ng.reshape(B, T, N_KV_HEADS, GROUP, 3))
    o_c, new_kv, new_win = nsa_fn(q, kv, ng)
    out_c = mm(o_c, prm['w_c_out'])
    mg = jax.nn.sigmoid(z_mg.reshape(B, T, 3, D_MODEL))
    m = mg[:, :, 0] * out_a + mg[:, :, 1] * out_b + mg[:, :, 2] * out_c
    x = x + mm(m, prm['w_o'])
    u = mm(rmsnorm(x, prm['norm_ffn']), prm['w_up'])
    cu, new_ffn = causal_dwconv(u, prev_ffn, prm['ffn_dw'])
    f = jax.nn.gelu(cu[..., :D_FF]) * cu[..., D_FF:]
    x = x + mm(f, prm['w_down'])
    gate = jax.nn.sigmoid(mm(rmsnorm(x, prm['norm_ple']), prm['w_ple_gate']))
    x = x + gate * mm(pe_i, prm['w_ple'])
    return x, (new_conv, new_pool, new_ffn, new_kv, new_win)


def kernel(x_prompt, x_sample, cache_nsa_kv, state_nsa_win, state_conv, state_pool, state_ffn, page_table,
           p_prompt, p_sample, norm_mix, w_in, conv_dw, conv_b, conv_ln_g, conv_ln_b, w_a_out, pool_w,
           pool_scale, w_b_out, pe_cmp, w_cmp1, w_cmp2, w_c_out, w_o, norm_ffn, w_up, ffn_dw, w_down,
           norm_ple, w_ple_gate, w_ple, norm_final):
    B, T, _ = x_prompt.shape
    bs, ts, _ = x_sample.shape
    n_pool, page = cache_nsa_kv.shape[1:3]
    assert page == PAGE and T % 512 == 0 and T // SEL_BLOCK <= LANES and (bs * ts) % 256 == 0 and bs % SB == 0
    past_len = page_table.shape[1] * page
    tables = _rope_tables(jnp.arange(T))
    tables_s = _rope_tables(past_len + jnp.repeat(jnp.arange(ts), bs))
    xp = x_prompt
    xs = x_sample.transpose(1, 0, 2).reshape(ts * bs, D_MODEL)
    cache_all = cache_nsa_kv.reshape(DEPTH * n_pool, page, 4 * KV_W)
    st_p, st_s = [], []
    for i in range(DEPTH):
        lw = layer_weights(i, norm_mix, w_in, conv_dw, conv_b, conv_ln_g, conv_ln_b, w_a_out, pool_w, pool_scale,
                           w_b_out, pe_cmp, w_cmp1, w_cmp2, w_c_out, w_o, norm_ffn, w_up, ffn_dw, w_down, norm_ple,
                           w_ple_gate, w_ple)
        xp, st = prompt_layer(xp, p_prompt[i], lw, tables)
        st_p.append(st)
        xs, st = sample_layer(xs, p_sample[i].transpose(1, 0, 2).reshape(ts * bs, PLE_DIM), lw, tables_s,
                              cache_all, page_table + i * n_pool, state_nsa_win[i],
                              state_conv[i], state_pool[i], state_ffn[i], ts)
        st_s.append(st)
    y_prompt = final_norm(xp.reshape(B * T, D_MODEL), norm_final).reshape(B, T, D_MODEL)
    y_sample = final_norm(xs, norm_final, tm=256).reshape(ts, bs, D_MODEL).transpose(1, 0, 2)
    stk = lambda lst, k: jnp.stack([s[k] for s in lst])
    return (y_prompt, y_sample, stk(st_p, 3), stk(st_p, 4), stk(st_p, 0), stk(st_p, 1), stk(st_p, 2),
            stk(st_s, 3), stk(st_s, 4), stk(st_s, 0), stk(st_s, 1), stk(st_s, 2))
```

```python
import functools
import numpy as np
import jax
import jax.numpy as jnp
from jax import lax
from jax.experimental import pallas as pl
from jax.experimental.pallas import tpu as pltpu

D_MODEL = 1024
DEPTH = 4
CONV_WIDTH = D_MODEL // 2
POOL_WIDTH = D_MODEL // 2
HEAD_DIM = 64
N_HEADS = (D_MODEL // 2) // HEAD_DIM
N_KV_HEADS = 2
GROUP = N_HEADS // N_KV_HEADS
ROT_DIM = HEAD_DIM // 4
ROPE_THETA = 500000.0
CONV_W = 31
POOL_WINDOWS = (2, 4, 8, 16)
POOL_MAX = max(POOL_WINDOWS)
POOL_GC = POOL_WIDTH // len(POOL_WINDOWS)
CMP_BLOCK = 32
CMP_STRIDE = 16
SEL_BLOCK = 64
N_SELECT = 16
WINDOW = 512
Q_BLOCK = 128
D_FF = ((8 * D_MODEL // 3 + 127) // 128) * 128
FFN_CONV_W = 3
PLE_DIM = 256
EPS = 1e-6
FORCE_SCORE = 1e4
NEG = -1e30

SPLIT_SIZES = (2 * CONV_WIDTH, POOL_WIDTH, N_HEADS * HEAD_DIM, 6 * N_KV_HEADS * HEAD_DIM, 3 * N_HEADS, 3 * D_MODEL)
SPLIT_AT = tuple(int(v) for v in np.cumsum(SPLIT_SIZES)[:-1])

F32 = jnp.float32
BF16 = jnp.bfloat16
LANES = 128
VMEM_LIMIT = 56 * 1024 * 1024
NT = (((1,), (1,)), ((), ()))
LOG2E = 1.4426950408889634

NG_PAD = LANES
C_A, C_B, C_Q, C_KV = 0, 2 * CONV_WIDTH, 2 * CONV_WIDTH + POOL_WIDTH, 2 * CONV_WIDTH + POOL_WIDTH + 512
C_NG = C_KV + 768
C_MG = C_NG + NG_PAD
IN_PACKED = C_MG + 3 * D_MODEL
KV_W = N_KV_HEADS * HEAD_DIM


def _cparams(*sem):
    return pltpu.CompilerParams(dimension_semantics=sem, vmem_limit_bytes=VMEM_LIMIT)


def _const_spec(shape):
    nd = len(shape)
    return pl.BlockSpec(shape, lambda *_: (0,) * nd, pipeline_mode=pl.Buffered(1))


def _rms(x, g):
    return x * lax.rsqrt(jnp.mean(x * x, axis=-1, keepdims=True) + EPS) * g


def _gelu(x):
    return 0.5 * x * (1.0 + jnp.tanh(np.sqrt(2.0 / np.pi).astype(np.float32) * (x + 0.044715 * (x * x * x))))


def _rope_tables(pos):
    half = ROT_DIM // 2
    inv = ROPE_THETA ** (-jnp.arange(half, dtype=F32) * 2.0 / ROT_DIM)
    ang = pos.astype(F32)[:, None] * inv[None, :]
    cos, sin = jnp.cos(ang), jnp.sin(ang)
    rows = pos.shape[0]
    ones = jnp.ones((rows, HEAD_DIM - ROT_DIM), F32)
    zeros = jnp.zeros((rows, HEAD_DIM - ROT_DIM), F32)
    z8 = jnp.zeros((rows, half), F32)
    cos_h = jnp.concatenate([cos, cos, ones], axis=1)
    sup_h = jnp.concatenate([-sin, z8, zeros], axis=1)
    sdn_h = jnp.concatenate([z8, sin, zeros], axis=1)
    rep = LANES // HEAD_DIM
    return jnp.tile(cos_h, (1, rep)), jnp.tile(sup_h, (1, rep)), jnp.tile(sdn_h, (1, rep))


def _rope128(x, cos, sup, sdn):
    return x * cos + pltpu.roll(x, LANES - ROT_DIM // 2, 1) * sup + pltpu.roll(x, ROT_DIM // 2, 1) * sdn


def _inproj_kernel(x_ref, g_ref, w_ref, cos_ref, sup_ref, sdn_ref,
                   a_ref, zb_ref, qh_ref, ql_ref, kv4_ref, win_ref, cmp_ref, slc_ref, winb_ref, ng_ref, mg_ref):
    h = _rms(x_ref[...], g_ref[...]).astype(BF16)

    def proj(c0, n):
        return jnp.dot(h, w_ref[:, c0:c0 + n], preferred_element_type=F32)

    za = proj(C_A, 2 * CONV_WIDTH)
    a_ref[...] = za[:, :CONV_WIDTH] * jax.nn.sigmoid(za[:, CONV_WIDTH:])
    zb_ref[...] = proj(C_B, POOL_WIDTH)
    cos, sup, sdn = cos_ref[...], sup_ref[...], sdn_ref[...]
    zq = proj(C_Q, N_HEADS * HEAD_DIM)
    scale = HEAD_DIM ** -0.5 * LOG2E
    for c in range(N_HEADS * HEAD_DIM // LANES):
        sl = slice(c * LANES, (c + 1) * LANES)
        qr = _rope128(zq[:, sl], cos, sup, sdn) * scale
        qh = qr.astype(BF16)
        qh_ref[:, sl] = qh
        ql_ref[:, sl] = (qr - qh.astype(F32)).astype(BF16)
    zkv = proj(C_KV, 6 * KV_W)
    kind = [zkv[:, j * KV_W:(j + 1) * KV_W] for j in range(6)]
    for j in (0, 2, 4):
        kind[j] = _rope128(kind[j], cos, sup, sdn)
    for j in range(4):
        kv4_ref[:, j * KV_W:(j + 1) * KV_W] = kind[j]
    cmp_ref[:, 0:KV_W] = kind[0]
    cmp_ref[:, KV_W:2 * KV_W] = kind[1]
    slc_ref[:, 0:KV_W] = kind[2].astype(BF16)
    slc_ref[:, KV_W:2 * KV_W] = kind[3].astype(BF16)
    win_ref[:, 0:KV_W] = kind[4]
    win_ref[:, KV_W:2 * KV_W] = kind[5]
    winb_ref[:, 0:KV_W] = kind[4].astype(BF16)
    winb_ref[:, KV_W:2 * KV_W] = kind[5].astype(BF16)
    ng_ref[...] = jax.nn.sigmoid(proj(C_NG, NG_PAD))
    mg_ref[...] = jax.nn.sigmoid(proj(C_MG, 3 * D_MODEL))


def _pack_w_in(w_in):
    parts = jnp.split(w_in, SPLIT_AT, axis=-1)
    ng = jnp.pad(parts[4], ((0, 0), (0, NG_PAD - parts[4].shape[1])))
    return jnp.concatenate([parts[0], parts[1], parts[2], parts[3], ng, parts[5]], axis=1).astype(BF16)


def inproj(x2, g, w_packed, tables, period, tm):
    n = x2.shape[0]
    pblocks = period // tm
    row = lambda c: pl.BlockSpec((tm, c), lambda i: (i, 0))
    tab = pl.BlockSpec((tm, LANES), lambda i: (i % pblocks, 0))
    outs = [(CONV_WIDTH, F32), (POOL_WIDTH, F32), (512, BF16), (512, BF16), (4 * KV_W, F32), (2 * KV_W, F32),
            (2 * KV_W, F32), (2 * KV_W, BF16), (2 * KV_W, BF16), (NG_PAD, F32), (3 * D_MODEL, F32)]
    return pl.pallas_call(
        _inproj_kernel,
        grid=(n // tm,),
        in_specs=[row(D_MODEL), _const_spec((1, D_MODEL)), _const_spec((D_MODEL, IN_PACKED)), tab, tab, tab],
        out_specs=[row(c) for c, _ in outs],
        out_shape=[jax.ShapeDtypeStruct((n, c), dt) for c, dt in outs],
        compiler_params=_cparams("parallel"),
        name="inproj",
    )(x2, g.reshape(1, D_MODEL), w_packed, *tables)


CONV_HALO = 32


def _conva_kernel(a_ref, w_ref, b_ref, g_ref, beta_ref, c_ref, ext_ref, *, tm, rc):
    @pl.when(pl.program_id(1) == 0)
    def _():
        ext_ref[0:CONV_HALO, :] = jnp.zeros((CONV_HALO, CONV_WIDTH), F32)

    ext_ref[CONV_HALO:CONV_HALO + tm, :] = a_ref[0]
    off = CONV_HALO - (CONV_W - 1)
    for r0 in range(0, tm, rc):
        acc = jnp.zeros((rc, CONV_WIDTH), F32)
        for k in range(CONV_W):
            acc = acc + ext_ref[r0 + off + k:r0 + off + k + rc, :] * w_ref[k:k + 1, :]
        y = acc + b_ref[...]
        mu = jnp.mean(y, axis=-1, keepdims=True)
        d = y - mu
        var = jnp.mean(d * d, axis=-1, keepdims=True)
        yn = d * lax.rsqrt(var + EPS) * g_ref[...] + beta_ref[...]
        c_ref[0, r0:r0 + rc, :] = (yn * jax.nn.sigmoid(yn)).astype(BF16)
    ext_ref[0:CONV_HALO, :] = ext_ref[tm:tm + CONV_HALO, :]


def conv_a(a3, conv_dw, conv_b, ln_g, ln_b, tm=256, rc=32):
    b, t, c = a3.shape
    vec = lambda v: v.reshape(1, c)
    return pl.pallas_call(
        functools.partial(_conva_kernel, tm=tm, rc=rc),
        grid=(b, t // tm),
        in_specs=[pl.BlockSpec((1, tm, c), lambda i, j: (i, j, 0)), _const_spec((CONV_W, c)),
                  _const_spec((1, c)), _const_spec((1, c)), _const_spec((1, c))],
        out_specs=pl.BlockSpec((1, tm, c), lambda i, j: (i, j, 0)),
        out_shape=jax.ShapeDtypeStruct((b, t, c), BF16),
        scratch_shapes=[pltpu.VMEM((tm + CONV_HALO, c), F32)],
        compiler_params=_cparams("arbitrary", "arbitrary"),
        name="conv_a",
    )(a3, conv_dw, vec(conv_b), vec(ln_g), vec(ln_b))


POOL_HALO = 16


def _pool_kernel(z_ref, pw_ref, sc_ref, pb_ref, ext_ref, *, tm, rc):
    i = pl.program_id(1)

    @pl.when(i == 0)
    def _():
        ext_ref[0:POOL_HALO, :] = jnp.zeros((POOL_HALO, POOL_WIDTH), F32)

    ext_ref[POOL_HALO:POOL_HALO + tm, :] = z_ref[0]
    for r0 in range(0, tm, rc):
        pos = (i * tm + r0 + lax.broadcasted_iota(jnp.int32, (rc, 1), 0)).astype(F32)
        for gi, w in enumerate(POOL_WINDOWS):
            lanes = slice(gi * POOL_GC, (gi + 1) * POOL_GC)
            base = POOL_HALO + r0
            cur = ext_ref[base:base + rc, lanes]
            s = cur
            for k in range(1, w):
                s = s + ext_ref[base - k:base - k + rc, lanes]
            cnt = jnp.minimum(pos + 1.0, float(w))
            d = s / cnt - cur
            y = jnp.dot(d.astype(BF16), pw_ref[gi], preferred_element_type=F32) * sc_ref[:, lanes]
            pb_ref[0, r0:r0 + rc, lanes] = y.astype(BF16)
    ext_ref[0:POOL_HALO, :] = ext_ref[tm:tm + POOL_HALO, :]


def pool_b(z3, pool_w, pool_scale, tm=512, rc=128):
    b, t, c = z3.shape
    return pl.pallas_call(
        functools.partial(_pool_kernel, tm=tm, rc=rc),
        grid=(b, t // tm),
        in_specs=[pl.BlockSpec((1, tm, c), lambda i, j: (i, j, 0)),
                  _const_spec((len(POOL_WINDOWS), POOL_GC, POOL_GC)), _const_spec((1, c))],
        out_specs=pl.BlockSpec((1, tm, c), lambda i, j: (i, j, 0)),
        out_shape=jax.ShapeDtypeStruct((b, t, c), BF16),
        scratch_shapes=[pltpu.VMEM((tm + POOL_HALO, c), F32)],
        compiler_params=_cparams("arbitrary", "arbitrary"),
        name="pool_b",
    )(z3, pool_w.astype(BF16), pool_scale.reshape(1, c))


SEG = CMP_STRIDE
CMP_LANES = 2 * KV_W


def _compress_weights(pe_c, w1, w2):
    eye = jnp.eye(2, dtype=F32)
    w1r = w1.reshape(2, CMP_BLOCK, HEAD_DIM, HEAD_DIM)
    def expand(w):
        return jnp.einsum('kjde,kl,gh->jkgdlhe', w, eye, eye).reshape(SEG * CMP_LANES, CMP_LANES).astype(BF16)
    wa, wb = expand(w1r[:, :SEG]), expand(w1r[:, SEG:])
    w2bd = jnp.einsum('kde,kl,gh->kgdlhe', w2, eye, eye).reshape(CMP_LANES, CMP_LANES).astype(BF16)
    def pe_row(p):
        return jnp.broadcast_to(p.transpose(1, 0, 2)[:, :, None, :], (SEG, 2, N_KV_HEADS, HEAD_DIM)).reshape(1, SEG * CMP_LANES)
    return wa, wb, w2bd, pe_row(pe_c[:, :SEG]), pe_row(pe_c[:, SEG:])


def _compress_kernel(x_ref, pea_ref, peb_ref, wa_ref, wb_ref, w2_ref, o_ref, bsc_ref, *, nseg):
    x = x_ref[0]
    a = jnp.dot((x + pea_ref[...]).astype(BF16), wa_ref[...], preferred_element_type=F32)
    bsc_ref[0:nseg, :] = jnp.dot((x + peb_ref[...]).astype(BF16), wb_ref[...], preferred_element_type=F32)
    bsc_ref[nseg:nseg + 8, :] = jnp.zeros((8, CMP_LANES), F32)
    u = a + bsc_ref[1:nseg + 1, :]
    o_ref[0] = jnp.dot(_gelu(u).astype(BF16), w2_ref[...], preferred_element_type=F32)


def compress_prompt(cmp3, cw):
    b, t, _ = cmp3.shape
    nseg = t // SEG
    wa, wb, w2bd, pea, peb = cw
    xs = cmp3.reshape(b, nseg, SEG * CMP_LANES)
    return pl.pallas_call(
        functools.partial(_compress_kernel, nseg=nseg),
        grid=(b,),
        in_specs=[pl.BlockSpec((1, nseg, SEG * CMP_LANES), lambda i: (i, 0, 0)),
                  _const_spec((1, SEG * CMP_LANES)), _const_spec((1, SEG * CMP_LANES)),
                  _const_spec((SEG * CMP_LANES, CMP_LANES)), _const_spec((SEG * CMP_LANES, CMP_LANES)),
                  _const_spec((CMP_LANES, CMP_LANES))],
        out_specs=pl.BlockSpec((1, nseg, CMP_LANES), lambda i: (i, 0, 0)),
        out_shape=jax.ShapeDtypeStruct((b, nseg, CMP_LANES), F32),
        scratch_shapes=[pltpu.VMEM((nseg + 8, CMP_LANES), F32)],
        compiler_params=_cparams("arbitrary"),
        name="compress",
    )(xs, pea, peb, wa, wb, w2bd)


QA = LANES + HEAD_DIM
Q3 = 3 * HEAD_DIM


def _softmax_rows(s, mask):
    s = jnp.where(mask, s, NEG)
    m = jnp.max(s, axis=-1, keepdims=True)
    e = jnp.where(mask, jnp.exp2(s - m), 0.0)
    l = jnp.sum(e, axis=-1, keepdims=True)
    return e * (1.0 / jnp.where(l > 0.0, l, 1.0))


KPAD = 2 * LANES
VT_ROWS = HEAD_DIM + 16


def _softmax_cols(s, mask):
    s = jnp.where(mask, s, NEG)
    m = jnp.max(s, axis=0, keepdims=True)
    e = jnp.where(mask, jnp.exp2(s - m), 0.0)
    l = jnp.sum(e, axis=0, keepdims=True)
    return e * (1.0 / jnp.where(l > 0.0, l, 1.0))


def _nsa_t_kernel(qh_ref, ql_ref, k_ref, vt_ref, wk0_ref, wk1_ref, wk2_ref, wv0_ref, wv1_ref, wv2_ref,
                  kc_ref, vct_ref, oht_ref, ng_ref, o_ref,
                  kaug_ref, kc3_ref, q3_ref, qaug_ref, m_ref, vta_ref, acc_ref, sc_ref, oc_ref, ow_ref,
                  *, t_len, tq, kc_len, ncmp):
    i = pl.program_id(1)
    nsel = t_len // SEL_BLOCK
    cols = GROUP * tq
    nseg = kc_ref.shape[1]

    @pl.when(i == 0)
    def _():
        blk = lax.broadcasted_iota(jnp.int32, (t_len, LANES), 0) // SEL_BLOCK
        lane = lax.broadcasted_iota(jnp.int32, (t_len, LANES), 1)
        onehot = jnp.where(blk == lane, 1.0, 0.0).astype(BF16)
        kc_all = kc_ref[0]
        for g in range(N_KV_HEADS):
            kaug_ref[g, :, 0:LANES] = onehot
            kaug_ref[g, :, LANES:QA] = k_ref[0, :, g * HEAD_DIM:(g + 1) * HEAD_DIM]
            kaug_ref[g, :, QA:KPAD] = jnp.zeros((t_len, KPAD - QA), BF16)
            vta_ref[g, 0:HEAD_DIM, :] = vt_ref[0, g * HEAD_DIM:(g + 1) * HEAD_DIM, :]
            one_row = lax.broadcasted_iota(jnp.int32, (VT_ROWS - HEAD_DIM, t_len), 0) == 0
            vta_ref[g, HEAD_DIM:VT_ROWS, :] = jnp.where(one_row, 1.0, 0.0).astype(BF16)
            kc = kc_all[:, g * HEAD_DIM:(g + 1) * HEAD_DIM]
            kch = kc.astype(BF16)
            kc3_ref[g, :, 0:HEAD_DIM] = kch
            kc3_ref[g, :, HEAD_DIM:2 * HEAD_DIM] = kch
            kc3_ref[g, :, 2 * HEAD_DIM:Q3] = (kc - kch.astype(F32)).astype(BF16)
            kc3_ref[g, :, Q3:KPAD] = jnp.zeros((nseg, KPAD - Q3), BF16)
        q3_ref[Q3:KPAD, :] = jnp.zeros((KPAD - Q3, cols), BF16)
        qaug_ref[QA:KPAD, :] = jnp.zeros((KPAD - QA, cols), BF16)

    s0 = i * tq
    pos = s0 + lax.broadcasted_iota(jnp.int32, (1, cols), 1) % tq
    c_last = s0 // kc_len

    for g in range(N_KV_HEADS):
        for r in range(GROUP):
            hs = slice((g * GROUP + r) * HEAD_DIM, (g * GROUP + r + 1) * HEAD_DIM)
            cs = slice(r * tq, (r + 1) * tq)
            qh = qh_ref[0, hs, :]
            q3_ref[0:HEAD_DIM, cs] = qh
            q3_ref[HEAD_DIM:2 * HEAD_DIM, cs] = ql_ref[0, hs, :]
            q3_ref[2 * HEAD_DIM:Q3, cs] = qh
            qaug_ref[LANES:QA, cs] = qh

        s = jnp.dot(kc3_ref[g], q3_ref[...], preferred_element_type=F32)
        n_idx = lax.broadcasted_iota(jnp.int32, (nseg, 1), 0)
        end = jnp.where(n_idx < ncmp, n_idx * CMP_STRIDE + (CMP_BLOCK - 1), t_len + CMP_BLOCK)
        p = _softmax_cols(s, end <= pos)
        vct = vct_ref[0, g * HEAD_DIM:(g + 1) * HEAD_DIM, :].astype(BF16)
        o_c = jnp.dot(vct, p.astype(BF16), preferred_element_type=F32)
        psum = p[:, 0:tq]
        for r in range(1, GROUP):
            psum = psum + p[:, r * tq:(r + 1) * tq]
        p_hi = psum.astype(BF16)
        p_lo = (psum - p_hi.astype(F32)).astype(BF16)
        v = (jnp.dot(oht_ref[...], p_hi, preferred_element_type=F32)
             + jnp.dot(oht_ref[...], p_lo, preferred_element_type=F32))

        jidx = lax.broadcasted_iota(jnp.int32, (LANES, tq), 0)
        qblk = (s0 + lax.broadcasted_iota(jnp.int32, (1, tq), 1)) // SEL_BLOCK
        forced = (jidx == 0) | (jidx == qblk) | (jidx == qblk - 1)
        v = jnp.where(jidx <= qblk, jnp.where(forced, FORCE_SCORE, v), -1.0)
        v = jnp.where(jidx < nsel, v, -3.0)
        sel = jnp.zeros((LANES, tq), F32)
        for _ in range(min(N_SELECT, nsel)):
            mx = jnp.max(v, axis=0, keepdims=True)
            first = jnp.min(jnp.where(v == mx, jidx, LANES), axis=0, keepdims=True)
            pick = jidx == first
            sel = jnp.where(pick, 1.0, sel)
            v = jnp.where(pick, -2.0, v)
        selneg = jnp.where(sel > 0.5, 0.0, NEG).astype(BF16)
        for r in range(GROUP):
            qaug_ref[0:LANES, r * tq:(r + 1) * tq] = selneg

        ks = slice(g * HEAD_DIM, (g + 1) * HEAD_DIM)
        kw = jnp.concatenate([w[0, :, ks] for w in (wk0_ref, wk1_ref, wk2_ref)], axis=0)
        vwt = jnp.concatenate([w[0, ks, :] for w in (wv0_ref, wv1_ref, wv2_ref)], axis=1)
        sw = jnp.dot(kw, q3_ref[0:HEAD_DIM, :], preferred_element_type=F32)
        pos_w = s0 - 2 * tq + lax.broadcasted_iota(jnp.int32, (3 * tq, 1), 0)
        mw = (pos_w <= pos) & (pos_w > pos - WINDOW) & (pos_w >= 0)
        pw = _softmax_cols(sw, mw)
        o_w = jnp.dot(vwt, pw.astype(BF16), preferred_element_type=F32)
        oc_ref[...] = o_c
        ow_ref[...] = o_w

        m_ref[...] = jnp.full((1, cols), NEG, F32)
        acc_ref[...] = jnp.zeros((VT_ROWS, cols), F32)

        def scores(c):
            k0 = pl.multiple_of(c * kc_len, kc_len)
            return jnp.dot(kaug_ref[g, pl.ds(k0, kc_len), :], qaug_ref[...], preferred_element_type=F32)

        def update(c, sc, cmax, causal):
            k0 = pl.multiple_of(c * kc_len, kc_len)
            if causal:
                kpos = k0 + lax.broadcasted_iota(jnp.int32, (kc_len, 1), 0)
                sc = jnp.where(kpos <= pos, sc, NEG)
                cmax = jnp.max(sc, axis=0, keepdims=True)
            m_prev = m_ref[...]
            m_new = jnp.maximum(m_prev, cmax)
            alpha = jnp.exp2(m_prev - m_new)
            pe = jnp.exp2(sc - m_new).astype(BF16)
            vt = vta_ref[g, :, pl.ds(k0, kc_len)]
            acc_ref[...] = alpha * acc_ref[...] + jnp.dot(vt, pe, preferred_element_type=F32)
            m_ref[...] = m_new

        sc_ref[0] = scores(0)

        def body(c, carry):
            slot = c % 2
            sc = sc_ref[slot]
            sc_ref[1 - slot] = scores(c + 1)
            update(c, sc, jnp.max(sc, axis=0, keepdims=True), False)
            return carry

        lax.fori_loop(0, c_last, body, 0)
        update(c_last, sc_ref[c_last % 2], None, True)
        o_s = acc_ref[0:HEAD_DIM, :] * (1.0 / acc_ref[HEAD_DIM:HEAD_DIM + 1, :])

        for r in range(GROUP):
            cs = slice(r * tq, (r + 1) * tq)
            c0 = (g * GROUP + r) * 3
            o = (ng_ref[0, c0:c0 + 1, :] * oc_ref[:, cs] + ng_ref[0, c0 + 1:c0 + 2, :] * o_s[:, cs]
                 + ng_ref[0, c0 + 2:c0 + 3, :] * ow_ref[:, cs])
            o_ref[0, (g * GROUP + r) * HEAD_DIM:(g * GROUP + r + 1) * HEAD_DIM, :] = o.astype(BF16)


def nsa_prompt_t(qh3, ql3, slc3, winb3, kcvc, ng3, tq=256, kc_len=512):
    b, t, _ = qh3.shape
    nseg = t // SEG
    ncmp = (t - CMP_BLOCK) // CMP_STRIDE + 1
    owner = (jnp.arange(nseg) * CMP_STRIDE) // SEL_BLOCK
    oht = (jnp.arange(LANES)[:, None] == owner[None, :]).astype(BF16)
    tr = lambda v: v.transpose(0, 2, 1)
    cols = GROUP * tq
    qspec = pl.BlockSpec((1, 512, tq), lambda i, j: (i, 0, j))
    wk = lambda back: pl.BlockSpec((1, tq, KV_W), lambda i, j: (i, jnp.maximum(j - back, 0), 0))
    wv = lambda back: pl.BlockSpec((1, KV_W, tq), lambda i, j: (i, 1, jnp.maximum(j - back, 0)))
    winbt = tr(winb3)
    out_t = pl.pallas_call(
        functools.partial(_nsa_t_kernel, t_len=t, tq=tq, kc_len=kc_len, ncmp=ncmp),
        grid=(b, t // tq),
        in_specs=[qspec, qspec,
                  pl.BlockSpec((1, t, KV_W), lambda i, j: (i, 0, 0)),
                  pl.BlockSpec((1, KV_W, t), lambda i, j: (i, 1, 0)),
                  wk(2), wk(1), wk(0), wv(2), wv(1), wv(0),
                  pl.BlockSpec((1, nseg, KV_W), lambda i, j: (i, 0, 0)),
                  pl.BlockSpec((1, KV_W, nseg), lambda i, j: (i, 1, 0)),
                  _const_spec((LANES, nseg)),
                  pl.BlockSpec((1, NG_PAD, tq), lambda i, j: (i, 0, j))],
        out_specs=qspec,
        out_shape=jax.ShapeDtypeStruct((b, 512, t), BF16),
        scratch_shapes=[pltpu.VMEM((N_KV_HEADS, t, KPAD), BF16), pltpu.VMEM((N_KV_HEADS, nseg, KPAD), BF16),
                        pltpu.VMEM((KPAD, cols), BF16), pltpu.VMEM((KPAD, cols), BF16),
                        pltpu.VMEM((1, cols), F32), pltpu.VMEM((N_KV_HEADS, VT_ROWS, t), BF16),
                        pltpu.VMEM((VT_ROWS, cols), F32),
                        pltpu.VMEM((2, kc_len, cols), F32), pltpu.VMEM((HEAD_DIM, cols), F32),
                        pltpu.VMEM((HEAD_DIM, cols), F32)],
        compiler_params=_cparams("arbitrary", "arbitrary"),
        name="nsa_prompt",
    )(tr(qh3), tr(ql3), slc3, tr(slc3), winb3, winb3, winb3, winbt, winbt, winbt, kcvc, tr(kcvc), oht, tr(ng3))
    return tr(out_t)


def _merge_kernel(c_ref, pb_ref, oc_ref, mg_ref, x_ref, wa_ref, wb_ref, wc_ref, wo_ref, o_ref):
    out_a = jnp.dot(c_ref[...], wa_ref[...], preferred_element_type=F32)
    out_b = jnp.dot(pb_ref[...], wb_ref[...], preferred_element_type=F32)
    out_c = jnp.dot(oc_ref[...], wc_ref[...], preferred_element_type=F32)
    m = (mg_ref[:, 0:D_MODEL] * out_a + mg_ref[:, D_MODEL:2 * D_MODEL] * out_b
         + mg_ref[:, 2 * D_MODEL:3 * D_MODEL] * out_c)
    o_ref[...] = x_ref[...] + jnp.dot(m.astype(BF16), wo_ref[...], preferred_element_type=F32)


def merge(c2, pb2, oc2, mg2, x2, wa, wb, wc, wo, tm=512):
    n = x2.shape[0]
    row = lambda c: pl.BlockSpec((tm, c), lambda i: (i, 0))
    return pl.pallas_call(
        _merge_kernel,
        grid=(n // tm,),
        in_specs=[row(512), row(512), row(512), row(3 * D_MODEL), row(D_MODEL),
                  _const_spec((512, D_MODEL)), _const_spec((512, D_MODEL)), _const_spec((512, D_MODEL)),
                  _const_spec((D_MODEL, D_MODEL))],
        out_specs=row(D_MODEL),
        out_shape=jax.ShapeDtypeStruct((n, D_MODEL), F32),
        compiler_params=_cparams("parallel"),
        name="merge",
    )(c2, pb2, oc2, mg2, x2, wa, wb, wc, wo)


FF_CHUNK = 256
FF_HALO = 8


def _ffn_kernel(x_ref, pe_ref, gf_ref, wup_ref, dw_ref, wdn_ref, gp_ref, wg_ref, wp_ref, o_ref, nf_ref,
                usc_ref, carry_ref, *, tm):
    @pl.when(pl.program_id(1) == 0)
    def _():
        carry_ref[...] = jnp.zeros((FF_HALO, 2 * D_FF), F32)

    x = x_ref[0]
    h = _rms(x, gf_ref[...]).astype(BF16)
    acc = jnp.zeros((tm, D_MODEL), F32)
    for c in range(D_FF // FF_CHUNK):
        cu = []
        for half in range(2):
            col = half * D_FF + c * FF_CHUNK
            cs = slice(col, col + FF_CHUNK)
            u = jnp.dot(h, wup_ref[:, cs], preferred_element_type=F32)
            usc_ref[half, 0:FF_HALO, :] = carry_ref[:, cs]
            usc_ref[half, FF_HALO:FF_HALO + tm, :] = u
            carry_ref[:, cs] = u[tm - FF_HALO:tm]
            cu.append(usc_ref[half, FF_HALO - 2:FF_HALO - 2 + tm, :] * dw_ref[0:1, cs]
                      + usc_ref[half, FF_HALO - 1:FF_HALO - 1 + tm, :] * dw_ref[1:2, cs]
                      + u * dw_ref[2:3, cs])
        f = _gelu(cu[0]) * cu[1]
        acc = acc + jnp.dot(f.astype(BF16), wdn_ref[c * FF_CHUNK:(c + 1) * FF_CHUNK, :], preferred_element_type=F32)
    nf_ref[0] = carry_ref[FF_HALO - 2:FF_HALO, :]
    x2 = x + acc
    gate = jax.nn.sigmoid(jnp.dot(_rms(x2, gp_ref[...]).astype(BF16), wg_ref[...], preferred_element_type=F32))
    o_ref[0] = x2 + gate * jnp.dot(pe_ref[0].astype(BF16), wp_ref[...], preferred_element_type=F32)


def ffn_prompt(x3, pe3, g_ffn, w_up, ffn_dw, w_down, g_ple, w_gate, w_ple, tm=512):
    b, t, _ = x3.shape
    tile = lambda c: pl.BlockSpec((1, tm, c), lambda i, j: (i, j, 0))
    return pl.pallas_call(
        functools.partial(_ffn_kernel, tm=tm),
        grid=(b, t // tm),
        in_specs=[tile(D_MODEL), tile(PLE_DIM), _const_spec((1, D_MODEL)), _const_spec((D_MODEL, 2 * D_FF)),
                  _const_spec((FFN_CONV_W, 2 * D_FF)), _const_spec((D_FF, D_MODEL)), _const_spec((1, D_MODEL)),
                  _const_spec((D_MODEL, D_MODEL)), _const_spec((PLE_DIM, D_MODEL))],
        out_specs=[tile(D_MODEL), pl.BlockSpec((1, FFN_CONV_W - 1, 2 * D_FF), lambda i, j: (i, 0, 0))],
        out_shape=[jax.ShapeDtypeStruct((b, t, D_MODEL), F32),
                   jax.ShapeDtypeStruct((b, FFN_CONV_W - 1, 2 * D_FF), F32)],
        scratch_shapes=[pltpu.VMEM((2, tm + FF_HALO, FF_CHUNK), F32), pltpu.VMEM((FF_HALO, 2 * D_FF), F32)],
        compiler_params=_cparams("arbitrary", "arbitrary"),
        name="ffn",
    )(x3, pe3, g_ffn.reshape(1, D_MODEL), w_up, ffn_dw, w_down, g_ple.reshape(1, D_MODEL), w_gate, w_ple)


def _norm_kernel(x_ref, g_ref, o_ref):
    o_ref[...] = _rms(x_ref[...], g_ref[...])


def final_norm(x2, g, tm=512):
    n = x2.shape[0]
    return pl.pallas_call(
        _norm_kernel,
        grid=(n // tm,),
        in_specs=[pl.BlockSpec((tm, D_MODEL), lambda i: (i, 0)), _const_spec((1, D_MODEL))],
        out_specs=pl.BlockSpec((tm, D_MODEL), lambda i: (i, 0)),
        out_shape=jax.ShapeDtypeStruct((n, D_MODEL), F32),
        compiler_params=_cparams("parallel"),
        name="final_norm",
    )(x2, g.reshape(1, D_MODEL))


def prompt_layer(x3, pe3, lw, tables):
    b, t, _ = x3.shape
    n = b * t
    (a, zb, qh, ql, kv4, win, cmpx, slc, winb, ng, mg) = inproj(
        x3.reshape(n, D_MODEL), lw['norm_mix'], lw['w_in'], tables, t, 512)
    r3 = lambda v: v.reshape(b, t, v.shape[-1])
    a3 = r3(a)
    zb3 = r3(zb)
    c = conv_a(a3, lw['conv_dw'], lw['conv_b'], lw['conv_ln_g'], lw['conv_ln_b'])
    pb = pool_b(zb3, lw['pool_w'], lw['pool_scale'])
    kcvc = compress_prompt(r3(cmpx), lw['cmp'])
    oc = nsa_prompt_t(r3(qh), r3(ql), r3(slc), r3(winb), kcvc, r3(ng))
    x1 = merge(c.reshape(n, 512), pb.reshape(n, 512), oc.reshape(n, 512), mg, x3.reshape(n, D_MODEL),
               lw['w_a_out'], lw['w_b_out'], lw['w_c_out'], lw['w_o'])
    x2, new_ffn = ffn_prompt(x1.reshape(b, t, D_MODEL), pe3, lw['norm_ffn'], lw['w_up'], lw['ffn_dw'], lw['w_down'],
                             lw['norm_ple'], lw['w_ple_gate'], lw['w_ple'])
    wp = min(WINDOW, t)
    new_kv = kv4.reshape(b, t, 4, N_KV_HEADS, HEAD_DIM)
    new_win = r3(win)[:, t - wp:].reshape(b, wp, 2, N_KV_HEADS, HEAD_DIM)
    new_conv = a3[:, t - (CONV_W - 1):]
    new_pool = zb3[:, t - (POOL_MAX - 1):]
    return x2, (new_conv, new_pool, new_ffn, new_kv, new_win)


def layer_weights(i, norm_mix, w_in, conv_dw, conv_b, conv_ln_g, conv_ln_b, w_a_out, pool_w, pool_scale, w_b_out,
                  pe_cmp, w_cmp1, w_cmp2, w_c_out, w_o, norm_ffn, w_up, ffn_dw, w_down, norm_ple, w_ple_gate, w_ple):
    return dict(norm_mix=norm_mix[i], w_in=_pack_w_in(w_in[i]), conv_dw=conv_dw[i], conv_b=conv_b[i],
                conv_ln_g=conv_ln_g[i], conv_ln_b=conv_ln_b[i], w_a_out=w_a_out[i].astype(BF16), pool_w=pool_w[i],
                pool_scale=pool_scale[i], w_b_out=w_b_out[i].astype(BF16),
                cmp=_compress_weights(pe_cmp[i], w_cmp1[i], w_cmp2[i]),
                cmp_s=_compress_weights_sample(pe_cmp[i], w_cmp1[i], w_cmp2[i]), w_c_out=w_c_out[i].astype(BF16),
                w_o=w_o[i].astype(BF16), norm_ffn=norm_ffn[i], w_up=w_up[i].astype(BF16), ffn_dw=ffn_dw[i],
                w_down=w_down[i].astype(BF16), norm_ple=norm_ple[i], w_ple_gate=w_ple_gate[i].astype(BF16),
                w_ple=w_ple[i].astype(BF16))


SB = 32


def _conva_s_kernel(st_ref, a_ref, w_ref, b_ref, g_ref, beta_ref, c_ref, *, t_new, bsz):
    hist = CONV_W - 1
    for t in range(t_new):
        for b0 in range(0, bsz, SB):
            acc = jnp.zeros((SB, CONV_WIDTH), F32)
            for k in range(CONV_W):
                j = t + k
                row = st_ref[j, b0:b0 + SB, :] if j < hist else a_ref[j - hist, b0:b0 + SB, :]
                acc = acc + row * w_ref[k:k + 1, :]
            y = acc + b_ref[...]
            mu = jnp.mean(y, axis=-1, keepdims=True)
            d = y - mu
            var = jnp.mean(d * d, axis=-1, keepdims=True)
            yn = d * lax.rsqrt(var + EPS) * g_ref[...] + beta_ref[...]
            c_ref[t, b0:b0 + SB, :] = (yn * jax.nn.sigmoid(yn)).astype(BF16)


def conv_a_sample(st_tm, a_tm, conv_dw, conv_b, ln_g, ln_b):
    t_new, bsz, c = a_tm.shape
    vec = lambda v: v.reshape(1, c)
    return pl.pallas_call(
        functools.partial(_conva_s_kernel, t_new=t_new, bsz=bsz),
        out_shape=jax.ShapeDtypeStruct((t_new, bsz, c), BF16),
        compiler_params=pltpu.CompilerParams(vmem_limit_bytes=VMEM_LIMIT),
        name="conv_a_sample",
    )(st_tm, a_tm, conv_dw, vec(conv_b), vec(ln_g), vec(ln_b))


def _pool_s_kernel(st_ref, z_ref, pw_ref, sc_ref, pb_ref, *, t_new, pos0):
    hist = POOL_MAX - 1
    for t in range(t_new):
        for gi, w in enumerate(POOL_WINDOWS):
            lanes = slice(gi * POOL_GC, (gi + 1) * POOL_GC)
            cur = z_ref[t, :, lanes]
            s = cur
            for k in range(1, w):
                j = hist + t - k
                s = s + (st_ref[j, :, lanes] if j < hist else z_ref[j - hist, :, lanes])
            cnt = float(min(pos0 + t + 1, w))
            d = s / cnt - cur
            y = jnp.dot(d.astype(BF16), pw_ref[gi], preferred_element_type=F32) * sc_ref[:, lanes]
            pb_ref[t, :, lanes] = y.astype(BF16)


def pool_b_sample(st_tm, z_tm, pool_w, pool_scale, pos0):
    t_new, bsz, c = z_tm.shape
    return pl.pallas_call(
        functools.partial(_pool_s_kernel, t_new=t_new, pos0=pos0),
        out_shape=jax.ShapeDtypeStruct((t_new, bsz, c), BF16),
        compiler_params=pltpu.CompilerParams(vmem_limit_bytes=VMEM_LIMIT),
        name="pool_b_sample",
    )(st_tm, z_tm, pool_w.astype(BF16), pool_scale.reshape(1, c))


def _ffn_s_kernel(x_ref, pe_ref, gf_ref, wv_ref, wg_ref, dwv_ref, dwg_ref, wdn_ref, stv_ref, stg_ref,
                  gp_ref, wgate_ref, wp_ref, o_ref, nfv_ref, nfg_ref, h_ref, acc_ref, *, t_new, bsz):
    c = pl.program_id(0)

    @pl.when(c == 0)
    def _():
        h_ref[...] = _rms(x_ref[...], gf_ref[...]).astype(BF16)
        acc_ref[...] = jnp.zeros(acc_ref.shape, F32)

    def conv_half(w_ref, dw_ref, st_ref, nf_ref):
        u = jnp.dot(h_ref[...], w_ref[...], preferred_element_type=F32)
        ext = [st_ref[0], st_ref[1]] + [u[t * bsz:(t + 1) * bsz] for t in range(t_new)]
        nf_ref[0] = ext[t_new]
        nf_ref[1] = ext[t_new + 1]
        return jnp.concatenate(
            [ext[t] * dw_ref[0:1, :] + ext[t + 1] * dw_ref[1:2, :] + ext[t + 2] * dw_ref[2:3, :] for t in range(t_new)],
            axis=0)

    cv = conv_half(wv_ref, dwv_ref, stv_ref, nfv_ref)
    cg = conv_half(wg_ref, dwg_ref, stg_ref, nfg_ref)
    f = _gelu(cv) * cg
    acc_ref[...] += jnp.dot(f.astype(BF16), wdn_ref[...], preferred_element_type=F32)

    @pl.when(c == pl.num_programs(0) - 1)
    def _():
        x2 = x_ref[...] + acc_ref[...]
        gate = jax.nn.sigmoid(jnp.dot(_rms(x2, gp_ref[...]).astype(BF16), wgate_ref[...], preferred_element_type=F32))
        o_ref[...] = x2 + gate * jnp.dot(pe_ref[...].astype(BF16), wp_ref[...], preferred_element_type=F32)


def ffn_sample(x2, pe2, st_tm, g_ffn, w_up, ffn_dw, w_down, g_ple, w_gate, w_ple, t_new):
    n = x2.shape[0]
    bsz = n // t_new
    nch = D_FF // FF_CHUNK
    full = lambda r, c: pl.BlockSpec((r, c), lambda i: (0, 0))
    return pl.pallas_call(
        functools.partial(_ffn_s_kernel, t_new=t_new, bsz=bsz),
        grid=(nch,),
        in_specs=[full(n, D_MODEL), full(n, PLE_DIM), full(1, D_MODEL),
                  pl.BlockSpec((D_MODEL, FF_CHUNK), lambda i: (0, i)),
                  pl.BlockSpec((D_MODEL, FF_CHUNK), lambda i: (0, nch + i)),
                  pl.BlockSpec((FFN_CONV_W, FF_CHUNK), lambda i: (0, i)),
                  pl.BlockSpec((FFN_CONV_W, FF_CHUNK), lambda i: (0, nch + i)),
                  pl.BlockSpec((FF_CHUNK, D_MODEL), lambda i: (i, 0)),
                  pl.BlockSpec((FFN_CONV_W - 1, bsz, FF_CHUNK), lambda i: (0, 0, i)),
                  pl.BlockSpec((FFN_CONV_W - 1, bsz, FF_CHUNK), lambda i: (0, 0, nch + i)),
                  full(1, D_MODEL), full(D_MODEL, D_MODEL), full(PLE_DIM, D_MODEL)],
        out_specs=[full(n, D_MODEL),
                   pl.BlockSpec((FFN_CONV_W - 1, bsz, FF_CHUNK), lambda i: (0, 0, i)),
                   pl.BlockSpec((FFN_CONV_W - 1, bsz, FF_CHUNK), lambda i: (0, 0, i))],
        out_shape=[jax.ShapeDtypeStruct((n, D_MODEL), F32),
                   jax.ShapeDtypeStruct((FFN_CONV_W - 1, bsz, D_FF), F32),
                   jax.ShapeDtypeStruct((FFN_CONV_W - 1, bsz, D_FF), F32)],
        scratch_shapes=[pltpu.VMEM((n, D_MODEL), BF16), pltpu.VMEM((n, D_MODEL), F32)],
        compiler_params=_cparams("arbitrary"),
        name="ffn_sample",
    )(x2, pe2, g_ffn.reshape(1, D_MODEL), w_up, w_up, ffn_dw, ffn_dw, w_down, st_tm, st_tm,
      g_ple.reshape(1, D_MODEL), w_gate, w_ple)


PAGE = 128
SEG_PER_PAGE = PAGE // SEG


def _compress_weights_sample(pe_c, w1, w2):
    eye = jnp.eye(2, dtype=F32)
    w1r = w1.reshape(2, CMP_BLOCK, HEAD_DIM, HEAD_DIM)
    def expand(w):
        return jnp.einsum('kjde,kl,gh->jkgdlhe', w, eye, eye).reshape(SEG, CMP_LANES, CMP_LANES).astype(BF16)
    w2bd = jnp.einsum('kde,kl,gh->kgdlhe', w2, eye, eye).reshape(CMP_LANES, CMP_LANES).astype(BF16)
    def pe_rows(p):
        return jnp.broadcast_to(p.transpose(1, 0, 2)[:, :, None, :], (SEG, 2, N_KV_HEADS, HEAD_DIM)).reshape(SEG, CMP_LANES)
    return expand(w1r[:, :SEG]), expand(w1r[:, SEG:]), w2bd, pe_rows(pe_c[:, :SEG]), pe_rows(pe_c[:, SEG:])


def _compress_s_kernel(pt_ref, *refs, n_pages, bt):
    del pt_ref
    np_all = bt * n_pages
    pages_k, pages_v = refs[:np_all], refs[np_all:2 * np_all]
    pea_ref, peb_ref, wa_ref, wb_ref, w2_ref, o_ref, bsc_ref = refs[2 * np_all:]
    m = bt * n_pages * SEG_PER_PAGE
    a = jnp.zeros((m, CMP_LANES), F32)
    bm = jnp.zeros((m, CMP_LANES), F32)
    for j in range(SEG):
        xj = jnp.concatenate(
            [jnp.concatenate([pg[pl.ds(j, SEG_PER_PAGE, stride=SEG), :] for pg in pgs], axis=0)
             for pgs in (pages_k, pages_v)], axis=1)
        a = a + jnp.dot((xj + pea_ref[j:j + 1, :]).astype(BF16), wa_ref[j], preferred_element_type=F32)
        bm = bm + jnp.dot((xj + peb_ref[j:j + 1, :]).astype(BF16), wb_ref[j], preferred_element_type=F32)
    bsc_ref[0:m, :] = bm
    bsc_ref[m:m + 8, :] = jnp.zeros((8, CMP_LANES), F32)
    u = a + bsc_ref[1:m + 1, :]
    out = jnp.dot(_gelu(u).astype(BF16), w2_ref[...], preferred_element_type=F32)
    nseg = n_pages * SEG_PER_PAGE
    for q in range(bt):
        o_ref[q] = out[q * nseg:(q + 1) * nseg]


def compress_sample(cache2, page_table, cw, bt=4):
    bsz, n_pages = page_table.shape
    nseg = n_pages * SEG_PER_PAGE
    wa, wb, w2bd, pea, peb = cw
    page_specs = [pl.BlockSpec((None, PAGE, KV_W),
                               functools.partial(lambda i, pt, q, p, kind: (pt[i * bt + q, p], 0, kind), q=q, p=p, kind=kind))
                  for kind in range(2) for q in range(bt) for p in range(n_pages)]
    const = lambda shape: pl.BlockSpec(shape, lambda i, pt: (0,) * len(shape))
    grid_spec = pltpu.PrefetchScalarGridSpec(
        num_scalar_prefetch=1, grid=(bsz // bt,),
        in_specs=page_specs + [const((SEG, CMP_LANES)), const((SEG, CMP_LANES)), const((SEG, CMP_LANES, CMP_LANES)),
                               const((SEG, CMP_LANES, CMP_LANES)), const((CMP_LANES, CMP_LANES))],
        out_specs=pl.BlockSpec((bt, nseg, CMP_LANES), lambda i, pt: (i, 0, 0)),
        scratch_shapes=[pltpu.VMEM((bt * nseg + 8, CMP_LANES), F32)])
    return pl.pallas_call(
        functools.partial(_compress_s_kernel, n_pages=n_pages, bt=bt),
        grid_spec=grid_spec,
        out_shape=jax.ShapeDtypeStruct((bsz, nseg, CMP_LANES), F32),
        compiler_params=_cparams("arbitrary"),
        name="compress_sample",
    )(page_table, *([cache2] * (2 * bt * n_pages)), pea, peb, wa, wb, w2bd)


NEW_PAD = 16


def _nsa_s_kernel(pt_ref, *refs, n_pages, t_new, win_len):
    del pt_ref
    pages = refs[:n_pages]
    (q3_ref, ngr_ref, kcvc_ref, nslc_ref, wst_ref, nwin_ref, oht_ref, o_ref,
     kaug_ref, vsel_ref, kw_ref, vw_ref, kc3_ref, qaug_ref) = refs[n_pages:]
    past = n_pages * PAGE
    kl = kaug_ref.shape[1]
    wl = kw_ref.shape[1]
    rows = GROUP * t_new
    nseg = kcvc_ref.shape[1]
    ncmp = (past + t_new - CMP_BLOCK) // CMP_STRIDE + 1
    nsel = -(-(past + t_new) // SEL_BLOCK)

    @pl.when(pl.program_id(0) == 0)
    def _():
        blk = lax.broadcasted_iota(jnp.int32, (kl, LANES), 0) // SEL_BLOCK
        lane = lax.broadcasted_iota(jnp.int32, (kl, LANES), 1)
        onehot = jnp.where(blk == lane, 1.0, 0.0).astype(BF16)
        for g in range(N_KV_HEADS):
            kaug_ref[g, :, 0:LANES] = onehot
            kaug_ref[g, past:kl, LANES:QA] = jnp.zeros((kl - past, HEAD_DIM), BF16)
            vsel_ref[g, past:kl, :] = jnp.zeros((kl - past, HEAD_DIM), BF16)
            kw_ref[g, win_len:wl, :] = jnp.zeros((wl - win_len, HEAD_DIM), BF16)
            vw_ref[g, win_len:wl, :] = jnp.zeros((wl - win_len, HEAD_DIM), BF16)

    pos4 = past + lax.broadcasted_iota(jnp.int32, (rows, 1), 0) % t_new
    kcvc = kcvc_ref[0]
    nslc = nslc_ref[0]
    nwin = nwin_ref[0]
    wst = wst_ref[0]
    outs = []
    for g in range(N_KV_HEADS):
        ks = slice(g * HEAD_DIM, (g + 1) * HEAD_DIM)
        vs = slice(KV_W + g * HEAD_DIM, KV_W + (g + 1) * HEAD_DIM)
        for p in range(n_pages):
            kaug_ref[g, p * PAGE:(p + 1) * PAGE, LANES:QA] = pages[p][:, ks].astype(BF16)
            vsel_ref[g, p * PAGE:(p + 1) * PAGE, :] = pages[p][:, vs].astype(BF16)
        kaug_ref[g, past:past + NEW_PAD, LANES:QA] = nslc[:, ks]
        vsel_ref[g, past:past + NEW_PAD, :] = nslc[:, vs]
        kw_ref[g, 0:win_len, :] = wst[:, ks].astype(BF16)
        vw_ref[g, 0:win_len, :] = wst[:, vs].astype(BF16)
        kw_ref[g, win_len:win_len + NEW_PAD, :] = nwin[:, ks]
        vw_ref[g, win_len:win_len + NEW_PAD, :] = nwin[:, vs]
        kc = kcvc[:, ks]
        kch = kc.astype(BF16)
        kc3_ref[g, :, 0:HEAD_DIM] = kch
        kc3_ref[g, :, HEAD_DIM:2 * HEAD_DIM] = kch
        kc3_ref[g, :, 2 * HEAD_DIM:Q3] = (kc - kch.astype(F32)).astype(BF16)
        vc = kcvc[:, vs].astype(BF16)
        q3 = q3_ref[0, g]

        s = lax.dot_general(q3, kc3_ref[g], NT, preferred_element_type=F32)
        n_idx = lax.broadcasted_iota(jnp.int32, (1, nseg), 1)
        end = n_idx * CMP_STRIDE + (CMP_BLOCK - 1)
        mask = jnp.where(n_idx < ncmp, end, past + t_new + CMP_BLOCK) <= pos4
        p = _softmax_rows(s, mask)
        o_c = jnp.dot(p.astype(BF16), vc, preferred_element_type=F32)
        ppad = jnp.concatenate([p, jnp.zeros((LANES - rows, nseg), F32)], axis=0)
        p_hi = ppad.astype(BF16)
        p_lo = (ppad - p_hi.astype(F32)).astype(BF16)
        x = (lax.dot_general(oht_ref[...], p_hi, NT, preferred_element_type=F32)
             + lax.dot_general(oht_ref[...], p_lo, NT, preferred_element_type=F32))
        v = x
        for r in range(1, GROUP):
            v = v + pltpu.roll(x, LANES - r * t_new, 1)

        jidx = lax.broadcasted_iota(jnp.int32, (LANES, LANES), 0)
        qblk = (past + lax.broadcasted_iota(jnp.int32, (1, LANES), 1) % t_new) // SEL_BLOCK
        forced = (jidx == 0) | (jidx == qblk) | (jidx == qblk - 1)
        v = jnp.where(jidx <= qblk, jnp.where(forced, FORCE_SCORE, v), -1.0)
        v = jnp.where(jidx < nsel, v, -3.0)
        sel = jnp.zeros((LANES, LANES), F32)
        for _ in range(min(N_SELECT, nsel)):
            mx = jnp.max(v, axis=0, keepdims=True)
            first = jnp.min(jnp.where(v == mx, jidx, LANES), axis=0, keepdims=True)
            pick = jidx == first
            sel = jnp.where(pick, 1.0, sel)
            v = jnp.where(pick, -2.0, v)
        ri = lax.broadcasted_iota(jnp.int32, (rows, LANES), 0) % t_new
        li = lax.broadcasted_iota(jnp.int32, (rows, LANES), 1)
        spread = jnp.where(ri == li, 1.0, 0.0).astype(BF16)
        sel_rows = lax.dot_general(spread, sel.astype(BF16), NT, preferred_element_type=F32)
        qaug_ref[g, :, 0:LANES] = jnp.where(sel_rows > 0.5, 0.0, NEG).astype(BF16)
        qaug_ref[g, :, LANES:QA] = q3[:, 0:HEAD_DIM]

        sc = lax.dot_general(qaug_ref[g], kaug_ref[g], NT, preferred_element_type=F32)
        kpos = lax.broadcasted_iota(jnp.int32, (1, kl), 1)
        ps = _softmax_rows(sc, kpos <= pos4)
        o_s = jnp.dot(ps.astype(BF16), vsel_ref[g], preferred_element_type=F32)

        sw = lax.dot_general(q3[:, 0:HEAD_DIM], kw_ref[g], NT, preferred_element_type=F32)
        widx = lax.broadcasted_iota(jnp.int32, (1, wl), 1)
        pos_w = past - win_len + widx
        mw = (widx < win_len + t_new) & (pos_w <= pos4) & (pos_w > pos4 - WINDOW) & (pos_w >= 0)
        pw = _softmax_rows(sw, mw)
        o_w = jnp.dot(pw.astype(BF16), vw_ref[g], preferred_element_type=F32)

        ngr = ngr_ref[0, g]
        outs.append(ngr[:, 0:1] * o_c + ngr[:, 1:2] * o_s + ngr[:, 2:3] * o_w)
    o_ref[0] = jnp.stack(outs, axis=0)


def nsa_sample_attn(cache2, page_table, q3, ngr, kcvc, nslc, wst, nwin):
    bsz, n_pages = page_table.shape
    t_new = q3.shape[2] // GROUP
    rows = GROUP * t_new
    past = n_pages * PAGE
    win_len = wst.shape[1]
    nseg = kcvc.shape[1]
    kl = past + LANES
    wl = win_len + LANES
    owner = (jnp.arange(nseg) * CMP_STRIDE) // SEL_BLOCK
    oht = (jnp.arange(LANES)[:, None] == owner[None, :]).astype(BF16)
    page_specs = [pl.BlockSpec((None, PAGE, 2 * KV_W), functools.partial(lambda i, pt, p: (pt[i, p], 0, 1), p=p))
                  for p in range(n_pages)]
    per_b = lambda *shape: pl.BlockSpec((1,) + shape, lambda i, pt: (i,) + (0,) * len(shape))
    grid_spec = pltpu.PrefetchScalarGridSpec(
        num_scalar_prefetch=1, grid=(bsz,),
        in_specs=page_specs + [per_b(N_KV_HEADS, rows, Q3), per_b(N_KV_HEADS, rows, LANES), per_b(nseg, CMP_LANES),
                               per_b(NEW_PAD, 2 * KV_W), per_b(win_len, 2 * KV_W), per_b(NEW_PAD, 2 * KV_W),
                               pl.BlockSpec((LANES, nseg), lambda i, pt: (0, 0))],
        out_specs=per_b(N_KV_HEADS, rows, HEAD_DIM),
        scratch_shapes=[pltpu.VMEM((N_KV_HEADS, kl, QA), BF16), pltpu.VMEM((N_KV_HEADS, kl, HEAD_DIM), BF16),
                        pltpu.VMEM((N_KV_HEADS, wl, HEAD_DIM), BF16), pltpu.VMEM((N_KV_HEADS, wl, HEAD_DIM), BF16),
                        pltpu.VMEM((N_KV_HEADS, nseg, Q3), BF16), pltpu.VMEM((N_KV_HEADS, rows, QA), BF16)])
    return pl.pallas_call(
        functools.partial(_nsa_s_kernel, n_pages=n_pages, t_new=t_new, win_len=win_len),
        grid_spec=grid_spec,
        out_shape=jax.ShapeDtypeStruct((bsz, N_KV_HEADS, rows, HEAD_DIM), F32),
        compiler_params=_cparams("arbitrary"),
        name="nsa_sample",
    )(page_table, *([cache2] * n_pages), q3, ngr, kcvc, nslc, wst, nwin, oht)


def sample_layer(x2, pe2, lw, tables, cache2, page_table, st_win, st_conv, st_pool, st_ffn, t_new):
    n = x2.shape[0]
    bsz = n // t_new
    past = page_table.shape[1] * PAGE
    (a, zb, qh, ql, kv4, win, _, slc, winb, ng, mg) = inproj(x2, lw['norm_mix'], lw['w_in'], tables, n, 256)
    tm3 = lambda v: v.reshape(t_new, bsz, v.shape[-1])
    bm3 = lambda v: tm3(v).transpose(1, 0, 2)
    c = conv_a_sample(st_conv.transpose(1, 0, 2), tm3(a), lw['conv_dw'], lw['conv_b'], lw['conv_ln_g'], lw['conv_ln_b'])
    pb = pool_b_sample(st_pool.transpose(1, 0, 2), tm3(zb), lw['pool_w'], lw['pool_scale'], past)
    kcvc = compress_sample(cache2, page_table, lw['cmp_s'])

    def heads(v):
        return v.reshape(t_new, bsz, N_KV_HEADS, GROUP, HEAD_DIM).transpose(1, 2, 3, 0, 4).reshape(
            bsz, N_KV_HEADS, GROUP * t_new, HEAD_DIM)
    qh4, ql4 = heads(qh), heads(ql)
    q3 = jnp.concatenate([qh4, ql4, qh4], axis=-1)
    ngr = ng[:, :3 * N_HEADS].reshape(t_new, bsz, N_KV_HEADS, GROUP, 3).transpose(1, 2, 3, 0, 4).reshape(
        bsz, N_KV_HEADS, GROUP * t_new, 3)
    ngr = jnp.pad(ngr, ((0, 0), (0, 0), (0, 0), (0, LANES - 3)))
    pad_new = lambda v: jnp.pad(bm3(v), ((0, 0), (0, NEW_PAD - t_new), (0, 0)))
    wst = st_win.reshape(bsz, st_win.shape[1], 2 * KV_W)
    oc = nsa_sample_attn(cache2, page_table, q3, ngr, kcvc, pad_new(slc), wst, pad_new(winb))
    oc2 = oc.reshape(bsz, N_KV_HEADS, GROUP, t_new, HEAD_DIM).transpose(3, 0, 1, 2, 4).reshape(n, 512).astype(BF16)
    x1 = merge(c.reshape(n, 512), pb.reshape(n, 512), oc2, mg, x2,
               lw['w_a_out'], lw['w_b_out'], lw['w_c_out'], lw['w_o'], tm=256)
    x3, nfv, nfg = ffn_sample(x1, pe2, st_ffn.transpose(1, 0, 2), lw['norm_ffn'], lw['w_up'], lw['ffn_dw'], lw['w_down'],
                              lw['norm_ple'], lw['w_ple_gate'], lw['w_ple'], t_new)
    new_kv = bm3(kv4).reshape(bsz, t_new, 4, N_KV_HEADS, HEAD_DIM)
    new_ffn = jnp.concatenate([nfv, nfg], axis=-1).transpose(1, 0, 2)
    return x3, (bm3(a), bm3(zb), new_ffn, new_kv, bm3(win))


def kernel(x_prompt, x_sample, cache_nsa_kv, state_nsa_win, state_conv, state_pool, state_ffn, page_table,
           p_prompt, p_sample, norm_mix, w_in, conv_dw, conv_b, conv_ln_g, conv_ln_b, w_a_out, pool_w,
           pool_scale, w_b_out, pe_cmp, w_cmp1, w_cmp2, w_c_out, w_o, norm_ffn, w_up, ffn_dw, w_down,
           norm_ple, w_ple_gate, w_ple, norm_final):
    B, T, _ = x_prompt.shape
    bs, ts, _ = x_sample.shape
    n_pool, page = cache_nsa_kv.shape[1:3]
    assert page == PAGE and T % 512 == 0 and T // SEL_BLOCK <= LANES and (bs * ts) % 256 == 0 and bs % SB == 0
    past_len = page_table.shape[1] * page
    tables = _rope_tables(jnp.arange(T))
    tables_s = _rope_tables(past_len + jnp.repeat(jnp.arange(ts), bs))
    xp = x_prompt
    xs = x_sample.transpose(1, 0, 2).reshape(ts * bs, D_MODEL)
    cache_all = cache_nsa_kv.reshape(DEPTH * n_pool, page, 4 * KV_W)
    st_p, st_s = [], []
    for i in range(DEPTH):
        lw = layer_weights(i, norm_mix, w_in, conv_dw, conv_b, conv_ln_g, conv_ln_b, w_a_out, pool_w, pool_scale,
                           w_b_out, pe_cmp, w_cmp1, w_cmp2, w_c_out, w_o, norm_ffn, w_up, ffn_dw, w_down, norm_ple,
                           w_ple_gate, w_ple)
        xp, st = prompt_layer(xp, p_prompt[i], lw, tables)
        st_p.append(st)
        xs, st = sample_layer(xs, p_sample[i].transpose(1, 0, 2).reshape(ts * bs, PLE_DIM), lw, tables_s,
                              cache_all, page_table + i * n_pool, state_nsa_win[i],
                              state_conv[i], state_pool[i], state_ffn[i], ts)
        st_s.append(st)
    y_prompt = final_norm(xp.reshape(B * T, D_MODEL), norm_final).reshape(B, T, D_MODEL)
    y_sample = final_norm(xs, norm_final, tm=256).reshape(ts, bs, D_MODEL).transpose(1, 0, 2)
    stk = lambda lst, k: jnp.stack([s[k] for s in lst])
    roll = lambda state, rows: jnp.concatenate([state[:, :, ts:], rows], axis=2)
    win_rows = stk(st_s, 4).reshape((DEPTH, bs, ts) + state_nsa_win.shape[3:])
    return (y_prompt, y_sample, stk(st_p, 3), stk(st_p, 4), stk(st_p, 0), stk(st_p, 1), stk(st_p, 2),
            stk(st_s, 3), roll(state_nsa_win, win_rows), roll(state_conv, stk(st_s, 0)), roll(state_pool, stk(st_s, 1)),
            stk(st_s, 2))
```

```python
import functools
import numpy as np
import jax
import jax.numpy as jnp
from jax import lax
from jax.experimental import pallas as pl
from jax.experimental.pallas import tpu as pltpu

D_MODEL = 1024
DEPTH = 4
CONV_WIDTH = D_MODEL // 2
POOL_WIDTH = D_MODEL // 2
HEAD_DIM = 64
N_HEADS = (D_MODEL // 2) // HEAD_DIM
N_KV_HEADS = 2
GROUP = N_HEADS // N_KV_HEADS
ROT_DIM = HEAD_DIM // 4
ROPE_THETA = 500000.0
CONV_W = 31
POOL_WINDOWS = (2, 4, 8, 16)
POOL_MAX = max(POOL_WINDOWS)
POOL_GC = POOL_WIDTH // len(POOL_WINDOWS)
CMP_BLOCK = 32
CMP_STRIDE = 16
SEL_BLOCK = 64
N_SELECT = 16
WINDOW = 512
Q_BLOCK = 128
D_FF = ((8 * D_MODEL // 3 + 127) // 128) * 128
FFN_CONV_W = 3
PLE_DIM = 256
EPS = 1e-6
FORCE_SCORE = 1e4
NEG = -1e30

SPLIT_SIZES = (2 * CONV_WIDTH, POOL_WIDTH, N_HEADS * HEAD_DIM, 6 * N_KV_HEADS * HEAD_DIM, 3 * N_HEADS, 3 * D_MODEL)
SPLIT_AT = tuple(int(v) for v in np.cumsum(SPLIT_SIZES)[:-1])

F32 = jnp.float32
BF16 = jnp.bfloat16
LANES = 128
VMEM_LIMIT = 56 * 1024 * 1024
NT = (((1,), (1,)), ((), ()))
LOG2E = 1.4426950408889634

NG_PAD = LANES
C_A, C_B, C_Q, C_KV = 0, 2 * CONV_WIDTH, 2 * CONV_WIDTH + POOL_WIDTH, 2 * CONV_WIDTH + POOL_WIDTH + 512
C_NG = C_KV + 768
C_MG = C_NG + NG_PAD
IN_PACKED = C_MG + 3 * D_MODEL
KV_W = N_KV_HEADS * HEAD_DIM


def _cparams(*sem):
    return pltpu.CompilerParams(dimension_semantics=sem, vmem_limit_bytes=VMEM_LIMIT)


def _const_spec(shape):
    nd = len(shape)
    return pl.BlockSpec(shape, lambda *_: (0,) * nd, pipeline_mode=pl.Buffered(1))


def _rms(x, g):
    return x * lax.rsqrt(jnp.mean(x * x, axis=-1, keepdims=True) + EPS) * g


def _gelu(x):
    return 0.5 * x * (1.0 + jnp.tanh(np.sqrt(2.0 / np.pi).astype(np.float32) * (x + 0.044715 * (x * x * x))))


def _rope_tables(pos):
    half = ROT_DIM // 2
    inv = ROPE_THETA ** (-jnp.arange(half, dtype=F32) * 2.0 / ROT_DIM)
    ang = pos.astype(F32)[:, None] * inv[None, :]
    cos, sin = jnp.cos(ang), jnp.sin(ang)
    rows = pos.shape[0]
    ones = jnp.ones((rows, HEAD_DIM - ROT_DIM), F32)
    zeros = jnp.zeros((rows, HEAD_DIM - ROT_DIM), F32)
    z8 = jnp.zeros((rows, half), F32)
    cos_h = jnp.concatenate([cos, cos, ones], axis=1)
    sup_h = jnp.concatenate([-sin, z8, zeros], axis=1)
    sdn_h = jnp.concatenate([z8, sin, zeros], axis=1)
    rep = LANES // HEAD_DIM
    return jnp.tile(cos_h, (1, rep)), jnp.tile(sup_h, (1, rep)), jnp.tile(sdn_h, (1, rep))


def _rope128(x, cos, sup, sdn):
    return x * cos + pltpu.roll(x, LANES - ROT_DIM // 2, 1) * sup + pltpu.roll(x, ROT_DIM // 2, 1) * sdn


def _inproj_kernel(x_ref, g_ref, w_ref, cos_ref, sup_ref, sdn_ref,
                   a_ref, zb_ref, qh_ref, ql_ref, kv4_ref, win_ref, cmp_ref, slc_ref, winb_ref, ng_ref, mg_ref):
    h = _rms(x_ref[...], g_ref[...]).astype(BF16)

    def proj(c0, n):
        return jnp.dot(h, w_ref[:, c0:c0 + n], preferred_element_type=F32)

    za = proj(C_A, 2 * CONV_WIDTH)
    a_ref[...] = za[:, :CONV_WIDTH] * jax.nn.sigmoid(za[:, CONV_WIDTH:])
    zb_ref[...] = proj(C_B, POOL_WIDTH)
    cos, sup, sdn = cos_ref[...], sup_ref[...], sdn_ref[...]
    zq = proj(C_Q, N_HEADS * HEAD_DIM)
    scale = HEAD_DIM ** -0.5 * LOG2E
    for c in range(N_HEADS * HEAD_DIM // LANES):
        sl = slice(c * LANES, (c + 1) * LANES)
        qr = _rope128(zq[:, sl], cos, sup, sdn) * scale
        qh = qr.astype(BF16)
        qh_ref[:, sl] = qh
        ql_ref[:, sl] = (qr - qh.astype(F32)).astype(BF16)
    zkv = proj(C_KV, 6 * KV_W)
    kind = [zkv[:, j * KV_W:(j + 1) * KV_W] for j in range(6)]
    for j in (0, 2, 4):
        kind[j] = _rope128(kind[j], cos, sup, sdn)
    for j in range(4):
        kv4_ref[:, j * KV_W:(j + 1) * KV_W] = kind[j]
    cmp_ref[:, 0:KV_W] = kind[0]
    cmp_ref[:, KV_W:2 * KV_W] = kind[1]
    slc_ref[:, 0:KV_W] = kind[2].astype(BF16)
    slc_ref[:, KV_W:2 * KV_W] = kind[3].astype(BF16)
    win_ref[:, 0:KV_W] = kind[4]
    win_ref[:, KV_W:2 * KV_W] = kind[5]
    winb_ref[:, 0:KV_W] = kind[4].astype(BF16)
    winb_ref[:, KV_W:2 * KV_W] = kind[5].astype(BF16)
    ng_ref[...] = jax.nn.sigmoid(proj(C_NG, NG_PAD))
    mg_ref[...] = jax.nn.sigmoid(proj(C_MG, 3 * D_MODEL))


def _pack_w_in(w_in):
    parts = jnp.split(w_in, SPLIT_AT, axis=-1)
    ng = jnp.pad(parts[4], ((0, 0), (0, NG_PAD - parts[4].shape[1])))
    return jnp.concatenate([parts[0], parts[1], parts[2], parts[3], ng, parts[5]], axis=1).astype(BF16)


def inproj(x2, g, w_packed, tables, period, tm):
    n = x2.shape[0]
    pblocks = period // tm
    row = lambda c: pl.BlockSpec((tm, c), lambda i: (i, 0))
    tab = pl.BlockSpec((tm, LANES), lambda i: (i % pblocks, 0))
    outs = [(CONV_WIDTH, F32), (POOL_WIDTH, F32), (512, BF16), (512, BF16), (4 * KV_W, F32), (2 * KV_W, F32),
            (2 * KV_W, F32), (2 * KV_W, BF16), (2 * KV_W, BF16), (NG_PAD, F32), (3 * D_MODEL, F32)]
    return pl.pallas_call(
        _inproj_kernel,
        grid=(n // tm,),
        in_specs=[row(D_MODEL), _const_spec((1, D_MODEL)), _const_spec((D_MODEL, IN_PACKED)), tab, tab, tab],
        out_specs=[row(c) for c, _ in outs],
        out_shape=[jax.ShapeDtypeStruct((n, c), dt) for c, dt in outs],
        compiler_params=_cparams("parallel"),
        name="inproj",
    )(x2, g.reshape(1, D_MODEL), w_packed, *tables)


CONV_HALO = 32


def _conva_kernel(a_ref, w_ref, b_ref, g_ref, beta_ref, c_ref, ext_ref, *, tm, rc):
    @pl.when(pl.program_id(1) == 0)
    def _():
        ext_ref[0:CONV_HALO, :] = jnp.zeros((CONV_HALO, CONV_WIDTH), F32)

    ext_ref[CONV_HALO:CONV_HALO + tm, :] = a_ref[0]
    off = CONV_HALO - (CONV_W - 1)
    for r0 in range(0, tm, rc):
        acc = jnp.zeros((rc, CONV_WIDTH), F32)
        for k in range(CONV_W):
            acc = acc + ext_ref[r0 + off + k:r0 + off + k + rc, :] * w_ref[k:k + 1, :]
        y = acc + b_ref[...]
        mu = jnp.mean(y, axis=-1, keepdims=True)
        d = y - mu
        var = jnp.mean(d * d, axis=-1, keepdims=True)
        yn = d * lax.rsqrt(var + EPS) * g_ref[...] + beta_ref[...]
        c_ref[0, r0:r0 + rc, :] = (yn * jax.nn.sigmoid(yn)).astype(BF16)
    ext_ref[0:CONV_HALO, :] = ext_ref[tm:tm + CONV_HALO, :]


def conv_a(a3, conv_dw, conv_b, ln_g, ln_b, tm=256, rc=32):
    b, t, c = a3.shape
    vec = lambda v: v.reshape(1, c)
    return pl.pallas_call(
        functools.partial(_conva_kernel, tm=tm, rc=rc),
        grid=(b, t // tm),
        in_specs=[pl.BlockSpec((1, tm, c), lambda i, j: (i, j, 0)), _const_spec((CONV_W, c)),
                  _const_spec((1, c)), _const_spec((1, c)), _const_spec((1, c))],
        out_specs=pl.BlockSpec((1, tm, c), lambda i, j: (i, j, 0)),
        out_shape=jax.ShapeDtypeStruct((b, t, c), BF16),
        scratch_shapes=[pltpu.VMEM((tm + CONV_HALO, c), F32)],
        compiler_params=_cparams("arbitrary", "arbitrary"),
        name="conv_a",
    )(a3, conv_dw, vec(conv_b), vec(ln_g), vec(ln_b))


POOL_HALO = 16


def _pool_kernel(z_ref, pw_ref, sc_ref, pb_ref, ext_ref, *, tm, rc):
    i = pl.program_id(1)

    @pl.when(i == 0)
    def _():
        ext_ref[0:POOL_HALO, :] = jnp.zeros((POOL_HALO, POOL_WIDTH), F32)

    ext_ref[POOL_HALO:POOL_HALO + tm, :] = z_ref[0]
    for r0 in range(0, tm, rc):
        pos = (i * tm + r0 + lax.broadcasted_iota(jnp.int32, (rc, 1), 0)).astype(F32)
        for gi, w in enumerate(POOL_WINDOWS):
            lanes = slice(gi * POOL_GC, (gi + 1) * POOL_GC)
            base = POOL_HALO + r0
            cur = ext_ref[base:base + rc, lanes]
            s = cur
            for k in range(1, w):
                s = s + ext_ref[base - k:base - k + rc, lanes]
            cnt = jnp.minimum(pos + 1.0, float(w))
            d = s / cnt - cur
            y = jnp.dot(d.astype(BF16), pw_ref[gi], preferred_element_type=F32) * sc_ref[:, lanes]
            pb_ref[0, r0:r0 + rc, lanes] = y.astype(BF16)
    ext_ref[0:POOL_HALO, :] = ext_ref[tm:tm + POOL_HALO, :]


def pool_b(z3, pool_w, pool_scale, tm=512, rc=128):
    b, t, c = z3.shape
    return pl.pallas_call(
        functools.partial(_pool_kernel, tm=tm, rc=rc),
        grid=(b, t // tm),
        in_specs=[pl.BlockSpec((1, tm, c), lambda i, j: (i, j, 0)),
                  _const_spec((len(POOL_WINDOWS), POOL_GC, POOL_GC)), _const_spec((1, c))],
        out_specs=pl.BlockSpec((1, tm, c), lambda i, j: (i, j, 0)),
        out_shape=jax.ShapeDtypeStruct((b, t, c), BF16),
        scratch_shapes=[pltpu.VMEM((tm + POOL_HALO, c), F32)],
        compiler_params=_cparams("arbitrary", "arbitrary"),
        name="pool_b",
    )(z3, pool_w.astype(BF16), pool_scale.reshape(1, c))


SEG = CMP_STRIDE
CMP_LANES = 2 * KV_W


def _compress_weights(pe_c, w1, w2):
    eye = jnp.eye(2, dtype=F32)
    w1r = w1.reshape(2, CMP_BLOCK, HEAD_DIM, HEAD_DIM)
    def expand(w):
        return jnp.einsum('kjde,kl,gh->jkgdlhe', w, eye, eye).reshape(SEG * CMP_LANES, CMP_LANES).astype(BF16)
    wa, wb = expand(w1r[:, :SEG]), expand(w1r[:, SEG:])
    w2bd = jnp.einsum('kde,kl,gh->kgdlhe', w2, eye, eye).reshape(CMP_LANES, CMP_LANES).astype(BF16)
    def pe_row(p):
        return jnp.broadcast_to(p.transpose(1, 0, 2)[:, :, None, :], (SEG, 2, N_KV_HEADS, HEAD_DIM)).reshape(1, SEG * CMP_LANES)
    return wa, wb, w2bd, pe_row(pe_c[:, :SEG]), pe_row(pe_c[:, SEG:])


def _compress_kernel(x_ref, pea_ref, peb_ref, wa_ref, wb_ref, w2_ref, o_ref, bsc_ref, *, nseg):
    x = x_ref[0]
    a = jnp.dot((x + pea_ref[...]).astype(BF16), wa_ref[...], preferred_element_type=F32)
    bsc_ref[0:nseg, :] = jnp.dot((x + peb_ref[...]).astype(BF16), wb_ref[...], preferred_element_type=F32)
    bsc_ref[nseg:nseg + 8, :] = jnp.zeros((8, CMP_LANES), F32)
    u = a + bsc_ref[1:nseg + 1, :]
    o_ref[0] = jnp.dot(_gelu(u).astype(BF16), w2_ref[...], preferred_element_type=F32)


def compress_prompt(cmp3, cw):
    b, t, _ = cmp3.shape
    nseg = t // SEG
    wa, wb, w2bd, pea, peb = cw
    xs = cmp3.reshape(b, nseg, SEG * CMP_LANES)
    return pl.pallas_call(
        functools.partial(_compress_kernel, nseg=nseg),
        grid=(b,),
        in_specs=[pl.BlockSpec((1, nseg, SEG * CMP_LANES), lambda i: (i, 0, 0)),
                  _const_spec((1, SEG * CMP_LANES)), _const_spec((1, SEG * CMP_LANES)),
                  _const_spec((SEG * CMP_LANES, CMP_LANES)), _const_spec((SEG * CMP_LANES, CMP_LANES)),
                  _const_spec((CMP_LANES, CMP_LANES))],
        out_specs=pl.BlockSpec((1, nseg, CMP_LANES), lambda i: (i, 0, 0)),
        out_shape=jax.ShapeDtypeStruct((b, nseg, CMP_LANES), F32),
        scratch_shapes=[pltpu.VMEM((nseg + 8, CMP_LANES), F32)],
        compiler_params=_cparams("arbitrary"),
        name="compress",
    )(xs, pea, peb, wa, wb, w2bd)


QA = LANES + HEAD_DIM
Q3 = 3 * HEAD_DIM


def _softmax_rows(s, mask):
    s = jnp.where(mask, s, NEG)
    m = jnp.max(s, axis=-1, keepdims=True)
    e = jnp.where(mask, jnp.exp2(s - m), 0.0)
    l = jnp.sum(e, axis=-1, keepdims=True)
    return e * (1.0 / jnp.where(l > 0.0, l, 1.0))


KPAD = 2 * LANES
VT_ROWS = HEAD_DIM + 16


def _softmax_cols(s, mask):
    s = jnp.where(mask, s, NEG)
    m = jnp.max(s, axis=0, keepdims=True)
    e = jnp.where(mask, jnp.exp2(s - m), 0.0)
    l = jnp.sum(e, axis=0, keepdims=True)
    return e * (1.0 / jnp.where(l > 0.0, l, 1.0))


def _nsa_t_kernel(qh_ref, ql_ref, k_ref, vt_ref, wk0_ref, wk1_ref, wk2_ref, wv0_ref, wv1_ref, wv2_ref,
                  kc_ref, vct_ref, oht_ref, ng_ref, o_ref,
                  kaug_ref, kc3_ref, q3_ref, qaug_ref, m_ref, vta_ref, acc_ref, sc_ref, oc_ref, ow_ref,
                  *, t_len, tq, kc_len, ncmp):
    i = pl.program_id(1)
    nsel = t_len // SEL_BLOCK
    cols = GROUP * tq
    nseg = kc_ref.shape[1]

    @pl.when(i == 0)
    def _():
        blk = lax.broadcasted_iota(jnp.int32, (t_len, LANES), 0) // SEL_BLOCK
        lane = lax.broadcasted_iota(jnp.int32, (t_len, LANES), 1)
        onehot = jnp.where(blk == lane, 1.0, 0.0).astype(BF16)
        kc_all = kc_ref[0]
        for g in range(N_KV_HEADS):
            kaug_ref[g, :, 0:LANES] = onehot
            kaug_ref[g, :, LANES:QA] = k_ref[0, :, g * HEAD_DIM:(g + 1) * HEAD_DIM]
            kaug_ref[g, :, QA:KPAD] = jnp.zeros((t_len, KPAD - QA), BF16)
            vta_ref[g, 0:HEAD_DIM, :] = vt_ref[0, g * HEAD_DIM:(g + 1) * HEAD_DIM, :]
            one_row = lax.broadcasted_iota(jnp.int32, (VT_ROWS - HEAD_DIM, t_len), 0) == 0
            vta_ref[g, HEAD_DIM:VT_ROWS, :] = jnp.where(one_row, 1.0, 0.0).astype(BF16)
            kc = kc_all[:, g * HEAD_DIM:(g + 1) * HEAD_DIM]
            kch = kc.astype(BF16)
            kc3_ref[g, :, 0:HEAD_DIM] = kch
            kc3_ref[g, :, HEAD_DIM:2 * HEAD_DIM] = kch
            kc3_ref[g, :, 2 * HEAD_DIM:Q3] = (kc - kch.astype(F32)).astype(BF16)
            kc3_ref[g, :, Q3:KPAD] = jnp.zeros((nseg, KPAD - Q3), BF16)
        q3_ref[Q3:KPAD, :] = jnp.zeros((KPAD - Q3, cols), BF16)
        qaug_ref[QA:KPAD, :] = jnp.zeros((KPAD - QA, cols), BF16)

    s0 = i * tq
    pos = s0 + lax.broadcasted_iota(jnp.int32, (1, cols), 1) % tq
    c_last = s0 // kc_len

    for g in range(N_KV_HEADS):
        for r in range(GROUP):
            hs = slice((g * GROUP + r) * HEAD_DIM, (g * GROUP + r + 1) * HEAD_DIM)
            cs = slice(r * tq, (r + 1) * tq)
            qh = qh_ref[0, hs, :]
            q3_ref[0:HEAD_DIM, cs] = qh
            q3_ref[HEAD_DIM:2 * HEAD_DIM, cs] = ql_ref[0, hs, :]
            q3_ref[2 * HEAD_DIM:Q3, cs] = qh
            qaug_ref[LANES:QA, cs] = qh

        s = jnp.dot(kc3_ref[g], q3_ref[...], preferred_element_type=F32)
        n_idx = lax.broadcasted_iota(jnp.int32, (nseg, 1), 0)
        end = jnp.where(n_idx < ncmp, n_idx * CMP_STRIDE + (CMP_BLOCK - 1), t_len + CMP_BLOCK)
        p = _softmax_cols(s, end <= pos)
        vct = vct_ref[0, g * HEAD_DIM:(g + 1) * HEAD_DIM, :].astype(BF16)
        o_c = jnp.dot(vct, p.astype(BF16), preferred_element_type=F32)
        psum = p[:, 0:tq]
        for r in range(1, GROUP):
            psum = psum + p[:, r * tq:(r + 1) * tq]
        p_hi = psum.astype(BF16)
        p_lo = (psum - p_hi.astype(F32)).astype(BF16)
        v = (jnp.dot(oht_ref[...], p_hi, preferred_element_type=F32)
             + jnp.dot(oht_ref[...], p_lo, preferred_element_type=F32))

        jidx = lax.broadcasted_iota(jnp.int32, (LANES, tq), 0)
        qblk = (s0 + lax.broadcasted_iota(jnp.int32, (1, tq), 1)) // SEL_BLOCK
        forced = (jidx == 0) | (jidx == qblk) | (jidx == qblk - 1)
        v = jnp.where(jidx <= qblk, jnp.where(forced, FORCE_SCORE, v), -1.0)
        v = jnp.where(jidx < nsel, v, -3.0)
        sel = jnp.zeros((LANES, tq), F32)
        for _ in range(min(N_SELECT, nsel)):
            mx = jnp.max(v, axis=0, keepdims=True)
            first = jnp.min(jnp.where(v == mx, jidx, LANES), axis=0, keepdims=True)
            pick = jidx == first
            sel = jnp.where(pick, 1.0, sel)
            v = jnp.where(pick, -2.0, v)
        selneg = jnp.where(sel > 0.5, 0.0, NEG).astype(BF16)
        for r in range(GROUP):
            qaug_ref[0:LANES, r * tq:(r + 1) * tq] = selneg

        ks = slice(g * HEAD_DIM, (g + 1) * HEAD_DIM)
        kw = jnp.concatenate([w[0, :, ks] for w in (wk0_ref, wk1_ref, wk2_ref)], axis=0)
        vwt = jnp.concatenate([w[0, ks, :] for w in (wv0_ref, wv1_ref, wv2_ref)], axis=1)
        sw = jnp.dot(kw, q3_ref[0:HEAD_DIM, :], preferred_element_type=F32)
        pos_w = s0 - 2 * tq + lax.broadcasted_iota(jnp.int32, (3 * tq, 1), 0)
        mw = (pos_w <= pos) & (pos_w > pos - WINDOW) & (pos_w >= 0)
        pw = _softmax_cols(sw, mw)
        o_w = jnp.dot(vwt, pw.astype(BF16), preferred_element_type=F32)
        oc_ref[...] = o_c
        ow_ref[...] = o_w

        m_ref[...] = jnp.full((1, cols), NEG, F32)
        acc_ref[...] = jnp.zeros((VT_ROWS, cols), F32)

        def scores(c):
            k0 = pl.multiple_of(c * kc_len, kc_len)
            return jnp.dot(kaug_ref[g, pl.ds(k0, kc_len), :], qaug_ref[...], preferred_element_type=F32)

        def update(c, sc, cmax, causal):
            k0 = pl.multiple_of(c * kc_len, kc_len)
            if causal:
                kpos = k0 + lax.broadcasted_iota(jnp.int32, (kc_len, 1), 0)
                sc = jnp.where(kpos <= pos, sc, NEG)
                cmax = jnp.max(sc, axis=0, keepdims=True)
            m_prev = m_ref[...]
            m_new = jnp.maximum(m_prev, cmax)
            alpha = jnp.exp2(m_prev - m_new)
            pe = jnp.exp2(sc - m_new).astype(BF16)
            vt = vta_ref[g, :, pl.ds(k0, kc_len)]
            acc_ref[...] = alpha * acc_ref[...] + jnp.dot(vt, pe, preferred_element_type=F32)
            m_ref[...] = m_new

        sc_ref[0] = scores(0)

        def body(c, carry):
            slot = c % 2
            sc = sc_ref[slot]
            sc_ref[1 - slot] = scores(c + 1)
            update(c, sc, jnp.max(sc, axis=0, keepdims=True), False)
            return carry

        lax.fori_loop(0, c_last, body, 0)
        update(c_last, sc_ref[c_last % 2], None, True)
        o_s = acc_ref[0:HEAD_DIM, :] * (1.0 / acc_ref[HEAD_DIM:HEAD_DIM + 1, :])

        for r in range(GROUP):
            cs = slice(r * tq, (r + 1) * tq)
            c0 = (g * GROUP + r) * 3
            o = (ng_ref[0, c0:c0 + 1, :] * oc_ref[:, cs] + ng_ref[0, c0 + 1:c0 + 2, :] * o_s[:, cs]
                 + ng_ref[0, c0 + 2:c0 + 3, :] * ow_ref[:, cs])
            o_ref[0, (g * GROUP + r) * HEAD_DIM:(g * GROUP + r + 1) * HEAD_DIM, :] = o.astype(BF16)


def nsa_prompt_t(qh3, ql3, slc3, winb3, kcvc, ng3, tq=256, kc_len=512):
    b, t, _ = qh3.shape
    nseg = t // SEG
    ncmp = (t - CMP_BLOCK) // CMP_STRIDE + 1
    owner = (jnp.arange(nseg) * CMP_STRIDE) // SEL_BLOCK
    oht = (jnp.arange(LANES)[:, None] == owner[None, :]).astype(BF16)
    tr = lambda v: v.transpose(0, 2, 1)
    cols = GROUP * tq
    qspec = pl.BlockSpec((1, 512, tq), lambda i, j: (i, 0, j))
    wk = lambda back: pl.BlockSpec((1, tq, KV_W), lambda i, j: (i, jnp.maximum(j - back, 0), 0))
    wv = lambda back: pl.BlockSpec((1, KV_W, tq), lambda i, j: (i, 1, jnp.maximum(j - back, 0)))
    winbt = tr(winb3)
    out_t = pl.pallas_call(
        functools.partial(_nsa_t_kernel, t_len=t, tq=tq, kc_len=kc_len, ncmp=ncmp),
        grid=(b, t // tq),
        in_specs=[qspec, qspec,
                  pl.BlockSpec((1, t, KV_W), lambda i, j: (i, 0, 0)),
                  pl.BlockSpec((1, KV_W, t), lambda i, j: (i, 1, 0)),
                  wk(2), wk(1), wk(0), wv(2), wv(1), wv(0),
                  pl.BlockSpec((1, nseg, KV_W), lambda i, j: (i, 0, 0)),
                  pl.BlockSpec((1, KV_W, nseg), lambda i, j: (i, 1, 0)),
                  _const_spec((LANES, nseg)),
                  pl.BlockSpec((1, NG_PAD, tq), lambda i, j: (i, 0, j))],
        out_specs=qspec,
        out_shape=jax.ShapeDtypeStruct((b, 512, t), BF16),
        scratch_shapes=[pltpu.VMEM((N_KV_HEADS, t, KPAD), BF16), pltpu.VMEM((N_KV_HEADS, nseg, KPAD), BF16),
                        pltpu.VMEM((KPAD, cols), BF16), pltpu.VMEM((KPAD, cols), BF16),
                        pltpu.VMEM((1, cols), F32), pltpu.VMEM((N_KV_HEADS, VT_ROWS, t), BF16),
                        pltpu.VMEM((VT_ROWS, cols), F32),
                        pltpu.VMEM((2, kc_len, cols), F32), pltpu.VMEM((HEAD_DIM, cols), F32),
                        pltpu.VMEM((HEAD_DIM, cols), F32)],
        compiler_params=_cparams("arbitrary", "arbitrary"),
        name="nsa_prompt",
    )(tr(qh3), tr(ql3), slc3, tr(slc3), winb3, winb3, winb3, winbt, winbt, winbt, kcvc, tr(kcvc), oht, tr(ng3))
    return tr(out_t)


def _merge_kernel(c_ref, pb_ref, oc_ref, mg_ref, x_ref, wa_ref, wb_ref, wc_ref, wo_ref, o_ref):
    out_a = jnp.dot(c_ref[...], wa_ref[...], preferred_element_type=F32)
    out_b = jnp.dot(pb_ref[...], wb_ref[...], preferred_element_type=F32)
    out_c = jnp.dot(oc_ref[...], wc_ref[...], preferred_element_type=F32)
    m = (mg_ref[:, 0:D_MODEL] * out_a + mg_ref[:, D_MODEL:2 * D_MODEL] * out_b
         + mg_ref[:, 2 * D_MODEL:3 * D_MODEL] * out_c)
    o_ref[...] = x_ref[...] + jnp.dot(m.astype(BF16), wo_ref[...], preferred_element_type=F32)


def merge(c2, pb2, oc2, mg2, x2, wa, wb, wc, wo, tm=512):
    n = x2.shape[0]
    row = lambda c: pl.BlockSpec((tm, c), lambda i: (i, 0))
    return pl.pallas_call(
        _merge_kernel,
        grid=(n // tm,),
        in_specs=[row(512), row(512), row(512), row(3 * D_MODEL), row(D_MODEL),
                  _const_spec((512, D_MODEL)), _const_spec((512, D_MODEL)), _const_spec((512, D_MODEL)),
                  _const_spec((D_MODEL, D_MODEL))],
        out_specs=row(D_MODEL),
        out_shape=jax.ShapeDtypeStruct((n, D_MODEL), F32),
        compiler_params=_cparams("parallel"),
        name="merge",
    )(c2, pb2, oc2, mg2, x2, wa, wb, wc, wo)


FF_CHUNK = 256
FF_HALO = 8


def _ffn_kernel(x_ref, pe_ref, gf_ref, wup_ref, dw_ref, wdn_ref, gp_ref, wg_ref, wp_ref, o_ref, nf_ref,
                usc_ref, carry_ref, f_ref, *, tm):
    @pl.when(pl.program_id(1) == 0)
    def _():
        carry_ref[...] = jnp.zeros((FF_HALO, 2 * D_FF), F32)

    x = x_ref[0]
    h = _rms(x, gf_ref[...]).astype(BF16)
    for c in range(D_FF // FF_CHUNK):
        cu = []
        for half in range(2):
            col = half * D_FF + c * FF_CHUNK
            cs = slice(col, col + FF_CHUNK)
            u = jnp.dot(h, wup_ref[:, cs], preferred_element_type=F32)
            usc_ref[half, 0:FF_HALO, :] = carry_ref[:, cs]
            usc_ref[half, FF_HALO:FF_HALO + tm, :] = u
            carry_ref[:, cs] = u[tm - FF_HALO:tm]
            cu.append(usc_ref[half, FF_HALO - 2:FF_HALO - 2 + tm, :] * dw_ref[0:1, cs]
                      + usc_ref[half, FF_HALO - 1:FF_HALO - 1 + tm, :] * dw_ref[1:2, cs]
                      + u * dw_ref[2:3, cs])
        f_ref[:, c * FF_CHUNK:(c + 1) * FF_CHUNK] = (_gelu(cu[0]) * cu[1]).astype(BF16)
    acc = jnp.dot(f_ref[...], wdn_ref[...], preferred_element_type=F32)
    nf_ref[0] = carry_ref[FF_HALO - 2:FF_HALO, :]
    x2 = x + acc
    gate = jax.nn.sigmoid(jnp.dot(_rms(x2, gp_ref[...]).astype(BF16), wg_ref[...], preferred_element_type=F32))
    o_ref[0] = x2 + gate * jnp.dot(pe_ref[0].astype(BF16), wp_ref[...], preferred_element_type=F32)


def ffn_prompt(x3, pe3, g_ffn, w_up, ffn_dw, w_down, g_ple, w_gate, w_ple, tm=512):
    b, t, _ = x3.shape
    tile = lambda c: pl.BlockSpec((1, tm, c), lambda i, j: (i, j, 0))
    return pl.pallas_call(
        functools.partial(_ffn_kernel, tm=tm),
        grid=(b, t // tm),
        in_specs=[tile(D_MODEL), tile(PLE_DIM), _const_spec((1, D_MODEL)), _const_spec((D_MODEL, 2 * D_FF)),
                  _const_spec((FFN_CONV_W, 2 * D_FF)), _const_spec((D_FF, D_MODEL)), _const_spec((1, D_MODEL)),
                  _const_spec((D_MODEL, D_MODEL)), _const_spec((PLE_DIM, D_MODEL))],
        out_specs=[tile(D_MODEL), pl.BlockSpec((1, FFN_CONV_W - 1, 2 * D_FF), lambda i, j: (i, 0, 0))],
        out_shape=[jax.ShapeDtypeStruct((b, t, D_MODEL), F32),
                   jax.ShapeDtypeStruct((b, FFN_CONV_W - 1, 2 * D_FF), F32)],
        scratch_shapes=[pltpu.VMEM((2, tm + FF_HALO, FF_CHUNK), F32), pltpu.VMEM((FF_HALO, 2 * D_FF), F32),
                        pltpu.VMEM((tm, D_FF), BF16)],
        compiler_params=_cparams("arbitrary", "arbitrary"),
        name="ffn",
    )(x3, pe3, g_ffn.reshape(1, D_MODEL), w_up, ffn_dw, w_down, g_ple.reshape(1, D_MODEL), w_gate, w_ple)


def _norm_kernel(x_ref, g_ref, o_ref):
    o_ref[...] = _rms(x_ref[...], g_ref[...])


def final_norm(x2, g, tm=512):
    n = x2.shape[0]
    return pl.pallas_call(
        _norm_kernel,
        grid=(n // tm,),
        in_specs=[pl.BlockSpec((tm, D_MODEL), lambda i: (i, 0)), _const_spec((1, D_MODEL))],
        out_specs=pl.BlockSpec((tm, D_MODEL), lambda i: (i, 0)),
        out_shape=jax.ShapeDtypeStruct((n, D_MODEL), F32),
        compiler_params=_cparams("parallel"),
        name="final_norm",
    )(x2, g.reshape(1, D_MODEL))


def prompt_layer(x3, pe3, lw, tables):
    b, t, _ = x3.shape
    n = b * t
    (a, zb, qh, ql, kv4, win, cmpx, slc, winb, ng, mg) = inproj(
        x3.reshape(n, D_MODEL), lw['norm_mix'], lw['w_in'], tables, t, 512)
    r3 = lambda v: v.reshape(b, t, v.shape[-1])
    a3 = r3(a)
    zb3 = r3(zb)
    c = conv_a(a3, lw['conv_dw'], lw['conv_b'], lw['conv_ln_g'], lw['conv_ln_b'])
    pb = pool_b(zb3, lw['pool_w'], lw['pool_scale'])
    kcvc = compress_prompt(r3(cmpx), lw['cmp'])
    oc = nsa_prompt_t(r3(qh), r3(ql), r3(slc), r3(winb), kcvc, r3(ng))
    x1 = merge(c.reshape(n, 512), pb.reshape(n, 512), oc.reshape(n, 512), mg, x3.reshape(n, D_MODEL),
               lw['w_a_out'], lw['w_b_out'], lw['w_c_out'], lw['w_o'])
    x2, new_ffn = ffn_prompt(x1.reshape(b, t, D_MODEL), pe3, lw['norm_ffn'], lw['w_up'], lw['ffn_dw'], lw['w_down'],
                             lw['norm_ple'], lw['w_ple_gate'], lw['w_ple'])
    wp = min(WINDOW, t)
    new_kv = kv4.reshape(b, t, 4, N_KV_HEADS, HEAD_DIM)
    new_win = r3(win)[:, t - wp:].reshape(b, wp, 2, N_KV_HEADS, HEAD_DIM)
    new_conv = a3[:, t - (CONV_W - 1):]
    new_pool = zb3[:, t - (POOL_MAX - 1):]
    return x2, (new_conv, new_pool, new_ffn, new_kv, new_win)


def layer_weights(i, norm_mix, w_in, conv_dw, conv_b, conv_ln_g, conv_ln_b, w_a_out, pool_w, pool_scale, w_b_out,
                  pe_cmp, w_cmp1, w_cmp2, w_c_out, w_o, norm_ffn, w_up, ffn_dw, w_down, norm_ple, w_ple_gate, w_ple):
    return dict(norm_mix=norm_mix[i], w_in=_pack_w_in(w_in[i]), conv_dw=conv_dw[i], conv_b=conv_b[i],
                conv_ln_g=conv_ln_g[i], conv_ln_b=conv_ln_b[i], w_a_out=w_a_out[i].astype(BF16), pool_w=pool_w[i],
                pool_scale=pool_scale[i], w_b_out=w_b_out[i].astype(BF16),
                cmp=_compress_weights(pe_cmp[i], w_cmp1[i], w_cmp2[i]),
                cmp_s=_compress_weights_sample(pe_cmp[i], w_cmp1[i], w_cmp2[i]), w_c_out=w_c_out[i].astype(BF16),
                w_o=w_o[i].astype(BF16), norm_ffn=norm_ffn[i], w_up=w_up[i].astype(BF16), ffn_dw=ffn_dw[i],
                w_down=w_down[i].astype(BF16), norm_ple=norm_ple[i], w_ple_gate=w_ple_gate[i].astype(BF16),
                w_ple=w_ple[i].astype(BF16))


SB = 32


def _conva_s_kernel(st_ref, a_ref, w_ref, b_ref, g_ref, beta_ref, c_ref, *, t_new, bsz):
    hist = CONV_W - 1
    for t in range(t_new):
        for b0 in range(0, bsz, SB):
            acc = jnp.zeros((SB, CONV_WIDTH), F32)
            for k in range(CONV_W):
                j = t + k
                row = st_ref[j, b0:b0 + SB, :] if j < hist else a_ref[j - hist, b0:b0 + SB, :]
                acc = acc + row * w_ref[k:k + 1, :]
            y = acc + b_ref[...]
            mu = jnp.mean(y, axis=-1, keepdims=True)
            d = y - mu
            var = jnp.mean(d * d, axis=-1, keepdims=True)
            yn = d * lax.rsqrt(var + EPS) * g_ref[...] + beta_ref[...]
            c_ref[t, b0:b0 + SB, :] = (yn * jax.nn.sigmoid(yn)).astype(BF16)


def conv_a_sample(st_tm, a_tm, conv_dw, conv_b, ln_g, ln_b):
    t_new, bsz, c = a_tm.shape
    vec = lambda v: v.reshape(1, c)
    return pl.pallas_call(
        functools.partial(_conva_s_kernel, t_new=t_new, bsz=bsz),
        out_shape=jax.ShapeDtypeStruct((t_new, bsz, c), BF16),
        compiler_params=pltpu.CompilerParams(vmem_limit_bytes=VMEM_LIMIT),
        name="conv_a_sample",
    )(st_tm, a_tm, conv_dw, vec(conv_b), vec(ln_g), vec(ln_b))


def _pool_s_kernel(st_ref, z_ref, pw_ref, sc_ref, pb_ref, *, t_new, pos0):
    hist = POOL_MAX - 1
    for t in range(t_new):
        for gi, w in enumerate(POOL_WINDOWS):
            lanes = slice(gi * POOL_GC, (gi + 1) * POOL_GC)
            cur = z_ref[t, :, lanes]
            s = cur
            for k in range(1, w):
                j = hist + t - k
                s = s + (st_ref[j, :, lanes] if j < hist else z_ref[j - hist, :, lanes])
            cnt = float(min(pos0 + t + 1, w))
            d = s / cnt - cur
            y = jnp.dot(d.astype(BF16), pw_ref[gi], preferred_element_type=F32) * sc_ref[:, lanes]
            pb_ref[t, :, lanes] = y.astype(BF16)


def pool_b_sample(st_tm, z_tm, pool_w, pool_scale, pos0):
    t_new, bsz, c = z_tm.shape
    return pl.pallas_call(
        functools.partial(_pool_s_kernel, t_new=t_new, pos0=pos0),
        out_shape=jax.ShapeDtypeStruct((t_new, bsz, c), BF16),
        compiler_params=pltpu.CompilerParams(vmem_limit_bytes=VMEM_LIMIT),
        name="pool_b_sample",
    )(st_tm, z_tm, pool_w.astype(BF16), pool_scale.reshape(1, c))


def _ffn_s_kernel(x_ref, pe_ref, gf_ref, wv_ref, wg_ref, dwv_ref, dwg_ref, wdn_ref, stv_ref, stg_ref,
                  gp_ref, wgate_ref, wp_ref, o_ref, nfv_ref, nfg_ref, h_ref, acc_ref, *, t_new, bsz):
    c = pl.program_id(0)

    @pl.when(c == 0)
    def _():
        h_ref[...] = _rms(x_ref[...], gf_ref[...]).astype(BF16)
        acc_ref[...] = jnp.zeros(acc_ref.shape, F32)

    def conv_half(w_ref, dw_ref, st_ref, nf_ref):
        u = jnp.dot(h_ref[...], w_ref[...], preferred_element_type=F32)
        ext = [st_ref[0], st_ref[1]] + [u[t * bsz:(t + 1) * bsz] for t in range(t_new)]
        nf_ref[0] = ext[t_new]
        nf_ref[1] = ext[t_new + 1]
        return jnp.concatenate(
            [ext[t] * dw_ref[0:1, :] + ext[t + 1] * dw_ref[1:2, :] + ext[t + 2] * dw_ref[2:3, :] for t in range(t_new)],
            axis=0)

    cv = conv_half(wv_ref, dwv_ref, stv_ref, nfv_ref)
    cg = conv_half(wg_ref, dwg_ref, stg_ref, nfg_ref)
    f = _gelu(cv) * cg
    acc_ref[...] += jnp.dot(f.astype(BF16), wdn_ref[...], preferred_element_type=F32)

    @pl.when(c == pl.num_programs(0) - 1)
    def _():
        x2 = x_ref[...] + acc_ref[...]
        gate = jax.nn.sigmoid(jnp.dot(_rms(x2, gp_ref[...]).astype(BF16), wgate_ref[...], preferred_element_type=F32))
        o_ref[...] = x2 + gate * jnp.dot(pe_ref[...].astype(BF16), wp_ref[...], preferred_element_type=F32)


def ffn_sample(x2, pe2, st_tm, g_ffn, w_up, ffn_dw, w_down, g_ple, w_gate, w_ple, t_new):
    n = x2.shape[0]
    bsz = n // t_new
    nch = D_FF // FF_CHUNK
    full = lambda r, c: pl.BlockSpec((r, c), lambda i: (0, 0))
    return pl.pallas_call(
        functools.partial(_ffn_s_kernel, t_new=t_new, bsz=bsz),
        grid=(nch,),
        in_specs=[full(n, D_MODEL), full(n, PLE_DIM), full(1, D_MODEL),
                  pl.BlockSpec((D_MODEL, FF_CHUNK), lambda i: (0, i)),
                  pl.BlockSpec((D_MODEL, FF_CHUNK), lambda i: (0, nch + i)),
                  pl.BlockSpec((FFN_CONV_W, FF_CHUNK), lambda i: (0, i)),
                  pl.BlockSpec((FFN_CONV_W, FF_CHUNK), lambda i: (0, nch + i)),
                  pl.BlockSpec((FF_CHUNK, D_MODEL), lambda i: (i, 0)),
                  pl.BlockSpec((FFN_CONV_W - 1, bsz, FF_CHUNK), lambda i: (0, 0, i)),
                  pl.BlockSpec((FFN_CONV_W - 1, bsz, FF_CHUNK), lambda i: (0, 0, nch + i)),
                  full(1, D_MODEL), full(D_MODEL, D_MODEL), full(PLE_DIM, D_MODEL)],
        out_specs=[full(n, D_MODEL),
                   pl.BlockSpec((FFN_CONV_W - 1, bsz, FF_CHUNK), lambda i: (0, 0, i)),
                   pl.BlockSpec((FFN_CONV_W - 1, bsz, FF_CHUNK), lambda i: (0, 0, i))],
        out_shape=[jax.ShapeDtypeStruct((n, D_MODEL), F32),
                   jax.ShapeDtypeStruct((FFN_CONV_W - 1, bsz, D_FF), F32),
                   jax.ShapeDtypeStruct((FFN_CONV_W - 1, bsz, D_FF), F32)],
        scratch_shapes=[pltpu.VMEM((n, D_MODEL), BF16), pltpu.VMEM((n, D_MODEL), F32)],
        compiler_params=_cparams("arbitrary"),
        name="ffn_sample",
    )(x2, pe2, g_ffn.reshape(1, D_MODEL), w_up, w_up, ffn_dw, ffn_dw, w_down, st_tm, st_tm,
      g_ple.reshape(1, D_MODEL), w_gate, w_ple)


PAGE = 128
SEG_PER_PAGE = PAGE // SEG


def _compress_weights_sample(pe_c, w1, w2):
    eye = jnp.eye(2, dtype=F32)
    w1r = w1.reshape(2, CMP_BLOCK, HEAD_DIM, HEAD_DIM)
    def expand(w):
        return jnp.einsum('kjde,kl,gh->jkgdlhe', w, eye, eye).reshape(SEG, CMP_LANES, CMP_LANES).astype(BF16)
    w2bd = jnp.einsum('kde,kl,gh->kgdlhe', w2, eye, eye).reshape(CMP_LANES, CMP_LANES).astype(BF16)
    def pe_rows(p):
        return jnp.broadcast_to(p.transpose(1, 0, 2)[:, :, None, :], (SEG, 2, N_KV_HEADS, HEAD_DIM)).reshape(SEG, CMP_LANES)
    return expand(w1r[:, :SEG]), expand(w1r[:, SEG:]), w2bd, pe_rows(pe_c[:, :SEG]), pe_rows(pe_c[:, SEG:])


def _compress_s_kernel(pt_ref, *refs, n_pages, bt):
    del pt_ref
    np_all = bt * n_pages
    pages_k, pages_v = refs[:np_all], refs[np_all:2 * np_all]
    pea_ref, peb_ref, wa_ref, wb_ref, w2_ref, o_ref, bsc_ref = refs[2 * np_all:]
    m = bt * n_pages * SEG_PER_PAGE
    a = jnp.zeros((m, CMP_LANES), F32)
    bm = jnp.zeros((m, CMP_LANES), F32)
    for j in range(SEG):
        xj = jnp.concatenate(
            [jnp.concatenate([pg[pl.ds(j, SEG_PER_PAGE, stride=SEG), :] for pg in pgs], axis=0)
             for pgs in (pages_k, pages_v)], axis=1)
        a = a + jnp.dot((xj + pea_ref[j:j + 1, :]).astype(BF16), wa_ref[j], preferred_element_type=F32)
        bm = bm + jnp.dot((xj + peb_ref[j:j + 1, :]).astype(BF16), wb_ref[j], preferred_element_type=F32)
    bsc_ref[0:m, :] = bm
    bsc_ref[m:m + 8, :] = jnp.zeros((8, CMP_LANES), F32)
    u = a + bsc_ref[1:m + 1, :]
    out = jnp.dot(_gelu(u).astype(BF16), w2_ref[...], preferred_element_type=F32)
    nseg = n_pages * SEG_PER_PAGE
    for q in range(bt):
        o_ref[q] = out[q * nseg:(q + 1) * nseg]


def compress_sample(cache2, page_table, cw, bt=4):
    bsz, n_pages = page_table.shape
    nseg = n_pages * SEG_PER_PAGE
    wa, wb, w2bd, pea, peb = cw
    page_specs = [pl.BlockSpec((None, PAGE, KV_W),
                               functools.partial(lambda i, pt, q, p, kind: (pt[i * bt + q, p], 0, kind), q=q, p=p, kind=kind))
                  for kind in range(2) for q in range(bt) for p in range(n_pages)]
    const = lambda shape: pl.BlockSpec(shape, lambda i, pt: (0,) * len(shape))
    grid_spec = pltpu.PrefetchScalarGridSpec(
        num_scalar_prefetch=1, grid=(bsz // bt,),
        in_specs=page_specs + [const((SEG, CMP_LANES)), const((SEG, CMP_LANES)), const((SEG, CMP_LANES, CMP_LANES)),
                               const((SEG, CMP_LANES, CMP_LANES)), const((CMP_LANES, CMP_LANES))],
        out_specs=pl.BlockSpec((bt, nseg, CMP_LANES), lambda i, pt: (i, 0, 0)),
        scratch_shapes=[pltpu.VMEM((bt * nseg + 8, CMP_LANES), F32)])
    return pl.pallas_call(
        functools.partial(_compress_s_kernel, n_pages=n_pages, bt=bt),
        grid_spec=grid_spec,
        out_shape=jax.ShapeDtypeStruct((bsz, nseg, CMP_LANES), F32),
        compiler_params=_cparams("arbitrary"),
        name="compress_sample",
    )(page_table, *([cache2] * (2 * bt * n_pages)), pea, peb, wa, wb, w2bd)


NEW_PAD = 16


def _nsa_s_kernel(pt_ref, *refs, n_pages, t_new, win_len):
    del pt_ref
    pages = refs[:n_pages]
    (q3_ref, ngr_ref, kcvc_ref, nslc_ref, wst_ref, nwin_ref, oht_ref, o_ref,
     kaug_ref, vsel_ref, kw_ref, vw_ref, kc3_ref, qaug_ref) = refs[n_pages:]
    past = n_pages * PAGE
    kl = kaug_ref.shape[1]
    wl = kw_ref.shape[1]
    rows = GROUP * t_new
    nseg = kcvc_ref.shape[1]
    ncmp = (past + t_new - CMP_BLOCK) // CMP_STRIDE + 1
    nsel = -(-(past + t_new) // SEL_BLOCK)

    @pl.when(pl.program_id(0) == 0)
    def _():
        blk = lax.broadcasted_iota(jnp.int32, (kl, LANES), 0) // SEL_BLOCK
        lane = lax.broadcasted_iota(jnp.int32, (kl, LANES), 1)
        onehot = jnp.where(blk == lane, 1.0, 0.0).astype(BF16)
        for g in range(N_KV_HEADS):
            kaug_ref[g, :, 0:LANES] = onehot
            kaug_ref[g, past:kl, LANES:QA] = jnp.zeros((kl - past, HEAD_DIM), BF16)
            vsel_ref[g, past:kl, :] = jnp.zeros((kl - past, HEAD_DIM), BF16)
            kw_ref[g, win_len:wl, :] = jnp.zeros((wl - win_len, HEAD_DIM), BF16)
            vw_ref[g, win_len:wl, :] = jnp.zeros((wl - win_len, HEAD_DIM), BF16)

    pos4 = past + lax.broadcasted_iota(jnp.int32, (rows, 1), 0) % t_new
    kcvc = kcvc_ref[0]
    nslc = nslc_ref[0]
    nwin = nwin_ref[0]
    wst = wst_ref[0]
    outs = []
    for g in range(N_KV_HEADS):
        ks = slice(g * HEAD_DIM, (g + 1) * HEAD_DIM)
        vs = slice(KV_W + g * HEAD_DIM, KV_W + (g + 1) * HEAD_DIM)
        for p in range(n_pages):
            kaug_ref[g, p * PAGE:(p + 1) * PAGE, LANES:QA] = pages[p][:, ks].astype(BF16)
            vsel_ref[g, p * PAGE:(p + 1) * PAGE, :] = pages[p][:, vs].astype(BF16)
        kaug_ref[g, past:past + NEW_PAD, LANES:QA] = nslc[:, ks]
        vsel_ref[g, past:past + NEW_PAD, :] = nslc[:, vs]
        kw_ref[g, 0:win_len, :] = wst[:, ks].astype(BF16)
        vw_ref[g, 0:win_len, :] = wst[:, vs].astype(BF16)
        kw_ref[g, win_len:win_len + NEW_PAD, :] = nwin[:, ks]
        vw_ref[g, win_len:win_len + NEW_PAD, :] = nwin[:, vs]
        kc = kcvc[:, ks]
        kch = kc.astype(BF16)
        kc3_ref[g, :, 0:HEAD_DIM] = kch
        kc3_ref[g, :, HEAD_DIM:2 * HEAD_DIM] = kch
        kc3_ref[g, :, 2 * HEAD_DIM:Q3] = (kc - kch.astype(F32)).astype(BF16)
        vc = kcvc[:, vs].astype(BF16)
        q3 = q3_ref[0, g]

        s = lax.dot_general(q3, kc3_ref[g], NT, preferred_element_type=F32)
        n_idx = lax.broadcasted_iota(jnp.int32, (1, nseg), 1)
        end = n_idx * CMP_STRIDE + (CMP_BLOCK - 1)
        mask = jnp.where(n_idx < ncmp, end, past + t_new + CMP_BLOCK) <= pos4
        p = _softmax_rows(s, mask)
        o_c = jnp.dot(p.astype(BF16), vc, preferred_element_type=F32)
        ppad = jnp.concatenate([p, jnp.zeros((LANES - rows, nseg), F32)], axis=0)
        p_hi = ppad.astype(BF16)
        p_lo = (ppad - p_hi.astype(F32)).astype(BF16)
        x = (lax.dot_general(oht_ref[...], p_hi, NT, preferred_element_type=F32)
             + lax.dot_general(oht_ref[...], p_lo, NT, preferred_element_type=F32))
        v = x
        for r in range(1, GROUP):
            v = v + pltpu.roll(x, LANES - r * t_new, 1)

        jidx = lax.broadcasted_iota(jnp.int32, (LANES, LANES), 0)
        qblk = (past + lax.broadcasted_iota(jnp.int32, (1, LANES), 1) % t_new) // SEL_BLOCK
        forced = (jidx == 0) | (jidx == qblk) | (jidx == qblk - 1)
        v = jnp.where(jidx <= qblk, jnp.where(forced, FORCE_SCORE, v), -1.0)
        v = jnp.where(jidx < nsel, v, -3.0)
        sel = jnp.zeros((LANES, LANES), F32)
        for _ in range(min(N_SELECT, nsel)):
            mx = jnp.max(v, axis=0, keepdims=True)
            first = jnp.min(jnp.where(v == mx, jidx, LANES), axis=0, keepdims=True)
            pick = jidx == first
            sel = jnp.where(pick, 1.0, sel)
            v = jnp.where(pick, -2.0, v)
        ri = lax.broadcasted_iota(jnp.int32, (rows, LANES), 0) % t_new
        li = lax.broadcasted_iota(jnp.int32, (rows, LANES), 1)
        spread = jnp.where(ri == li, 1.0, 0.0).astype(BF16)
        sel_rows = lax.dot_general(spread, sel.astype(BF16), NT, preferred_element_type=F32)
        qaug_ref[g, :, 0:LANES] = jnp.where(sel_rows > 0.5, 0.0, NEG).astype(BF16)
        qaug_ref[g, :, LANES:QA] = q3[:, 0:HEAD_DIM]

        sc = lax.dot_general(qaug_ref[g], kaug_ref[g], NT, preferred_element_type=F32)
        kpos = lax.broadcasted_iota(jnp.int32, (1, kl), 1)
        ps = _softmax_rows(sc, kpos <= pos4)
        o_s = jnp.dot(ps.astype(BF16), vsel_ref[g], preferred_element_type=F32)

        sw = lax.dot_general(q3[:, 0:HEAD_DIM], kw_ref[g], NT, preferred_element_type=F32)
        widx = lax.broadcasted_iota(jnp.int32, (1, wl), 1)
        pos_w = past - win_len + widx
        mw = (widx < win_len + t_new) & (pos_w <= pos4) & (pos_w > pos4 - WINDOW) & (pos_w >= 0)
        pw = _softmax_rows(sw, mw)
        o_w = jnp.dot(pw.astype(BF16), vw_ref[g], preferred_element_type=F32)

        ngr = ngr_ref[0, g]
        outs.append(ngr[:, 0:1] * o_c + ngr[:, 1:2] * o_s + ngr[:, 2:3] * o_w)
    o_ref[0] = jnp.stack(outs, axis=0)


def nsa_sample_attn(cache2, page_table, q3, ngr, kcvc, nslc, wst, nwin):
    bsz, n_pages = page_table.shape
    t_new = q3.shape[2] // GROUP
    rows = GROUP * t_new
    past = n_pages * PAGE
    win_len = wst.shape[1]
    nseg = kcvc.shape[1]
    kl = past + LANES
    wl = win_len + LANES
    owner = (jnp.arange(nseg) * CMP_STRIDE) // SEL_BLOCK
    oht = (jnp.arange(LANES)[:, None] == owner[None, :]).astype(BF16)
    page_specs = [pl.BlockSpec((None, PAGE, 2 * KV_W), functools.partial(lambda i, pt, p: (pt[i, p], 0, 1), p=p))
                  for p in range(n_pages)]
    per_b = lambda *shape: pl.BlockSpec((1,) + shape, lambda i, pt: (i,) + (0,) * len(shape))
    grid_spec = pltpu.PrefetchScalarGridSpec(
        num_scalar_prefetch=1, grid=(bsz,),
        in_specs=page_specs + [per_b(N_KV_HEADS, rows, Q3), per_b(N_KV_HEADS, rows, LANES), per_b(nseg, CMP_LANES),
                               per_b(NEW_PAD, 2 * KV_W), per_b(win_len, 2 * KV_W), per_b(NEW_PAD, 2 * KV_W),
                               pl.BlockSpec((LANES, nseg), lambda i, pt: (0, 0))],
        out_specs=per_b(N_KV_HEADS, rows, HEAD_DIM),
        scratch_shapes=[pltpu.VMEM((N_KV_HEADS, kl, QA), BF16), pltpu.VMEM((N_KV_HEADS, kl, HEAD_DIM), BF16),
                        pltpu.VMEM((N_KV_HEADS, wl, HEAD_DIM), BF16), pltpu.VMEM((N_KV_HEADS, wl, HEAD_DIM), BF16),
                        pltpu.VMEM((N_KV_HEADS, nseg, Q3), BF16), pltpu.VMEM((N_KV_HEADS, rows, QA), BF16)])
    return pl.pallas_call(
        functools.partial(_nsa_s_kernel, n_pages=n_pages, t_new=t_new, win_len=win_len),
        grid_spec=grid_spec,
        out_shape=jax.ShapeDtypeStruct((bsz, N_KV_HEADS, rows, HEAD_DIM), F32),
        compiler_params=_cparams("arbitrary"),
        name="nsa_sample",
    )(page_table, *([cache2] * n_pages), q3, ngr, kcvc, nslc, wst, nwin, oht)


def sample_layer(x2, pe2, lw, tables, cache2, page_table, st_win, st_conv, st_pool, st_ffn, t_new):
    n = x2.shape[0]
    bsz = n // t_new
    past = page_table.shape[1] * PAGE
    (a, zb, qh, ql, kv4, win, _, slc, winb, ng, mg) = inproj(x2, lw['norm_mix'], lw['w_in'], tables, n, 256)
    tm3 = lambda v: v.reshape(t_new, bsz, v.shape[-1])
    bm3 = lambda v: tm3(v).transpose(1, 0, 2)
    c = conv_a_sample(st_conv.transpose(1, 0, 2), tm3(a), lw['conv_dw'], lw['conv_b'], lw['conv_ln_g'], lw['conv_ln_b'])
    pb = pool_b_sample(st_pool.transpose(1, 0, 2), tm3(zb), lw['pool_w'], lw['pool_scale'], past)
    kcvc = compress_sample(cache2, page_table, lw['cmp_s'])

    def heads(v):
        return v.reshape(t_new, bsz, N_KV_HEADS, GROUP, HEAD_DIM).transpose(1, 2, 3, 0, 4).reshape(
            bsz, N_KV_HEADS, GROUP * t_new, HEAD_DIM)
    qh4, ql4 = heads(qh), heads(ql)
    q3 = jnp.concatenate([qh4, ql4, qh4], axis=-1)
    ngr = ng[:, :3 * N_HEADS].reshape(t_new, bsz, N_KV_HEADS, GROUP, 3).transpose(1, 2, 3, 0, 4).reshape(
        bsz, N_KV_HEADS, GROUP * t_new, 3)
    ngr = jnp.pad(ngr, ((0, 0), (0, 0), (0, 0), (0, LANES - 3)))
    pad_new = lambda v: jnp.pad(bm3(v), ((0, 0), (0, NEW_PAD - t_new), (0, 0)))
    wst = st_win.reshape(bsz, st_win.shape[1], 2 * KV_W)
    oc = nsa_sample_attn(cache2, page_table, q3, ngr, kcvc, pad_new(slc), wst, pad_new(winb))
    oc2 = oc.reshape(bsz, N_KV_HEADS, GROUP, t_new, HEAD_DIM).transpose(3, 0, 1, 2, 4).reshape(n, 512).astype(BF16)
    x1 = merge(c.reshape(n, 512), pb.reshape(n, 512), oc2, mg, x2,
               lw['w_a_out'], lw['w_b_out'], lw['w_c_out'], lw['w_o'], tm=256)
    x3, nfv, nfg = ffn_sample(x1, pe2, st_ffn.transpose(1, 0, 2), lw['norm_ffn'], lw['w_up'], lw['ffn_dw'], lw['w_down'],
                              lw['norm_ple'], lw['w_ple_gate'], lw['w_ple'], t_new)
    new_kv = bm3(kv4).reshape(bsz, t_new, 4, N_KV_HEADS, HEAD_DIM)
    new_ffn = jnp.concatenate([nfv, nfg], axis=-1).transpose(1, 0, 2)
    return x3, (bm3(a), bm3(zb), new_ffn, new_kv, bm3(win))


def kernel(x_prompt, x_sample, cache_nsa_kv, state_nsa_win, state_conv, state_pool, state_ffn, page_table,
           p_prompt, p_sample, norm_mix, w_in, conv_dw, conv_b, conv_ln_g, conv_ln_b, w_a_out, pool_w,
           pool_scale, w_b_out, pe_cmp, w_cmp1, w_cmp2, w_c_out, w_o, norm_ffn, w_up, ffn_dw, w_down,
           norm_ple, w_ple_gate, w_ple, norm_final):
    B, T, _ = x_prompt.shape
    bs, ts, _ = x_sample.shape
    n_pool, page = cache_nsa_kv.shape[1:3]
    assert page == PAGE and T % 512 == 0 and T // SEL_BLOCK <= LANES and (bs * ts) % 256 == 0 and bs % SB == 0
    past_len = page_table.shape[1] * page
    tables = _rope_tables(jnp.arange(T))
    tables_s = _rope_tables(past_len + jnp.repeat(jnp.arange(ts), bs))
    xp = x_prompt
    xs = x_sample.transpose(1, 0, 2).reshape(ts * bs, D_MODEL)
    cache_all = cache_nsa_kv.reshape(DEPTH * n_pool, page, 4 * KV_W)
    st_p, st_s = [], []
    for i in range(DEPTH):
        lw = layer_weights(i, norm_mix, w_in, conv_dw, conv_b, conv_ln_g, conv_ln_b, w_a_out, pool_w, pool_scale,
                           w_b_out, pe_cmp, w_cmp1, w_cmp2, w_c_out, w_o, norm_ffn, w_up, ffn_dw, w_down, norm_ple,
                           w_ple_gate, w_ple)
        xp, st = prompt_layer(xp, p_prompt[i], lw, tables)
        st_p.append(st)
        xs, st = sample_layer(xs, p_sample[i].transpose(1, 0, 2).reshape(ts * bs, PLE_DIM), lw, tables_s,
                              cache_all, page_table + i * n_pool, state_nsa_win[i],
                              state_conv[i], state_pool[i], state_ffn[i], ts)
        st_s.append(st)
    y_prompt = final_norm(xp.reshape(B * T, D_MODEL), norm_final).reshape(B, T, D_MODEL)
    y_sample = final_norm(xs, norm_final, tm=256).reshape(ts, bs, D_MODEL).transpose(1, 0, 2)
    stk = lambda lst, k: jnp.stack([s[k] for s in lst])
    roll = lambda state, rows: jnp.concatenate([state[:, :, ts:], rows], axis=2)
    win_rows = stk(st_s, 4).reshape((DEPTH, bs, ts) + state_nsa_win.shape[3:])
    return (y_prompt, y_sample, stk(st_p, 3), stk(st_p, 4), stk(st_p, 0), stk(st_p, 1), stk(st_p, 2),
            stk(st_s, 3), roll(state_nsa_win, win_rows), roll(state_conv, stk(st_s, 0)), roll(state_pool, stk(st_s, 1)),
            stk(st_s, 2))
```

```python
import functools
import numpy as np
import jax
import jax.numpy as jnp
from jax import lax
from jax.experimental import pallas as pl
from jax.experimental.pallas import tpu as pltpu

D_MODEL = 1024
DEPTH = 4
CONV_WIDTH = D_MODEL // 2
POOL_WIDTH = D_MODEL // 2
HEAD_DIM = 64
N_HEADS = (D_MODEL // 2) // HEAD_DIM
N_KV_HEADS = 2
GROUP = N_HEADS // N_KV_HEADS
ROT_DIM = HEAD_DIM // 4
ROPE_THETA = 500000.0
CONV_W = 31
POOL_WINDOWS = (2, 4, 8, 16)
POOL_MAX = max(POOL_WINDOWS)
POOL_GC = POOL_WIDTH // len(POOL_WINDOWS)
CMP_BLOCK = 32
CMP_STRIDE = 16
SEL_BLOCK = 64
N_SELECT = 16
WINDOW = 512
Q_BLOCK = 128
D_FF = ((8 * D_MODEL // 3 + 127) // 128) * 128
FFN_CONV_W = 3
PLE_DIM = 256
EPS = 1e-6
FORCE_SCORE = 1e4
NEG = -1e30

SPLIT_SIZES = (2 * CONV_WIDTH, POOL_WIDTH, N_HEADS * HEAD_DIM, 6 * N_KV_HEADS * HEAD_DIM, 3 * N_HEADS, 3 * D_MODEL)
SPLIT_AT = tuple(int(v) for v in np.cumsum(SPLIT_SIZES)[:-1])

F32 = jnp.float32
BF16 = jnp.bfloat16
LANES = 128
SUBLANES = 8
VMEM_LIMIT = 56 * 1024 * 1024
NT = (((1,), (1,)), ((), ()))
LOG2E = 1.4426950408889634

NG_PAD = LANES
C_A, C_B, C_Q, C_KV = 0, 2 * CONV_WIDTH, 2 * CONV_WIDTH + POOL_WIDTH, 2 * CONV_WIDTH + POOL_WIDTH + 512
C_NG = C_KV + 768
C_MG = C_NG + NG_PAD
IN_PACKED = C_MG + 3 * D_MODEL
KV_W = N_KV_HEADS * HEAD_DIM


def _cparams(*sem):
    return pltpu.CompilerParams(dimension_semantics=sem, vmem_limit_bytes=VMEM_LIMIT)


def _const_spec(shape):
    nd = len(shape)
    return pl.BlockSpec(shape, lambda *_: (0,) * nd, pipeline_mode=pl.Buffered(1))


def _rms(x, g):
    return x * lax.rsqrt(jnp.mean(x * x, axis=-1, keepdims=True) + EPS) * g


def _gelu(x):
    return 0.5 * x * (1.0 + jnp.tanh(np.sqrt(2.0 / np.pi).astype(np.float32) * (x + 0.044715 * (x * x * x))))


def _rope_tables(pos):
    half = ROT_DIM // 2
    inv = ROPE_THETA ** (-jnp.arange(half, dtype=F32) * 2.0 / ROT_DIM)
    ang = pos.astype(F32)[:, None] * inv[None, :]
    cos, sin = jnp.cos(ang), jnp.sin(ang)
    rows = pos.shape[0]
    ones = jnp.ones((rows, HEAD_DIM - ROT_DIM), F32)
    zeros = jnp.zeros((rows, HEAD_DIM - ROT_DIM), F32)
    z8 = jnp.zeros((rows, half), F32)
    cos_h = jnp.concatenate([cos, cos, ones], axis=1)
    sup_h = jnp.concatenate([-sin, z8, zeros], axis=1)
    sdn_h = jnp.concatenate([z8, sin, zeros], axis=1)
    rep = LANES // HEAD_DIM
    return jnp.tile(cos_h, (1, rep)), jnp.tile(sup_h, (1, rep)), jnp.tile(sdn_h, (1, rep))


def _rope128(x, cos, sup, sdn):
    return x * cos + pltpu.roll(x, LANES - ROT_DIM // 2, 1) * sup + pltpu.roll(x, ROT_DIM // 2, 1) * sdn


def _inproj_kernel(x_ref, g_ref, w_ref, cos_ref, sup_ref, sdn_ref,
                   a_ref, zb_ref, qh_ref, ql_ref, kv4_ref, win_ref, cmp_ref, slc_ref, winb_ref, ng_ref, mg_ref):
    h = _rms(x_ref[...], g_ref[...]).astype(BF16)

    def proj(c0, n):
        return jnp.dot(h, w_ref[:, c0:c0 + n], preferred_element_type=F32)

    za = proj(C_A, 2 * CONV_WIDTH)
    a_ref[...] = za[:, :CONV_WIDTH] * jax.nn.sigmoid(za[:, CONV_WIDTH:])
    zb_ref[...] = proj(C_B, POOL_WIDTH)
    cos, sup, sdn = cos_ref[...], sup_ref[...], sdn_ref[...]
    zq = proj(C_Q, N_HEADS * HEAD_DIM)
    scale = HEAD_DIM ** -0.5 * LOG2E
    for c in range(N_HEADS * HEAD_DIM // LANES):
        sl = slice(c * LANES, (c + 1) * LANES)
        qr = _rope128(zq[:, sl], cos, sup, sdn) * scale
        qh = qr.astype(BF16)
        qh_ref[:, sl] = qh
        ql_ref[:, sl] = (qr - qh.astype(F32)).astype(BF16)
    zkv = proj(C_KV, 6 * KV_W)
    kind = [zkv[:, j * KV_W:(j + 1) * KV_W] for j in range(6)]
    for j in (0, 2, 4):
        kind[j] = _rope128(kind[j], cos, sup, sdn)
    for j in range(4):
        kv4_ref[:, j * KV_W:(j + 1) * KV_W] = kind[j]
    cmp_ref[:, 0:KV_W] = kind[0]
    cmp_ref[:, KV_W:2 * KV_W] = kind[1]
    slc_ref[:, 0:KV_W] = kind[2].astype(BF16)
    slc_ref[:, KV_W:2 * KV_W] = kind[3].astype(BF16)
    win_ref[:, 0:KV_W] = kind[4]
    win_ref[:, KV_W:2 * KV_W] = kind[5]
    winb_ref[:, 0:KV_W] = kind[4].astype(BF16)
    winb_ref[:, KV_W:2 * KV_W] = kind[5].astype(BF16)
    ng_ref[...] = jax.nn.sigmoid(proj(C_NG, NG_PAD))
    mg_ref[...] = jax.nn.sigmoid(proj(C_MG, 3 * D_MODEL))


def _pack_w_in(w_in):
    parts = jnp.split(w_in, SPLIT_AT, axis=-1)
    ng = jnp.pad(parts[4], ((0, 0), (0, NG_PAD - parts[4].shape[1])))
    return jnp.concatenate([parts[0], parts[1], parts[2], parts[3], ng, parts[5]], axis=1).astype(BF16)


def inproj(x2, g, w_packed, tables, period, tm):
    n = x2.shape[0]
    pblocks = period // tm
    row = lambda c: pl.BlockSpec((tm, c), lambda i: (i, 0))
    tab = pl.BlockSpec((tm, LANES), lambda i: (i % pblocks, 0))
    outs = [(CONV_WIDTH, F32), (POOL_WIDTH, F32), (512, BF16), (512, BF16), (4 * KV_W, F32), (2 * KV_W, F32),
            (2 * KV_W, F32), (2 * KV_W, BF16), (2 * KV_W, BF16), (NG_PAD, F32), (3 * D_MODEL, F32)]
    return pl.pallas_call(
        _inproj_kernel,
        grid=(n // tm,),
        in_specs=[row(D_MODEL), _const_spec((1, D_MODEL)), _const_spec((D_MODEL, IN_PACKED)), tab, tab, tab],
        out_specs=[row(c) for c, _ in outs],
        out_shape=[jax.ShapeDtypeStruct((n, c), dt) for c, dt in outs],
        compiler_params=_cparams("parallel"),
        name="inproj",
    )(x2, g.reshape(1, D_MODEL), w_packed, *tables)


CONV_HALO = 32


def _conva_kernel(a_ref, w_ref, b_ref, g_ref, beta_ref, c_ref, ext_ref, sh_ref, *, tm, rc):
    @pl.when(pl.program_id(1) == 0)
    def _():
        ext_ref[0:CONV_HALO, :] = jnp.zeros((CONV_HALO, CONV_WIDTH), F32)

    ext_ref[CONV_HALO:CONV_HALO + tm, :] = a_ref[0]
    off = CONV_HALO - (CONV_W - 1)
    span = tm + CONV_HALO - SUBLANES
    for s in range(1, SUBLANES):
        sh_ref[s - 1, 0:span, :] = ext_ref[s:s + span, :]
    for r0 in range(0, tm, rc):
        acc = jnp.zeros((rc, CONV_WIDTH), F32)
        for k in range(CONV_W):
            s, base = (off + k) % SUBLANES, r0 + (off + k) // SUBLANES * SUBLANES
            tap = ext_ref[base:base + rc, :] if s == 0 else sh_ref[s - 1, base:base + rc, :]
            acc = acc + tap * w_ref[k:k + 1, :]
        y = acc + b_ref[...]
        mu = jnp.mean(y, axis=-1, keepdims=True)
        d = y - mu
        var = jnp.mean(d * d, axis=-1, keepdims=True)
        yn = d * lax.rsqrt(var + EPS) * g_ref[...] + beta_ref[...]
        c_ref[0, r0:r0 + rc, :] = (yn * jax.nn.sigmoid(yn)).astype(BF16)
    ext_ref[0:CONV_HALO, :] = ext_ref[tm:tm + CONV_HALO, :]


def conv_a(a3, conv_dw, conv_b, ln_g, ln_b, tm=256, rc=64):
    b, t, c = a3.shape
    vec = lambda v: v.reshape(1, c)
    return pl.pallas_call(
        functools.partial(_conva_kernel, tm=tm, rc=rc),
        grid=(b, t // tm),
        in_specs=[pl.BlockSpec((1, tm, c), lambda i, j: (i, j, 0)), _const_spec((CONV_W, c)),
                  _const_spec((1, c)), _const_spec((1, c)), _const_spec((1, c))],
        out_specs=pl.BlockSpec((1, tm, c), lambda i, j: (i, j, 0)),
        out_shape=jax.ShapeDtypeStruct((b, t, c), BF16),
        scratch_shapes=[pltpu.VMEM((tm + CONV_HALO, c), F32),
                        pltpu.VMEM((SUBLANES - 1, tm + CONV_HALO - SUBLANES, c), F32)],
        compiler_params=_cparams("arbitrary", "arbitrary"),
        name="conv_a",
    )(a3, conv_dw, vec(conv_b), vec(ln_g), vec(ln_b))


POOL_HALO = 16


def _pool_kernel(z_ref, pw_ref, sc_ref, pb_ref, ext_ref, *, tm, rc):
    i = pl.program_id(1)

    @pl.when(i == 0)
    def _():
        ext_ref[0:POOL_HALO, :] = jnp.zeros((POOL_HALO, POOL_WIDTH), F32)

    ext_ref[POOL_HALO:POOL_HALO + tm, :] = z_ref[0]
    for r0 in range(0, tm, rc):
        pos = (i * tm + r0 + lax.broadcasted_iota(jnp.int32, (rc, 1), 0)).astype(F32)
        for gi, w in enumerate(POOL_WINDOWS):
            lanes = slice(gi * POOL_GC, (gi + 1) * POOL_GC)
            base = POOL_HALO + r0
            cur = ext_ref[base:base + rc, lanes]
            s = cur
            for k in range(1, w):
                s = s + ext_ref[base - k:base - k + rc, lanes]
            cnt = jnp.minimum(pos + 1.0, float(w))
            d = s / cnt - cur
            y = jnp.dot(d.astype(BF16), pw_ref[gi], preferred_element_type=F32) * sc_ref[:, lanes]
            pb_ref[0, r0:r0 + rc, lanes] = y.astype(BF16)
    ext_ref[0:POOL_HALO, :] = ext_ref[tm:tm + POOL_HALO, :]


def pool_b(z3, pool_w, pool_scale, tm=512, rc=128):
    b, t, c = z3.shape
    return pl.pallas_call(
        functools.partial(_pool_kernel, tm=tm, rc=rc),
        grid=(b, t // tm),
        in_specs=[pl.BlockSpec((1, tm, c), lambda i, j: (i, j, 0)),
                  _const_spec((len(POOL_WINDOWS), POOL_GC, POOL_GC)), _const_spec((1, c))],
        out_specs=pl.BlockSpec((1, tm, c), lambda i, j: (i, j, 0)),
        out_shape=jax.ShapeDtypeStruct((b, t, c), BF16),
        scratch_shapes=[pltpu.VMEM((tm + POOL_HALO, c), F32)],
        compiler_params=_cparams("arbitrary", "arbitrary"),
        name="pool_b",
    )(z3, pool_w.astype(BF16), pool_scale.reshape(1, c))


SEG = CMP_STRIDE
CMP_LANES = 2 * KV_W


def _compress_weights(pe_c, w1, w2):
    eye = jnp.eye(2, dtype=F32)
    w1r = w1.reshape(2, CMP_BLOCK, HEAD_DIM, HEAD_DIM)
    def expand(w):
        return jnp.einsum('kjde,kl,gh->jkgdlhe', w, eye, eye).reshape(SEG * CMP_LANES, CMP_LANES).astype(BF16)
    wa, wb = expand(w1r[:, :SEG]), expand(w1r[:, SEG:])
    w2bd = jnp.einsum('kde,kl,gh->kgdlhe', w2, eye, eye).reshape(CMP_LANES, CMP_LANES).astype(BF16)
    def pe_row(p):
        return jnp.broadcast_to(p.transpose(1, 0, 2)[:, :, None, :], (SEG, 2, N_KV_HEADS, HEAD_DIM)).reshape(1, SEG * CMP_LANES)
    return wa, wb, w2bd, pe_row(pe_c[:, :SEG]), pe_row(pe_c[:, SEG:])


def _compress_kernel(x_ref, pea_ref, peb_ref, wa_ref, wb_ref, w2_ref, o_ref, bsc_ref, *, nseg):
    x = x_ref[0]
    a = jnp.dot((x + pea_ref[...]).astype(BF16), wa_ref[...], preferred_element_type=F32)
    bsc_ref[0:nseg, :] = jnp.dot((x + peb_ref[...]).astype(BF16), wb_ref[...], preferred_element_type=F32)
    bsc_ref[nseg:nseg + 8, :] = jnp.zeros((8, CMP_LANES), F32)
    u = a + bsc_ref[1:nseg + 1, :]
    o_ref[0] = jnp.dot(_gelu(u).astype(BF16), w2_ref[...], preferred_element_type=F32)


def compress_prompt(cmp3, cw):
    b, t, _ = cmp3.shape
    nseg = t // SEG
    wa, wb, w2bd, pea, peb = cw
    xs = cmp3.reshape(b, nseg, SEG * CMP_LANES)
    return pl.pallas_call(
        functools.partial(_compress_kernel, nseg=nseg),
        grid=(b,),
        in_specs=[pl.BlockSpec((1, nseg, SEG * CMP_LANES), lambda i: (i, 0, 0)),
                  _const_spec((1, SEG * CMP_LANES)), _const_spec((1, SEG * CMP_LANES)),
                  _const_spec((SEG * CMP_LANES, CMP_LANES)), _const_spec((SEG * CMP_LANES, CMP_LANES)),
                  _const_spec((CMP_LANES, CMP_LANES))],
        out_specs=pl.BlockSpec((1, nseg, CMP_LANES), lambda i: (i, 0, 0)),
        out_shape=jax.ShapeDtypeStruct((b, nseg, CMP_LANES), F32),
        scratch_shapes=[pltpu.VMEM((nseg + 8, CMP_LANES), F32)],
        compiler_params=_cparams("arbitrary"),
        name="compress",
    )(xs, pea, peb, wa, wb, w2bd)


QA = LANES + HEAD_DIM
Q3 = 3 * HEAD_DIM


def _softmax_rows(s, mask):
    s = jnp.where(mask, s, NEG)
    m = jnp.max(s, axis=-1, keepdims=True)
    e = jnp.where(mask, jnp.exp2(s - m), 0.0)
    l = jnp.sum(e, axis=-1, keepdims=True)
    return e * (1.0 / jnp.where(l > 0.0, l, 1.0))


KPAD = 2 * LANES
VT_ROWS = HEAD_DIM + 16


def _softmax_cols(s, mask):
    s = jnp.where(mask, s, NEG)
    m = jnp.max(s, axis=0, keepdims=True)
    e = jnp.where(mask, jnp.exp2(s - m), 0.0)
    l = jnp.sum(e, axis=0, keepdims=True)
    return e * (1.0 / jnp.where(l > 0.0, l, 1.0))


def _nsa_t_kernel(qh_ref, ql_ref, k_ref, vt_ref, wk0_ref, wk1_ref, wk2_ref, wv0_ref, wv1_ref, wv2_ref,
                  kc_ref, vct_ref, oht_ref, ng_ref, o_ref,
                  kaug_ref, kc3_ref, q3_ref, qaug_ref, m_ref, vta_ref, acc_ref, sc_ref, oc_ref, ow_ref,
                  *, t_len, tq, kc_len, ncmp):
    i = pl.program_id(1)
    nsel = t_len // SEL_BLOCK
    cols = GROUP * tq
    nseg = kc_ref.shape[1]

    @pl.when(i == 0)
    def _():
        blk = lax.broadcasted_iota(jnp.int32, (t_len, LANES), 0) // SEL_BLOCK
        lane = lax.broadcasted_iota(jnp.int32, (t_len, LANES), 1)
        onehot = jnp.where(blk == lane, 1.0, 0.0).astype(BF16)
        kc_all = kc_ref[0]
        for g in range(N_KV_HEADS):
            kaug_ref[g, :, 0:LANES] = onehot
            kaug_ref[g, :, LANES:QA] = k_ref[0, :, g * HEAD_DIM:(g + 1) * HEAD_DIM]
            kaug_ref[g, :, QA:KPAD] = jnp.zeros((t_len, KPAD - QA), BF16)
            vta_ref[g, 0:HEAD_DIM, :] = vt_ref[0, g * HEAD_DIM:(g + 1) * HEAD_DIM, :]
            one_row = lax.broadcasted_iota(jnp.int32, (VT_ROWS - HEAD_DIM, t_len), 0) == 0
            vta_ref[g, HEAD_DIM:VT_ROWS, :] = jnp.where(one_row, 1.0, 0.0).astype(BF16)
            kc = kc_all[:, g * HEAD_DIM:(g + 1) * HEAD_DIM]
            kch = kc.astype(BF16)
            kc3_ref[g, :, 0:HEAD_DIM] = kch
            kc3_ref[g, :, HEAD_DIM:2 * HEAD_DIM] = kch
            kc3_ref[g, :, 2 * HEAD_DIM:Q3] = (kc - kch.astype(F32)).astype(BF16)
            kc3_ref[g, :, Q3:KPAD] = jnp.zeros((nseg, KPAD - Q3), BF16)
        q3_ref[Q3:KPAD, :] = jnp.zeros((KPAD - Q3, cols), BF16)
        qaug_ref[QA:KPAD, :] = jnp.zeros((KPAD - QA, cols), BF16)

    s0 = i * tq
    pos = s0 + lax.broadcasted_iota(jnp.int32, (1, cols), 1) % tq
    c_last = s0 // kc_len

    for g in range(N_KV_HEADS):
        for r in range(GROUP):
            hs = slice((g * GROUP + r) * HEAD_DIM, (g * GROUP + r + 1) * HEAD_DIM)
            cs = slice(r * tq, (r + 1) * tq)
            qh = qh_ref[0, hs, :]
            q3_ref[0:HEAD_DIM, cs] = qh
            q3_ref[HEAD_DIM:2 * HEAD_DIM, cs] = ql_ref[0, hs, :]
            q3_ref[2 * HEAD_DIM:Q3, cs] = qh
            qaug_ref[LANES:QA, cs] = qh

        s = jnp.dot(kc3_ref[g], q3_ref[...], preferred_element_type=F32)
        n_idx = lax.broadcasted_iota(jnp.int32, (nseg, 1), 0)
        end = jnp.where(n_idx < ncmp, n_idx * CMP_STRIDE + (CMP_BLOCK - 1), t_len + CMP_BLOCK)
        p = _softmax_cols(s, end <= pos)
        vct = vct_ref[0, g * HEAD_DIM:(g + 1) * HEAD_DIM, :].astype(BF16)
        o_c = jnp.dot(vct, p.astype(BF16), preferred_element_type=F32)
        psum = p[:, 0:tq]
        for r in range(1, GROUP):
            psum = psum + p[:, r * tq:(r + 1) * tq]
        p_hi = psum.astype(BF16)
        p_lo = (psum - p_hi.astype(F32)).astype(BF16)
        v = (jnp.dot(oht_ref[...], p_hi, preferred_element_type=F32)
             + jnp.dot(oht_ref[...], p_lo, preferred_element_type=F32))

        jidx = lax.broadcasted_iota(jnp.int32, (LANES, tq), 0)
        qblk = (s0 + lax.broadcasted_iota(jnp.int32, (1, tq), 1)) // SEL_BLOCK
        forced = (jidx == 0) | (jidx == qblk) | (jidx == qblk - 1)
        v = jnp.where(jidx <= qblk, jnp.where(forced, FORCE_SCORE, v), -1.0)
        v = jnp.where(jidx < nsel, v, -3.0)
        sel = jnp.zeros((LANES, tq), F32)
        for _ in range(min(N_SELECT, nsel)):
            mx = jnp.max(v, axis=0, keepdims=True)
            first = jnp.min(jnp.where(v == mx, jidx, LANES), axis=0, keepdims=True)
            pick = jidx == first
            sel = jnp.where(pick, 1.0, sel)
            v = jnp.where(pick, -2.0, v)
        selneg = jnp.where(sel > 0.5, 0.0, NEG).astype(BF16)
        for r in range(GROUP):
            qaug_ref[0:LANES, r * tq:(r + 1) * tq] = selneg

        ks = slice(g * HEAD_DIM, (g + 1) * HEAD_DIM)
        kw = jnp.concatenate([w[0, :, ks] for w in (wk0_ref, wk1_ref, wk2_ref)], axis=0)
        vwt = jnp.concatenate([w[0, ks, :] for w in (wv0_ref, wv1_ref, wv2_ref)], axis=1)
        sw = jnp.dot(kw, q3_ref[0:HEAD_DIM, :], preferred_element_type=F32)
        pos_w = s0 - 2 * tq + lax.broadcasted_iota(jnp.int32, (3 * tq, 1), 0)
        mw = (pos_w <= pos) & (pos_w > pos - WINDOW) & (pos_w >= 0)
        pw = _softmax_cols(sw, mw)
        o_w = jnp.dot(vwt, pw.astype(BF16), preferred_element_type=F32)
        oc_ref[...] = o_c
        ow_ref[...] = o_w

        m_ref[...] = jnp.full((1, cols), NEG, F32)
        acc_ref[...] = jnp.zeros((VT_ROWS, cols), F32)

        def scores(c):
            k0 = pl.multiple_of(c * kc_len, kc_len)
            return jnp.dot(kaug_ref[g, pl.ds(k0, kc_len), :], qaug_ref[...], preferred_element_type=F32)

        def update(c, sc, cmax, causal):
            k0 = pl.multiple_of(c * kc_len, kc_len)
            if causal:
                kpos = k0 + lax.broadcasted_iota(jnp.int32, (kc_len, 1), 0)
                sc = jnp.where(kpos <= pos, sc, NEG)
                cmax = jnp.max(sc, axis=0, keepdims=True)
            m_prev = m_ref[...]
            m_new = jnp.maximum(m_prev, cmax)
            alpha = jnp.exp2(m_prev - m_new)
            pe = jnp.exp2(sc - m_new).astype(BF16)
            vt = vta_ref[g, :, pl.ds(k0, kc_len)]
            acc_ref[...] = alpha * acc_ref[...] + jnp.dot(vt, pe, preferred_element_type=F32)
            m_ref[...] = m_new

        sc_ref[0] = scores(0)

        def body(c, carry):
            slot = c % 2
            sc = sc_ref[slot]
            sc_ref[1 - slot] = scores(c + 1)
            update(c, sc, jnp.max(sc, axis=0, keepdims=True), False)
            return carry

        lax.fori_loop(0, c_last, body, 0)
        update(c_last, sc_ref[c_last % 2], None, True)
        o_s = acc_ref[0:HEAD_DIM, :] * (1.0 / acc_ref[HEAD_DIM:HEAD_DIM + 1, :])

        for r in range(GROUP):
            cs = slice(r * tq, (r + 1) * tq)
            c0 = (g * GROUP + r) * 3
            o = (ng_ref[0, c0:c0 + 1, :] * oc_ref[:, cs] + ng_ref[0, c0 + 1:c0 + 2, :] * o_s[:, cs]
                 + ng_ref[0, c0 + 2:c0 + 3, :] * ow_ref[:, cs])
            o_ref[0, (g * GROUP + r) * HEAD_DIM:(g * GROUP + r + 1) * HEAD_DIM, :] = o.astype(BF16)


def nsa_prompt_t(qh3, ql3, slc3, winb3, kcvc, ng3, tq=256, kc_len=512):
    b, t, _ = qh3.shape
    nseg = t // SEG
    ncmp = (t - CMP_BLOCK) // CMP_STRIDE + 1
    owner = (jnp.arange(nseg) * CMP_STRIDE) // SEL_BLOCK
    oht = (jnp.arange(LANES)[:, None] == owner[None, :]).astype(BF16)
    tr = lambda v: v.transpose(0, 2, 1)
    cols = GROUP * tq
    qspec = pl.BlockSpec((1, 512, tq), lambda i, j: (i, 0, j))
    wk = lambda back: pl.BlockSpec((1, tq, KV_W), lambda i, j: (i, jnp.maximum(j - back, 0), 0))
    wv = lambda back: pl.BlockSpec((1, KV_W, tq), lambda i, j: (i, 1, jnp.maximum(j - back, 0)))
    winbt = tr(winb3)
    out_t = pl.pallas_call(
        functools.partial(_nsa_t_kernel, t_len=t, tq=tq, kc_len=kc_len, ncmp=ncmp),
        grid=(b, t // tq),
        in_specs=[qspec, qspec,
                  pl.BlockSpec((1, t, KV_W), lambda i, j: (i, 0, 0)),
                  pl.BlockSpec((1, KV_W, t), lambda i, j: (i, 1, 0)),
                  wk(2), wk(1), wk(0), wv(2), wv(1), wv(0),
                  pl.BlockSpec((1, nseg, KV_W), lambda i, j: (i, 0, 0)),
                  pl.BlockSpec((1, KV_W, nseg), lambda i, j: (i, 1, 0)),
                  _const_spec((LANES, nseg)),
                  pl.BlockSpec((1, NG_PAD, tq), lambda i, j: (i, 0, j))],
        out_specs=qspec,
        out_shape=jax.ShapeDtypeStruct((b, 512, t), BF16),
        scratch_shapes=[pltpu.VMEM((N_KV_HEADS, t, KPAD), BF16), pltpu.VMEM((N_KV_HEADS, nseg, KPAD), BF16),
                        pltpu.VMEM((KPAD, cols), BF16), pltpu.VMEM((KPAD, cols), BF16),
                        pltpu.VMEM((1, cols), F32), pltpu.VMEM((N_KV_HEADS, VT_ROWS, t), BF16),
                        pltpu.VMEM((VT_ROWS, cols), F32),
                        pltpu.VMEM((2, kc_len, cols), F32), pltpu.VMEM((HEAD_DIM, cols), F32),
                        pltpu.VMEM((HEAD_DIM, cols), F32)],
        compiler_params=_cparams("arbitrary", "arbitrary"),
        name="nsa_prompt",
    )(tr(qh3), tr(ql3), slc3, tr(slc3), winb3, winb3, winb3, winbt, winbt, winbt, kcvc, tr(kcvc), oht, tr(ng3))
    return tr(out_t)


def _merge_kernel(c_ref, pb_ref, oc_ref, mg_ref, x_ref, wa_ref, wb_ref, wc_ref, wo_ref, o_ref):
    out_a = jnp.dot(c_ref[...], wa_ref[...], preferred_element_type=F32)
    out_b = jnp.dot(pb_ref[...], wb_ref[...], preferred_element_type=F32)
    out_c = jnp.dot(oc_ref[...], wc_ref[...], preferred_element_type=F32)
    m = (mg_ref[:, 0:D_MODEL] * out_a + mg_ref[:, D_MODEL:2 * D_MODEL] * out_b
         + mg_ref[:, 2 * D_MODEL:3 * D_MODEL] * out_c)
    o_ref[...] = x_ref[...] + jnp.dot(m.astype(BF16), wo_ref[...], preferred_element_type=F32)


def merge(c2, pb2, oc2, mg2, x2, wa, wb, wc, wo, tm=512):
    n = x2.shape[0]
    row = lambda c: pl.BlockSpec((tm, c), lambda i: (i, 0))
    return pl.pallas_call(
        _merge_kernel,
        grid=(n // tm,),
        in_specs=[row(512), row(512), row(512), row(3 * D_MODEL), row(D_MODEL),
                  _const_spec((512, D_MODEL)), _const_spec((512, D_MODEL)), _const_spec((512, D_MODEL)),
                  _const_spec((D_MODEL, D_MODEL))],
        out_specs=row(D_MODEL),
        out_shape=jax.ShapeDtypeStruct((n, D_MODEL), F32),
        compiler_params=_cparams("parallel"),
        name="merge",
    )(c2, pb2, oc2, mg2, x2, wa, wb, wc, wo)


FF_CHUNK = 256
FF_HALO = 8


def _ffn_kernel(x_ref, pe_ref, gf_ref, wup_ref, dw_ref, wdn_ref, gp_ref, wg_ref, wp_ref, o_ref, nf_ref,
                usc_ref, carry_ref, f_ref, *, tm):
    @pl.when(pl.program_id(1) == 0)
    def _():
        carry_ref[...] = jnp.zeros((FF_HALO, 2 * D_FF), F32)

    x = x_ref[0]
    h = _rms(x, gf_ref[...]).astype(BF16)
    for c in range(D_FF // FF_CHUNK):
        cu = []
        for half in range(2):
            col = half * D_FF + c * FF_CHUNK
            cs = slice(col, col + FF_CHUNK)
            u = jnp.dot(h, wup_ref[:, cs], preferred_element_type=F32)
            usc_ref[half, 0:FF_HALO, :] = carry_ref[:, cs]
            usc_ref[half, FF_HALO:FF_HALO + tm, :] = u
            carry_ref[:, cs] = u[tm - FF_HALO:tm]
            cu.append(usc_ref[half, FF_HALO - 2:FF_HALO - 2 + tm, :] * dw_ref[0:1, cs]
                      + usc_ref[half, FF_HALO - 1:FF_HALO - 1 + tm, :] * dw_ref[1:2, cs]
                      + u * dw_ref[2:3, cs])
        f_ref[:, c * FF_CHUNK:(c + 1) * FF_CHUNK] = (_gelu(cu[0]) * cu[1]).astype(BF16)
    acc = jnp.dot(f_ref[...], wdn_ref[...], preferred_element_type=F32)
    nf_ref[0] = carry_ref[FF_HALO - 2:FF_HALO, :]
    x2 = x + acc
    gate = jax.nn.sigmoid(jnp.dot(_rms(x2, gp_ref[...]).astype(BF16), wg_ref[...], preferred_element_type=F32))
    o_ref[0] = x2 + gate * jnp.dot(pe_ref[0].astype(BF16), wp_ref[...], preferred_element_type=F32)


def ffn_prompt(x3, pe3, g_ffn, w_up, ffn_dw, w_down, g_ple, w_gate, w_ple, tm=512):
    b, t, _ = x3.shape
    tile = lambda c: pl.BlockSpec((1, tm, c), lambda i, j: (i, j, 0))
    return pl.pallas_call(
        functools.partial(_ffn_kernel, tm=tm),
        grid=(b, t // tm),
        in_specs=[tile(D_MODEL), tile(PLE_DIM), _const_spec((1, D_MODEL)), _const_spec((D_MODEL, 2 * D_FF)),
                  _const_spec((FFN_CONV_W, 2 * D_FF)), _const_spec((D_FF, D_MODEL)), _const_spec((1, D_MODEL)),
                  _const_spec((D_MODEL, D_MODEL)), _const_spec((PLE_DIM, D_MODEL))],
        out_specs=[tile(D_MODEL), pl.BlockSpec((1, FFN_CONV_W - 1, 2 * D_FF), lambda i, j: (i, 0, 0))],
        out_shape=[jax.ShapeDtypeStruct((b, t, D_MODEL), F32),
                   jax.ShapeDtypeStruct((b, FFN_CONV_W - 1, 2 * D_FF), F32)],
        scratch_shapes=[pltpu.VMEM((2, tm + FF_HALO, FF_CHUNK), F32), pltpu.VMEM((FF_HALO, 2 * D_FF), F32),
                        pltpu.VMEM((tm, D_FF), BF16)],
        compiler_params=_cparams("arbitrary", "arbitrary"),
        name="ffn",
    )(x3, pe3, g_ffn.reshape(1, D_MODEL), w_up, ffn_dw, w_down, g_ple.reshape(1, D_MODEL), w_gate, w_ple)


def _norm_kernel(x_ref, g_ref, o_ref):
    o_ref[...] = _rms(x_ref[...], g_ref[...])


def final_norm(x2, g, tm=512):
    n = x2.shape[0]
    return pl.pallas_call(
        _norm_kernel,
        grid=(n // tm,),
        in_specs=[pl.BlockSpec((tm, D_MODEL), lambda i: (i, 0)), _const_spec((1, D_MODEL))],
        out_specs=pl.BlockSpec((tm, D_MODEL), lambda i: (i, 0)),
        out_shape=jax.ShapeDtypeStruct((n, D_MODEL), F32),
        compiler_params=_cparams("parallel"),
        name="final_norm",
    )(x2, g.reshape(1, D_MODEL))


def prompt_layer(x3, pe3, lw, tables):
    b, t, _ = x3.shape
    n = b * t
    (a, zb, qh, ql, kv4, win, cmpx, slc, winb, ng, mg) = inproj(
        x3.reshape(n, D_MODEL), lw['norm_mix'], lw['w_in'], tables, t, 512)
    r3 = lambda v: v.reshape(b, t, v.shape[-1])
    a3 = r3(a)
    zb3 = r3(zb)
    c = conv_a(a3, lw['conv_dw'], lw['conv_b'], lw['conv_ln_g'], lw['conv_ln_b'])
    pb = pool_b(zb3, lw['pool_w'], lw['pool_scale'])
    kcvc = compress_prompt(r3(cmpx), lw['cmp'])
    oc = nsa_prompt_t(r3(qh), r3(ql), r3(slc), r3(winb), kcvc, r3(ng))
    x1 = merge(c.reshape(n, 512), pb.reshape(n, 512), oc.reshape(n, 512), mg, x3.reshape(n, D_MODEL),
               lw['w_a_out'], lw['w_b_out'], lw['w_c_out'], lw['w_o'])
    x2, new_ffn = ffn_prompt(x1.reshape(b, t, D_MODEL), pe3, lw['norm_ffn'], lw['w_up'], lw['ffn_dw'], lw['w_down'],
                             lw['norm_ple'], lw['w_ple_gate'], lw['w_ple'])
    wp = min(WINDOW, t)
    new_kv = kv4.reshape(b, t, 4, N_KV_HEADS, HEAD_DIM)
    new_win = r3(win)[:, t - wp:].reshape(b, wp, 2, N_KV_HEADS, HEAD_DIM)
    new_conv = a3[:, t - (CONV_W - 1):]
    new_pool = zb3[:, t - (POOL_MAX - 1):]
    return x2, (new_conv, new_pool, new_ffn, new_kv, new_win)


def layer_weights(i, norm_mix, w_in, conv_dw, conv_b, conv_ln_g, conv_ln_b, w_a_out, pool_w, pool_scale, w_b_out,
                  pe_cmp, w_cmp1, w_cmp2, w_c_out, w_o, norm_ffn, w_up, ffn_dw, w_down, norm_ple, w_ple_gate, w_ple):
    return dict(norm_mix=norm_mix[i], w_in=_pack_w_in(w_in[i]), conv_dw=conv_dw[i], conv_b=conv_b[i],
                conv_ln_g=conv_ln_g[i], conv_ln_b=conv_ln_b[i], w_a_out=w_a_out[i].astype(BF16), pool_w=pool_w[i],
                pool_scale=pool_scale[i], w_b_out=w_b_out[i].astype(BF16),
                cmp=_compress_weights(pe_cmp[i], w_cmp1[i], w_cmp2[i]),
                cmp_s=_compress_weights_sample(pe_cmp[i], w_cmp1[i], w_cmp2[i]), w_c_out=w_c_out[i].astype(BF16),
                w_o=w_o[i].astype(BF16), norm_ffn=norm_ffn[i], w_up=w_up[i].astype(BF16), ffn_dw=ffn_dw[i],
                w_down=w_down[i].astype(BF16), norm_ple=norm_ple[i], w_ple_gate=w_ple_gate[i].astype(BF16),
                w_ple=w_ple[i].astype(BF16))


SB = 32


def _conva_s_kernel(st_ref, a_ref, w_ref, b_ref, g_ref, beta_ref, c_ref, *, t_new, bsz):
    hist = CONV_W - 1
    for t in range(t_new):
        for b0 in range(0, bsz, SB):
            acc = jnp.zeros((SB, CONV_WIDTH), F32)
            for k in range(CONV_W):
                j = t + k
                row = st_ref[j, b0:b0 + SB, :] if j < hist else a_ref[j - hist, b0:b0 + SB, :]
                acc = acc + row * w_ref[k:k + 1, :]
            y = acc + b_ref[...]
            mu = jnp.mean(y, axis=-1, keepdims=True)
            d = y - mu
            var = jnp.mean(d * d, axis=-1, keepdims=True)
            yn = d * lax.rsqrt(var + EPS) * g_ref[...] + beta_ref[...]
            c_ref[t, b0:b0 + SB, :] = (yn * jax.nn.sigmoid(yn)).astype(BF16)


def conv_a_sample(st_tm, a_tm, conv_dw, conv_b, ln_g, ln_b):
    t_new, bsz, c = a_tm.shape
    vec = lambda v: v.reshape(1, c)
    return pl.pallas_call(
        functools.partial(_conva_s_kernel, t_new=t_new, bsz=bsz),
        out_shape=jax.ShapeDtypeStruct((t_new, bsz, c), BF16),
        compiler_params=pltpu.CompilerParams(vmem_limit_bytes=VMEM_LIMIT),
        name="conv_a_sample",
    )(st_tm, a_tm, conv_dw, vec(conv_b), vec(ln_g), vec(ln_b))


def _pool_s_kernel(st_ref, z_ref, pw_ref, sc_ref, pb_ref, *, t_new, pos0):
    hist = POOL_MAX - 1
    for t in range(t_new):
        for gi, w in enumerate(POOL_WINDOWS):
            lanes = slice(gi * POOL_GC, (gi + 1) * POOL_GC)
            cur = z_ref[t, :, lanes]
            s = cur
            for k in range(1, w):
                j = hist + t - k
                s = s + (st_ref[j, :, lanes] if j < hist else z_ref[j - hist, :, lanes])
            cnt = float(min(pos0 + t + 1, w))
            d = s / cnt - cur
            y = jnp.dot(d.astype(BF16), pw_ref[gi], preferred_element_type=F32) * sc_ref[:, lanes]
            pb_ref[t, :, lanes] = y.astype(BF16)


def pool_b_sample(st_tm, z_tm, pool_w, pool_scale, pos0):
    t_new, bsz, c = z_tm.shape
    return pl.pallas_call(
        functools.partial(_pool_s_kernel, t_new=t_new, pos0=pos0),
        out_shape=jax.ShapeDtypeStruct((t_new, bsz, c), BF16),
        compiler_params=pltpu.CompilerParams(vmem_limit_bytes=VMEM_LIMIT),
        name="pool_b_sample",
    )(st_tm, z_tm, pool_w.astype(BF16), pool_scale.reshape(1, c))


def _ffn_s_kernel(x_ref, pe_ref, gf_ref, wv_ref, wg_ref, dwv_ref, dwg_ref, wdn_ref, stv_ref, stg_ref,
                  gp_ref, wgate_ref, wp_ref, o_ref, nfv_ref, nfg_ref, h_ref, acc_ref, *, t_new, bsz):
    c = pl.program_id(0)

    @pl.when(c == 0)
    def _():
        h_ref[...] = _rms(x_ref[...], gf_ref[...]).astype(BF16)
        acc_ref[...] = jnp.zeros(acc_ref.shape, F32)

    def conv_half(w_ref, dw_ref, st_ref, nf_ref):
        u = jnp.dot(h_ref[...], w_ref[...], preferred_element_type=F32)
        ext = [st_ref[0], st_ref[1]] + [u[t * bsz:(t + 1) * bsz] for t in range(t_new)]
        nf_ref[0] = ext[t_new]
        nf_ref[1] = ext[t_new + 1]
        return jnp.concatenate(
            [ext[t] * dw_ref[0:1, :] + ext[t + 1] * dw_ref[1:2, :] + ext[t + 2] * dw_ref[2:3, :] for t in range(t_new)],
            axis=0)

    cv = conv_half(wv_ref, dwv_ref, stv_ref, nfv_ref)
    cg = conv_half(wg_ref, dwg_ref, stg_ref, nfg_ref)
    f = _gelu(cv) * cg
    acc_ref[...] += jnp.dot(f.astype(BF16), wdn_ref[...], preferred_element_type=F32)

    @pl.when(c == pl.num_programs(0) - 1)
    def _():
        x2 = x_ref[...] + acc_ref[...]
        gate = jax.nn.sigmoid(jnp.dot(_rms(x2, gp_ref[...]).astype(BF16), wgate_ref[...], preferred_element_type=F32))
        o_ref[...] = x2 + gate * jnp.dot(pe_ref[...].astype(BF16), wp_ref[...], preferred_element_type=F32)


def ffn_sample(x2, pe2, st_tm, g_ffn, w_up, ffn_dw, w_down, g_ple, w_gate, w_ple, t_new):
    n = x2.shape[0]
    bsz = n // t_new
    nch = D_FF // FF_CHUNK
    full = lambda r, c: pl.BlockSpec((r, c), lambda i: (0, 0))
    return pl.pallas_call(
        functools.partial(_ffn_s_kernel, t_new=t_new, bsz=bsz),
        grid=(nch,),
        in_specs=[full(n, D_MODEL), full(n, PLE_DIM), full(1, D_MODEL),
                  pl.BlockSpec((D_MODEL, FF_CHUNK), lambda i: (0, i)),
                  pl.BlockSpec((D_MODEL, FF_CHUNK), lambda i: (0, nch + i)),
                  pl.BlockSpec((FFN_CONV_W, FF_CHUNK), lambda i: (0, i)),
                  pl.BlockSpec((FFN_CONV_W, FF_CHUNK), lambda i: (0, nch + i)),
                  pl.BlockSpec((FF_CHUNK, D_MODEL), lambda i: (i, 0)),
                  pl.BlockSpec((FFN_CONV_W - 1, bsz, FF_CHUNK), lambda i: (0, 0, i)),
                  pl.BlockSpec((FFN_CONV_W - 1, bsz, FF_CHUNK), lambda i: (0, 0, nch + i)),
                  full(1, D_MODEL), full(D_MODEL, D_MODEL), full(PLE_DIM, D_MODEL)],
        out_specs=[full(n, D_MODEL),
                   pl.BlockSpec((FFN_CONV_W - 1, bsz, FF_CHUNK), lambda i: (0, 0, i)),
                   pl.BlockSpec((FFN_CONV_W - 1, bsz, FF_CHUNK), lambda i: (0, 0, i))],
        out_shape=[jax.ShapeDtypeStruct((n, D_MODEL), F32),
                   jax.ShapeDtypeStruct((FFN_CONV_W - 1, bsz, D_FF), F32),
                   jax.ShapeDtypeStruct((FFN_CONV_W - 1, bsz, D_FF), F32)],
        scratch_shapes=[pltpu.VMEM((n, D_MODEL), BF16), pltpu.VMEM((n, D_MODEL), F32)],
        compiler_params=_cparams("arbitrary"),
        name="ffn_sample",
    )(x2, pe2, g_ffn.reshape(1, D_MODEL), w_up, w_up, ffn_dw, ffn_dw, w_down, st_tm, st_tm,
      g_ple.reshape(1, D_MODEL), w_gate, w_ple)


PAGE = 128
SEG_PER_PAGE = PAGE // SEG


def _compress_weights_sample(pe_c, w1, w2):
    eye = jnp.eye(2, dtype=F32)
    w1r = w1.reshape(2, CMP_BLOCK, HEAD_DIM, HEAD_DIM)
    def expand(w):
        return jnp.einsum('kjde,kl,gh->jkgdlhe', w, eye, eye).reshape(SEG, CMP_LANES, CMP_LANES).astype(BF16)
    w2bd = jnp.einsum('kde,kl,gh->kgdlhe', w2, eye, eye).reshape(CMP_LANES, CMP_LANES).astype(BF16)
    def pe_rows(p):
        return jnp.broadcast_to(p.transpose(1, 0, 2)[:, :, None, :], (SEG, 2, N_KV_HEADS, HEAD_DIM)).reshape(SEG, CMP_LANES)
    return expand(w1r[:, :SEG]), expand(w1r[:, SEG:]), w2bd, pe_rows(pe_c[:, :SEG]), pe_rows(pe_c[:, SEG:])


def _compress_s_kernel(pt_ref, *refs, n_pages, bt):
    del pt_ref
    np_all = bt * n_pages
    pages_k, pages_v = refs[:np_all], refs[np_all:2 * np_all]
    pea_ref, peb_ref, wa_ref, wb_ref, w2_ref, o_ref, bsc_ref = refs[2 * np_all:]
    m = bt * n_pages * SEG_PER_PAGE
    a = jnp.zeros((m, CMP_LANES), F32)
    bm = jnp.zeros((m, CMP_LANES), F32)
    for j in range(SEG):
        xj = jnp.concatenate(
            [jnp.concatenate([pg[pl.ds(j, SEG_PER_PAGE, stride=SEG), :] for pg in pgs], axis=0)
             for pgs in (pages_k, pages_v)], axis=1)
        a = a + jnp.dot((xj + pea_ref[j:j + 1, :]).astype(BF16), wa_ref[j], preferred_element_type=F32)
        bm = bm + jnp.dot((xj + peb_ref[j:j + 1, :]).astype(BF16), wb_ref[j], preferred_element_type=F32)
    bsc_ref[0:m, :] = bm
    bsc_ref[m:m + 8, :] = jnp.zeros((8, CMP_LANES), F32)
    u = a + bsc_ref[1:m + 1, :]
    out = jnp.dot(_gelu(u).astype(BF16), w2_ref[...], preferred_element_type=F32)
    nseg = n_pages * SEG_PER_PAGE
    for q in range(bt):
        o_ref[q] = out[q * nseg:(q + 1) * nseg]


def compress_sample(cache2, page_table, cw, bt=4):
    bsz, n_pages = page_table.shape
    nseg = n_pages * SEG_PER_PAGE
    wa, wb, w2bd, pea, peb = cw
    page_specs = [pl.BlockSpec((None, PAGE, KV_W),
                               functools.partial(lambda i, pt, q, p, kind: (pt[i * bt + q, p], 0, kind), q=q, p=p, kind=kind))
                  for kind in range(2) for q in range(bt) for p in range(n_pages)]
    const = lambda shape: pl.BlockSpec(shape, lambda i, pt: (0,) * len(shape))
    grid_spec = pltpu.PrefetchScalarGridSpec(
        num_scalar_prefetch=1, grid=(bsz // bt,),
        in_specs=page_specs + [const((SEG, CMP_LANES)), const((SEG, CMP_LANES)), const((SEG, CMP_LANES, CMP_LANES)),
                               const((SEG, CMP_LANES, CMP_LANES)), const((CMP_LANES, CMP_LANES))],
        out_specs=pl.BlockSpec((bt, nseg, CMP_LANES), lambda i, pt: (i, 0, 0)),
        scratch_shapes=[pltpu.VMEM((bt * nseg + 8, CMP_LANES), F32)])
    return pl.pallas_call(
        functools.partial(_compress_s_kernel, n_pages=n_pages, bt=bt),
        grid_spec=grid_spec,
        out_shape=jax.ShapeDtypeStruct((bsz, nseg, CMP_LANES), F32),
        compiler_params=_cparams("arbitrary"),
        name="compress_sample",
    )(page_table, *([cache2] * (2 * bt * n_pages)), pea, peb, wa, wb, w2bd)


NEW_PAD = 16


def _nsa_s_kernel(pt_ref, *refs, n_pages, t_new, win_len):
    del pt_ref
    pages = refs[:n_pages]
    (q3_ref, ngr_ref, kcvc_ref, nslc_ref, wst_ref, nwin_ref, oht_ref, o_ref,
     kaug_ref, vsel_ref, kw_ref, vw_ref, kc3_ref, qaug_ref) = refs[n_pages:]
    past = n_pages * PAGE
    kl = kaug_ref.shape[1]
    wl = kw_ref.shape[1]
    rows = GROUP * t_new
    nseg = kcvc_ref.shape[1]
    ncmp = (past + t_new - CMP_BLOCK) // CMP_STRIDE + 1
    nsel = -(-(past + t_new) // SEL_BLOCK)

    @pl.when(pl.program_id(0) == 0)
    def _():
        blk = lax.broadcasted_iota(jnp.int32, (kl, LANES), 0) // SEL_BLOCK
        lane = lax.broadcasted_iota(jnp.int32, (kl, LANES), 1)
        onehot = jnp.where(blk == lane, 1.0, 0.0).astype(BF16)
        for g in range(N_KV_HEADS):
            kaug_ref[g, :, 0:LANES] = onehot
            kaug_ref[g, past:kl, LANES:QA] = jnp.zeros((kl - past, HEAD_DIM), BF16)
            vsel_ref[g, past:kl, :] = jnp.zeros((kl - past, HEAD_DIM), BF16)
            kw_ref[g, win_len:wl, :] = jnp.zeros((wl - win_len, HEAD_DIM), BF16)
            vw_ref[g, win_len:wl, :] = jnp.zeros((wl - win_len, HEAD_DIM), BF16)

    pos4 = past + lax.broadcasted_iota(jnp.int32, (rows, 1), 0) % t_new
    kcvc = kcvc_ref[0]
    nslc = nslc_ref[0]
    nwin = nwin_ref[0]
    wst = wst_ref[0]
    outs = []
    for g in range(N_KV_HEADS):
        ks = slice(g * HEAD_DIM, (g + 1) * HEAD_DIM)
        vs = slice(KV_W + g * HEAD_DIM, KV_W + (g + 1) * HEAD_DIM)
        for p in range(n_pages):
            kaug_ref[g, p * PAGE:(p + 1) * PAGE, LANES:QA] = pages[p][:, ks].astype(BF16)
            vsel_ref[g, p * PAGE:(p + 1) * PAGE, :] = pages[p][:, vs].astype(BF16)
        kaug_ref[g, past:past + NEW_PAD, LANES:QA] = nslc[:, ks]
        vsel_ref[g, past:past + NEW_PAD, :] = nslc[:, vs]
        kw_ref[g, 0:win_len, :] = wst[:, ks].astype(BF16)
        vw_ref[g, 0:win_len, :] = wst[:, vs].astype(BF16)
        kw_ref[g, win_len:win_len + NEW_PAD, :] = nwin[:, ks]
        vw_ref[g, win_len:win_len + NEW_PAD, :] = nwin[:, vs]
        kc = kcvc[:, ks]
        kch = kc.astype(BF16)
        kc3_ref[g, :, 0:HEAD_DIM] = kch
        kc3_ref[g, :, HEAD_DIM:2 * HEAD_DIM] = kch
        kc3_ref[g, :, 2 * HEAD_DIM:Q3] = (kc - kch.astype(F32)).astype(BF16)
        vc = kcvc[:, vs].astype(BF16)
        q3 = q3_ref[0, g]

        s = lax.dot_general(q3, kc3_ref[g], NT, preferred_element_type=F32)
        n_idx = lax.broadcasted_iota(jnp.int32, (1, nseg), 1)
        end = n_idx * CMP_STRIDE + (CMP_BLOCK - 1)
        mask = jnp.where(n_idx < ncmp, end, past + t_new + CMP_BLOCK) <= pos4
        p = _softmax_rows(s, mask)
        o_c = jnp.dot(p.astype(BF16), vc, preferred_element_type=F32)
        ppad = jnp.concatenate([p, jnp.zeros((LANES - rows, nseg), F32)], axis=0)
        p_hi = ppad.astype(BF16)
        p_lo = (ppad - p_hi.astype(F32)).astype(BF16)
        x = (lax.dot_general(oht_ref[...], p_hi, NT, preferred_element_type=F32)
             + lax.dot_general(oht_ref[...], p_lo, NT, preferred_element_type=F32))
        v = x
        for r in range(1, GROUP):
            v = v + pltpu.roll(x, LANES - r * t_new, 1)

        jidx = lax.broadcasted_iota(jnp.int32, (LANES, LANES), 0)
        qblk = (past + lax.broadcasted_iota(jnp.int32, (1, LANES), 1) % t_new) // SEL_BLOCK
        forced = (jidx == 0) | (jidx == qblk) | (jidx == qblk - 1)
        v = jnp.where(jidx <= qblk, jnp.where(forced, FORCE_SCORE, v), -1.0)
        v = jnp.where(jidx < nsel, v, -3.0)
        sel = jnp.zeros((LANES, LANES), F32)
        for _ in range(min(N_SELECT, nsel)):
            mx = jnp.max(v, axis=0, keepdims=True)
            first = jnp.min(jnp.where(v == mx, jidx, LANES), axis=0, keepdims=True)
            pick = jidx == first
            sel = jnp.where(pick, 1.0, sel)
            v = jnp.where(pick, -2.0, v)
        ri = lax.broadcasted_iota(jnp.int32, (rows, LANES), 0) % t_new
        li = lax.broadcasted_iota(jnp.int32, (rows, LANES), 1)
        spread = jnp.where(ri == li, 1.0, 0.0).astype(BF16)
        sel_rows = lax.dot_general(spread, sel.astype(BF16), NT, preferred_element_type=F32)
        qaug_ref[g, :, 0:LANES] = jnp.where(sel_rows > 0.5, 0.0, NEG).astype(BF16)
        qaug_ref[g, :, LANES:QA] = q3[:, 0:HEAD_DIM]

        sc = lax.dot_general(qaug_ref[g], kaug_ref[g], NT, preferred_element_type=F32)
        kpos = lax.broadcasted_iota(jnp.int32, (1, kl), 1)
        ps = _softmax_rows(sc, kpos <= pos4)
        o_s = jnp.dot(ps.astype(BF16), vsel_ref[g], preferred_element_type=F32)

        sw = lax.dot_general(q3[:, 0:HEAD_DIM], kw_ref[g], NT, preferred_element_type=F32)
        widx = lax.broadcasted_iota(jnp.int32, (1, wl), 1)
        pos_w = past - win_len + widx
        mw = (widx < win_len + t_new) & (pos_w <= pos4) & (pos_w > pos4 - WINDOW) & (pos_w >= 0)
        pw = _softmax_rows(sw, mw)
        o_w = jnp.dot(pw.astype(BF16), vw_ref[g], preferred_element_type=F32)

        ngr = ngr_ref[0, g]
        outs.append(ngr[:, 0:1] * o_c + ngr[:, 1:2] * o_s + ngr[:, 2:3] * o_w)
    o_ref[0] = jnp.stack(outs, axis=0)


def nsa_sample_attn(cache2, page_table, q3, ngr, kcvc, nslc, wst, nwin):
    bsz, n_pages = page_table.shape
    t_new = q3.shape[2] // GROUP
    rows = GROUP * t_new
    past = n_pages * PAGE
    win_len = wst.shape[1]
    nseg = kcvc.shape[1]
    kl = past + LANES
    wl = win_len + LANES
    owner = (jnp.arange(nseg) * CMP_STRIDE) // SEL_BLOCK
    oht = (jnp.arange(LANES)[:, None] == owner[None, :]).astype(BF16)
    page_specs = [pl.BlockSpec((None, PAGE, 2 * KV_W), functools.partial(lambda i, pt, p: (pt[i, p], 0, 1), p=p))
                  for p in range(n_pages)]
    per_b = lambda *shape: pl.BlockSpec((1,) + shape, lambda i, pt: (i,) + (0,) * len(shape))
    grid_spec = pltpu.PrefetchScalarGridSpec(
        num_scalar_prefetch=1, grid=(bsz,),
        in_specs=page_specs + [per_b(N_KV_HEADS, rows, Q3), per_b(N_KV_HEADS, rows, LANES), per_b(nseg, CMP_LANES),
                               per_b(NEW_PAD, 2 * KV_W), per_b(win_len, 2 * KV_W), per_b(NEW_PAD, 2 * KV_W),
                               pl.BlockSpec((LANES, nseg), lambda i, pt: (0, 0))],
        out_specs=per_b(N_KV_HEADS, rows, HEAD_DIM),
        scratch_shapes=[pltpu.VMEM((N_KV_HEADS, kl, QA), BF16), pltpu.VMEM((N_KV_HEADS, kl, HEAD_DIM), BF16),
                        pltpu.VMEM((N_KV_HEADS, wl, HEAD_DIM), BF16), pltpu.VMEM((N_KV_HEADS, wl, HEAD_DIM), BF16),
                        pltpu.VMEM((N_KV_HEADS, nseg, Q3), BF16), pltpu.VMEM((N_KV_HEADS, rows, QA), BF16)])
    return pl.pallas_call(
        functools.partial(_nsa_s_kernel, n_pages=n_pages, t_new=t_new, win_len=win_len),
        grid_spec=grid_spec,
        out_shape=jax.ShapeDtypeStruct((bsz, N_KV_HEADS, rows, HEAD_DIM), F32),
        compiler_params=_cparams("arbitrary"),
        name="nsa_sample",
    )(page_table, *([cache2] * n_pages), q3, ngr, kcvc, nslc, wst, nwin, oht)


def sample_layer(x2, pe2, lw, tables, cache2, page_table, st_win, st_conv, st_pool, st_ffn, t_new):
    n = x2.shape[0]
    bsz = n // t_new
    past = page_table.shape[1] * PAGE
    (a, zb, qh, ql, kv4, win, _, slc, winb, ng, mg) = inproj(x2, lw['norm_mix'], lw['w_in'], tables, n, 256)
    tm3 = lambda v: v.reshape(t_new, bsz, v.shape[-1])
    bm3 = lambda v: tm3(v).transpose(1, 0, 2)
    c = conv_a_sample(st_conv.transpose(1, 0, 2), tm3(a), lw['conv_dw'], lw['conv_b'], lw['conv_ln_g'], lw['conv_ln_b'])
    pb = pool_b_sample(st_pool.transpose(1, 0, 2), tm3(zb), lw['pool_w'], lw['pool_scale'], past)
    kcvc = compress_sample(cache2, page_table, lw['cmp_s'])

    def heads(v):
        return v.reshape(t_new, bsz, N_KV_HEADS, GROUP, HEAD_DIM).transpose(1, 2, 3, 0, 4).reshape(
            bsz, N_KV_HEADS, GROUP * t_new, HEAD_DIM)
    qh4, ql4 = heads(qh), heads(ql)
    q3 = jnp.concatenate([qh4, ql4, qh4], axis=-1)
    ngr = ng[:, :3 * N_HEADS].reshape(t_new, bsz, N_KV_HEADS, GROUP, 3).transpose(1, 2, 3, 0, 4).reshape(
        bsz, N_KV_HEADS, GROUP * t_new, 3)
    ngr = jnp.pad(ngr, ((0, 0), (0, 0), (0, 0), (0, LANES - 3)))
    pad_new = lambda v: jnp.pad(bm3(v), ((0, 0), (0, NEW_PAD - t_new), (0, 0)))
    wst = st_win.reshape(bsz, st_win.shape[1], 2 * KV_W)
    oc = nsa_sample_attn(cache2, page_table, q3, ngr, kcvc, pad_new(slc), wst, pad_new(winb))
    oc2 = oc.reshape(bsz, N_KV_HEADS, GROUP, t_new, HEAD_DIM).transpose(3, 0, 1, 2, 4).reshape(n, 512).astype(BF16)
    x1 = merge(c.reshape(n, 512), pb.reshape(n, 512), oc2, mg, x2,
               lw['w_a_out'], lw['w_b_out'], lw['w_c_out'], lw['w_o'], tm=256)
    x3, nfv, nfg = ffn_sample(x1, pe2, st_ffn.transpose(1, 0, 2), lw['norm_ffn'], lw['w_up'], lw['ffn_dw'], lw['w_down'],
                              lw['norm_ple'], lw['w_ple_gate'], lw['w_ple'], t_new)
    new_kv = bm3(kv4).reshape(bsz, t_new, 4, N_KV_HEADS, HEAD_DIM)
    new_ffn = jnp.concatenate([nfv, nfg], axis=-1).transpose(1, 0, 2)
    return x3, (bm3(a), bm3(zb), new_ffn, new_kv, bm3(win))


def kernel(x_prompt, x_sample, cache_nsa_kv, state_nsa_win, state_conv, state_pool, state_ffn, page_table,
           p_prompt, p_sample, norm_mix, w_in, conv_dw, conv_b, conv_ln_g, conv_ln_b, w_a_out, pool_w,
           pool_scale, w_b_out, pe_cmp, w_cmp1, w_cmp2, w_c_out, w_o, norm_ffn, w_up, ffn_dw, w_down,
           norm_ple, w_ple_gate, w_ple, norm_final):
    B, T, _ = x_prompt.shape
    bs, ts, _ = x_sample.shape
    n_pool, page = cache_nsa_kv.shape[1:3]
    assert page == PAGE and T % 512 == 0 and T // SEL_BLOCK <= LANES and (bs * ts) % 256 == 0 and bs % SB == 0
    past_len = page_table.shape[1] * page
    tables = _rope_tables(jnp.arange(T))
    tables_s = _rope_tables(past_len + jnp.repeat(jnp.arange(ts), bs))
    xp = x_prompt
    xs = x_sample.transpose(1, 0, 2).reshape(ts * bs, D_MODEL)
    cache_all = cache_nsa_kv.reshape(DEPTH * n_pool, page, 4 * KV_W)
    st_p, st_s = [], []
    for i in range(DEPTH):
        lw = layer_weights(i, norm_mix, w_in, conv_dw, conv_b, conv_ln_g, conv_ln_b, w_a_out, pool_w, pool_scale,
                           w_b_out, pe_cmp, w_cmp1, w_cmp2, w_c_out, w_o, norm_ffn, w_up, ffn_dw, w_down, norm_ple,
                           w_ple_gate, w_ple)
        xp, st = prompt_layer(xp, p_prompt[i], lw, tables)
        st_p.append(st)
        xs, st = sample_layer(xs, p_sample[i].transpose(1, 0, 2).reshape(ts * bs, PLE_DIM), lw, tables_s,
                              cache_all, page_table + i * n_pool, state_nsa_win[i],
                              state_conv[i], state_pool[i], state_ffn[i], ts)
        st_s.append(st)
    y_prompt = final_norm(xp.reshape(B * T, D_MODEL), norm_final).reshape(B, T, D_MODEL)
    y_sample = final_norm(xs, norm_final, tm=256).reshape(ts, bs, D_MODEL).transpose(1, 0, 2)
    stk = lambda lst, k: jnp.stack([s[k] for s in lst])
    roll = lambda state, rows: jnp.concatenate([state[:, :, ts:], rows], axis=2)
    win_rows = stk(st_s, 4).reshape((DEPTH, bs, ts) + state_nsa_win.shape[3:])
    return (y_prompt, y_sample, stk(st_p, 3), stk(st_p, 4), stk(st_p, 0), stk(st_p, 1), stk(st_p, 2),
            stk(st_s, 3), roll(state_nsa_win, win_rows), roll(state_conv, stk(st_s, 0)), roll(state_pool, stk(st_s, 1)),
            stk(st_s, 2))
```
